```python
import jax, jax.numpy as jnp
from jax import lax
import numpy as np

D_MODEL = 1024
BATCH = 8
SEQ = 2048
DEPTH = 2

CHUNK = 64
Q_BLOCK = 128
N_MIXERS = 2
EPS = 1e-6
A_HEADS = 8
A_LAT = 128
IDX_HEADS = 8
IDX_DIM = 64
TOPK_MAX = 256
A_Q_COLS = A_HEADS * A_LAT
A_IN_COLS = A_Q_COLS + A_LAT + IDX_HEADS * IDX_DIM + IDX_DIM + IDX_HEADS
B_HEADS = D_MODEL // 256
B_DK = D_MODEL // B_HEADS
B_DV = 2 * B_DK
B_IN_COLS = 2 * D_MODEL + 2 * (B_HEADS * B_DV)
ROPE_BASE = 10000.0
D_FF = 2816
CONV_W = 3
N_A = (DEPTH + N_MIXERS - 1) // N_MIXERS
N_B = DEPTH // N_MIXERS

kernel_name = 'hybrid_dsa_retention_convffn'


def rms_norm(x, g):
    xf = x.astype(jnp.float32)
    y = xf * lax.rsqrt(jnp.mean(xf * xf, axis=-1, keepdims=True) + EPS)
    return (y * g.astype(jnp.float32)).astype(x.dtype)


def dsa_mixer(h, w_in, q_g, k_g, iq_g, ik_g, w_out):
    bsz, seq, _ = h.shape
    topk = min(TOPK_MAX, seq // 4)
    proj = h @ w_in
    o1 = A_Q_COLS
    o2 = o1 + A_LAT
    o3 = o2 + IDX_HEADS * IDX_DIM
    o4 = o3 + IDX_DIM
    q = proj[..., :o1].reshape(bsz, seq, A_HEADS, A_LAT)
    c = proj[..., o1:o2]
    qi = proj[..., o2:o3].reshape(bsz, seq, IDX_HEADS, IDX_DIM)
    ki = rms_norm(proj[..., o3:o4], ik_g)
    wi = proj[..., o4:] * (IDX_HEADS ** -0.5 * IDX_DIM ** -0.5)
    q = rms_norm(q, q_g) * (A_LAT ** -0.5)
    k = rms_norm(c, k_g)
    qi = rms_norm(qi, iq_g)

    nb = seq // Q_BLOCK

    def to_blocks(a):
        return a.reshape(bsz, nb, Q_BLOCK, *a.shape[2:]).swapaxes(0, 1)

    t_blk = jnp.arange(seq, dtype=jnp.int32).reshape(nb, Q_BLOCK)
    key_pos = jnp.arange(seq, dtype=jnp.int32)
    gather = jax.vmap(lambda a, idx: a[idx])

    def block(args):
        qb, qib, wb, tb = args
        limit = (tb // CHUNK + 1) * CHUNK
        admissible = key_pos[None, :] < limit[:, None]
        rel = jax.nn.relu(jnp.einsum('bqhd,bsd->bqhs', qib, ki).astype(jnp.float32))
        score = jnp.einsum('bqhs,bqh->bqs', rel, wb.astype(jnp.float32))
        score = jnp.where(admissible[None], score, -jnp.inf)
        _, idx = lax.top_k(score, topk)
        valid = idx < limit[None, :, None]
        k_sel = gather(k, idx)
        v_sel = gather(c, idx)
        logits = jnp.einsum('bqhd,bqkd->bqhk', qb, k_sel).astype(jnp.float32)
        logits = jnp.where(valid[:, :, None, :], logits, -jnp.inf)
        p = jax.nn.softmax(logits, axis=-1).astype(v_sel.dtype)
        return jnp.einsum('bqhk,bqkd->bqhd', p, v_sel)

    o = lax.map(block, (to_blocks(q), to_blocks(qi), to_blocks(wi), t_blk))
    o = o.swapaxes(0, 1).reshape(bsz, seq, A_Q_COLS)
    return o @ w_out


def rotary(x):
    seq, d = x.shape[1], x.shape[-1]
    inv = 1.0 / (ROPE_BASE ** (jnp.arange(0, d, 2, dtype=jnp.float32) / d))
    ang = jnp.arange(seq, dtype=jnp.float32)[:, None] * inv[None, :]
    cos = jnp.cos(ang)[None, :, None, :]
    sin = jnp.sin(ang)[None, :, None, :]
    xf = x.astype(jnp.float32)
    x1, x2 = xf[..., : d // 2], xf[..., d // 2:]
    return jnp.concatenate([x1 * cos - x2 * sin, x1 * sin + x2 * cos], axis=-1)


def retention_mixer(h, w_in, out_g, w_out):
    bsz, seq, _ = h.shape
    nc = seq // CHUNK
    proj = h @ w_in
    dq = B_HEADS * B_DK
    dv = B_HEADS * B_DV
    q = rotary(proj[..., :dq].reshape(bsz, seq, B_HEADS, B_DK))
    k = rotary(proj[..., dq:2 * dq].reshape(bsz, seq, B_HEADS, B_DK)) * (B_DK ** -0.5)
    v = proj[..., 2 * dq:2 * dq + dv].reshape(bsz, seq, B_HEADS, B_DV).astype(jnp.float32)
    g = proj[..., 2 * dq + dv:]

    log_gamma = jnp.log1p(-(2.0 ** (-5.0 - jnp.arange(B_HEADS, dtype=jnp.float32))))
    pos = jnp.arange(CHUNK, dtype=jnp.float32)
    diff = pos[:, None] - pos[None, :]
    decay_intra = jnp.where(diff[None] >= 0,
                            jnp.exp(jnp.maximum(diff, 0.0)[None] * log_gamma[:, None, None]), 0.0)
    xi = jnp.exp((pos + 1.0)[:, None] * log_gamma[None, :])
    zeta = jnp.exp((CHUNK - 1.0 - pos)[:, None] * log_gamma[None, :])
    gamma_c = jnp.exp(CHUNK * log_gamma)

    qc = q.reshape(bsz, nc, CHUNK, B_HEADS, B_DK)
    kc = k.reshape(bsz, nc, CHUNK, B_HEADS, B_DK)
    vc = v.reshape(bsz, nc, CHUNK, B_HEADS, B_DV)
    s = jnp.einsum('bnqhd,bnkhd->bnhqk', qc, kc) * decay_intra[None, None]
    intra = jnp.einsum('bnhqk,bnkhe->bnqhe', s, vc)

    def step(state, inp):
        qn, kn, vn = inp
        cross = jnp.einsum('bqhd,bhde->bqhe', qn, state) * xi[None, :, :, None]
        state = state * gamma_c[None, :, None, None] + jnp.einsum(
            'bkhd,bkhe->bhde', kn * zeta[None, :, :, None], vn)
        return state, cross

    state0 = jnp.zeros((bsz, B_HEADS, B_DK, B_DV), jnp.float32)
    _, cross = lax.scan(step, state0, (qc.swapaxes(0, 1), kc.swapaxes(0, 1), vc.swapaxes(0, 1)))
    ret = (intra + cross.swapaxes(0, 1)).reshape(bsz, seq, B_HEADS, B_DV)
    ret = rms_norm(ret, out_g.reshape(B_HEADS, B_DV)).reshape(bsz, seq, dv)
    y = ret * jax.nn.silu(g.astype(jnp.float32))
    return y.astype(h.dtype) @ w_out


def conv_ffn(h, w_up, conv_w, conv_b, w_down):
    seq = h.shape[1]
    u = h @ w_up
    up = jnp.pad(u, ((0, 0), (CONV_W - 1, 0), (0, 0)))
    u = sum(up[:, j:j + seq] * conv_w[j] for j in range(CONV_W)) + conv_b
    a, b = u[..., :D_FF], u[..., D_FF:]
    return (jax.nn.silu(a) * b) @ w_down


def setup_inputs(seed: int = 0) -> dict:
    key = jax.random.key(seed)
    ks = jax.random.split(key, 16)

    def nrm(k, shape, scale):
        return jax.random.normal(k, shape, jnp.float32) * scale

    return {
        'x': nrm(ks[0], (BATCH, SEQ, D_MODEL), 1.0),
        'norm_mix_g': 1.0 + nrm(ks[1], (DEPTH, D_MODEL), 0.02),
        'norm_ffn_g': 1.0 + nrm(ks[2], (DEPTH, D_MODEL), 0.02),
        'a_w_in': nrm(ks[3], (N_A, D_MODEL, A_IN_COLS), D_MODEL ** -0.5),
        'a_q_g': 1.0 + nrm(ks[4], (N_A, A_LAT), 0.02),
        'a_k_g': 1.0 + nrm(ks[5], (N_A, A_LAT), 0.02),
        'a_iq_g': 1.0 + nrm(ks[6], (N_A, IDX_DIM), 0.02),
        'a_ik_g': 1.0 + nrm(ks[7], (N_A, IDX_DIM), 0.02),
        'a_w_out': nrm(ks[8], (N_A, A_Q_COLS, D_MODEL), A_Q_COLS ** -0.5),
        'b_w_in': nrm(ks[9], (N_B, D_MODEL, B_IN_COLS), D_MODEL ** -0.5),
        'b_out_g': 1.0 + nrm(ks[10], (N_B, B_HEADS * B_DV), 0.02),
        'b_w_out': nrm(ks[11], (N_B, B_HEADS * B_DV, D_MODEL), (B_HEADS * B_DV) ** -0.5),
        'f_w_up': nrm(ks[12], (DEPTH, D_MODEL, 2 * D_FF), D_MODEL ** -0.5),
        'f_conv_w': nrm(ks[13], (DEPTH, CONV_W, 2 * D_FF), CONV_W ** -0.5),
        'f_conv_b': nrm(ks[14], (DEPTH, 2 * D_FF), 0.01),
        'f_w_down': nrm(ks[15], (DEPTH, D_FF, D_MODEL), D_FF ** -0.5),
    }


def reference(x, norm_mix_g, norm_ffn_g, a_w_in, a_q_g, a_k_g, a_iq_g, a_ik_g, a_w_out,
              b_w_in, b_out_g, b_w_out, f_w_up, f_conv_w, f_conv_b, f_w_down):
    for i in range(DEPTH):
        h = rms_norm(x, norm_mix_g[i])
        j = i // N_MIXERS
        if i % N_MIXERS == 0:
            x = x + dsa_mixer(h, a_w_in[j], a_q_g[j], a_k_g[j], a_iq_g[j], a_ik_g[j], a_w_out[j])
        else:
            x = x + retention_mixer(h, b_w_in[j], b_out_g[j], b_w_out[j])
        x = x + conv_ffn(rms_norm(x, norm_ffn_g[i]), f_w_up[i], f_conv_w[i], f_conv_b[i], f_w_down[i])
    return x
```

```python
import functools

import jax
import jax.numpy as jnp
import numpy as np
from jax import lax
from jax.experimental import pallas as pl
from jax.experimental.pallas import tpu as pltpu

D_MODEL = 1024
BATCH = 8
SEQ = 2048
TOKENS = BATCH * SEQ
CHUNK = 64
EPS = 1e-6
A_HEADS = 8
A_LAT = 128
IDX_HEADS = 8
IDX_DIM = 64
TOPK = 256
A_Q_COLS = A_HEADS * A_LAT
B_HEADS = 4
B_DK = 256
B_DV = 512
ROPE_BASE = 10000.0
RET_CHUNK = 256
D_FF = 2816
CONV_W = 3
FF_CHUNK = 256
N_FF_CHUNKS = D_FF // FF_CHUNK

LANES = 128
SUBLANES = 8
ROW_TILE = 512
Q_BLOCK = 128
KEY_CHUNK = 256
VMEM_LIMIT = 56 * 1024 * 1024
MASK_NEG = -1e30

BF16 = jnp.bfloat16
F32 = jnp.float32


def _params(*sem):
    return pltpu.CompilerParams(dimension_semantics=sem, vmem_limit_bytes=VMEM_LIMIT)


def _const_spec(shape):
    nd = len(shape)
    return pl.BlockSpec(shape, lambda *_: (0,) * nd, pipeline_mode=pl.Buffered(1))


def _rms(x, g):
    return x * lax.rsqrt(jnp.mean(x * x, axis=-1, keepdims=True) + EPS) * g


def _dot(a, b):
    return jnp.dot(a, b, preferred_element_type=F32)


def _dot_nt(a, b):
    return lax.dot_general(a, b, (((1,), (1,)), ((), ())), preferred_element_type=F32)


def _dot_tn(a, b):
    return lax.dot_general(a, b, (((0,), (0,)), ((), ())), preferred_element_type=F32)


def _proj_a_kernel(x_ref, g_ref, wq_ref, wck_ref, wt_ref, qg_ref, kg_ref, iqg_ref, ikg_ref,
                   q_out, k_out, c_out, qit_out, ki_out, wit_out):
    hb = _rms(x_ref[...], g_ref[...]).astype(BF16)
    q = _dot(hb, wq_ref[...])
    qg = qg_ref[...] * (A_LAT ** -0.5)
    for h in range(A_HEADS):
        q_out[h] = _rms(q[:, h * A_LAT:(h + 1) * A_LAT], qg).astype(BF16)
    ck = _dot(hb, wck_ref[...])
    c = ck[:, :A_LAT]
    c_out[...] = c.astype(BF16)
    k_out[...] = _rms(c, kg_ref[...]).astype(BF16)
    ki = ck[:, A_LAT:A_LAT + IDX_DIM]
    ki_out[...] = _rms(ki, ikg_ref[...]).astype(BF16)
    t = _dot_nt(wt_ref[...], hb)
    for h in range(IDX_HEADS):
        qi = t[h * IDX_DIM:(h + 1) * IDX_DIM, :]
        r = lax.rsqrt(jnp.mean(qi * qi, axis=0, keepdims=True) + EPS)
        qit_out[h * IDX_DIM:(h + 1) * IDX_DIM, :] = (qi * r * iqg_ref[...]).astype(BF16)
    wit_out[...] = t[IDX_HEADS * IDX_DIM:, :] * (IDX_HEADS ** -0.5 * IDX_DIM ** -0.5)


def _proj_a(x2, g, wq, wck, wt, qg, kg, iqg, ikg):
    n = TOKENS // ROW_TILE
    nt = IDX_HEADS * IDX_DIM + SUBLANES
    return pl.pallas_call(
        _proj_a_kernel,
        grid=(n,),
        in_specs=[
            pl.BlockSpec((ROW_TILE, D_MODEL), lambda i: (i, 0)),
            _const_spec((1, D_MODEL)),
            _const_spec((D_MODEL, A_Q_COLS)),
            _const_spec((D_MODEL, 2 * A_LAT)),
            _const_spec((nt, D_MODEL)),
            _const_spec((1, A_LAT)),
            _const_spec((1, A_LAT)),
            _const_spec((IDX_DIM, 1)),
            _const_spec((1, IDX_DIM)),
        ],
        out_specs=[
            pl.BlockSpec((A_HEADS, ROW_TILE, A_LAT), lambda i: (0, i, 0)),
            pl.BlockSpec((ROW_TILE, A_LAT), lambda i: (i, 0)),
            pl.BlockSpec((ROW_TILE, A_LAT), lambda i: (i, 0)),
            pl.BlockSpec((IDX_HEADS * IDX_DIM, ROW_TILE), lambda i: (0, i)),
            pl.BlockSpec((ROW_TILE, IDX_DIM), lambda i: (i, 0)),
            pl.BlockSpec((SUBLANES, ROW_TILE), lambda i: (0, i)),
        ],
        out_shape=[
            jax.ShapeDtypeStruct((A_HEADS, TOKENS, A_LAT), BF16),
            jax.ShapeDtypeStruct((TOKENS, A_LAT), BF16),
            jax.ShapeDtypeStruct((TOKENS, A_LAT), BF16),
            jax.ShapeDtypeStruct((IDX_HEADS * IDX_DIM, TOKENS), BF16),
            jax.ShapeDtypeStruct((TOKENS, IDX_DIM), BF16),
            jax.ShapeDtypeStruct((SUBLANES, TOKENS), F32),
        ],
        compiler_params=_params("parallel"),
        name="dsa_in_proj",
    )(x2, g, wq, wck, wt, qg, kg, iqg, ikg)


def _ordered_bits_to_float(u):
    k = u ^ jnp.int32(-2 ** 31)
    bits = k ^ ((k >> 31) & jnp.int32(0x7FFFFFFF))
    return pltpu.bitcast(bits, F32)


def _index_kernel(qit_ref, wit_ref, ki_ref, bias_ref, score_scr, thr_scr, cut_scr):
    i = pl.program_id(1)
    n_chunks = (i + 2) // 2
    qpos = i * Q_BLOCK + lax.broadcasted_iota(jnp.int32, (1, Q_BLOCK), 1)
    limit = ((qpos >> 6) + 1) << 6

    def key_pos(r0, rows):
        return r0 + lax.broadcasted_iota(jnp.int32, (rows, Q_BLOCK), 0)

    def chunk_start(j):
        return pl.multiple_of(j * KEY_CHUNK, KEY_CHUNK)

    def count(pred_fn):
        def body(j, c):
            r0 = chunk_start(j)
            s = score_scr[pl.ds(r0, KEY_CHUNK), :]
            hit = pred_fn(s, key_pos(r0, KEY_CHUNK))
            return c + jnp.sum(hit, axis=0, keepdims=True)
        return lax.fori_loop(0, n_chunks, body, jnp.zeros((1, Q_BLOCK), F32))

    @pl.when(i < 2)
    def _():
        score_scr[pl.ds(0, KEY_CHUNK), :] = jnp.zeros((KEY_CHUNK, Q_BLOCK), F32)
        thr_scr[...] = jnp.full(thr_scr.shape, -jnp.inf, F32)
        cut_scr[...] = jnp.full(cut_scr.shape, 2 * SEQ, jnp.int32)

    @pl.when(i >= 2)
    def _():
        def score_body(j, carry):
            r0 = chunk_start(j)
            kic = ki_ref[pl.ds(r0, KEY_CHUNK), :]
            acc = jnp.zeros((KEY_CHUNK, Q_BLOCK), F32)
            for h in range(IDX_HEADS):
                rel = _dot(kic, qit_ref[h * IDX_DIM:(h + 1) * IDX_DIM, :])
                acc = acc + jnp.maximum(rel, 0.0) * wit_ref[h:h + 1, :]
            score_scr[pl.ds(r0, KEY_CHUNK), :] = jnp.where(key_pos(r0, KEY_CHUNK) < limit, acc, -jnp.inf)
            return carry
        lax.fori_loop(0, n_chunks, score_body, 0)

        def bit_body(b, u):
            cand = u | (jnp.int32(1) << (31 - b))
            cf = _ordered_bits_to_float(cand)
            n_ge = count(lambda s, _: jnp.where(s >= cf, 1.0, 0.0))
            return jnp.where(n_ge >= float(TOPK), cand, u)
        u = lax.fori_loop(0, 32, bit_body, jnp.zeros((1, Q_BLOCK), jnp.int32))
        thr = _ordered_bits_to_float(u)
        thr_scr[...] = jnp.broadcast_to(thr, thr_scr.shape)
        cut_scr[...] = jnp.full(cut_scr.shape, 2 * SEQ, jnp.int32)

        n_ge = count(lambda s, _: jnp.where(s >= thr, 1.0, 0.0))
        any_tie = jnp.max(jnp.where(n_ge > float(TOPK), 1.0, 0.0))

        @pl.when(any_tie > 0.0)
        def _():
            n_gt = count(lambda s, _: jnp.where(s > thr, 1.0, 0.0))
            need = float(TOPK) - n_gt
            cut = jnp.zeros((1, Q_BLOCK), jnp.int32)
            for b in range(11, -1, -1):
                cand = cut | (1 << b)
                n_eq = count(lambda s, kp: jnp.where(s == thr, jnp.where(kp < cand, 1.0, 0.0), 0.0))
                cut = jnp.where(n_eq <= need, cand, cut)
            cut_scr[...] = jnp.broadcast_to(cut, cut_scr.shape)

    bias_ref[...] = jnp.full(bias_ref.shape, MASK_NEG, BF16)
    thr = thr_scr[0:1, :]
    cut = cut_scr[0:1, :]

    def out_body(j, carry):
        r0 = pl.multiple_of(j * Q_BLOCK, Q_BLOCK)
        s = score_scr[pl.ds(r0, Q_BLOCK), :]
        kp = key_pos(r0, Q_BLOCK)
        keep = jnp.where(s > thr, 1.0, jnp.where(s == thr, jnp.where(kp < cut, 1.0, 0.0), 0.0))
        keep = jnp.where(kp < limit, keep, 0.0)
        bias_t = jnp.where(keep > 0.0, 0.0, MASK_NEG)
        bias_ref[:, pl.ds(r0, Q_BLOCK)] = bias_t.T.astype(BF16)
        return carry
    lax.fori_loop(0, i + 1, out_body, 0)


def _index(qit, wit, ki):
    nq = SEQ // Q_BLOCK
    return pl.pallas_call(
        _index_kernel,
        grid=(BATCH, nq),
        in_specs=[
            pl.BlockSpec((IDX_HEADS * IDX_DIM, Q_BLOCK), lambda b, i: (0, b * nq + i)),
            pl.BlockSpec((SUBLANES, Q_BLOCK), lambda b, i: (0, b * nq + i)),
            pl.BlockSpec((SEQ, IDX_DIM), lambda b, i: (b, 0)),
        ],
        out_specs=pl.BlockSpec((Q_BLOCK, SEQ), lambda b, i: (b * nq + i, 0)),
        out_shape=jax.ShapeDtypeStruct((TOKENS, SEQ), BF16),
        scratch_shapes=[
            pltpu.VMEM((SEQ, Q_BLOCK), F32),
            pltpu.VMEM((SUBLANES, Q_BLOCK), F32),
            pltpu.VMEM((SUBLANES, Q_BLOCK), jnp.int32),
        ],
        compiler_params=_params("parallel", "arbitrary"),
        name="dsa_index",
    )(qit, wit, ki)


def _attn_kernel(q_ref, k_ref, c_ref, bias_ref, o_ref, m_scr, l_scr, acc_scr):
    i = pl.program_id(1)
    n_chunks = (i + 2) // 2
    rows = A_HEADS * Q_BLOCK
    q = q_ref[...].reshape(rows, A_LAT)
    m_scr[...] = jnp.full(m_scr.shape, -jnp.inf, F32)
    l_scr[...] = jnp.zeros(l_scr.shape, F32)
    acc_scr[...] = jnp.zeros(acc_scr.shape, F32)

    def body(j, carry):
        r0 = pl.multiple_of(j * KEY_CHUNK, KEY_CHUNK)
        kc = k_ref[pl.ds(r0, KEY_CHUNK), :]
        cc = c_ref[pl.ds(r0, KEY_CHUNK), :]
        bias = bias_ref[:, pl.ds(r0, KEY_CHUNK)].astype(F32)
        s = _dot_nt(q, kc).reshape(A_HEADS, Q_BLOCK, KEY_CHUNK) + bias[None]
        s = s.reshape(rows, KEY_CHUNK)
        m_old = m_scr[...]
        m_new = jnp.maximum(m_old, jnp.max(s, axis=-1, keepdims=True))
        alpha = jnp.exp(m_old - m_new)
        p = jnp.exp(s - m_new)
        l_scr[...] = alpha * l_scr[...] + jnp.sum(p, axis=-1, keepdims=True)
        acc_scr[...] = alpha * acc_scr[...] + _dot(p.astype(BF16), cc)
        m_scr[...] = m_new
        return carry
    lax.fori_loop(0, n_chunks, body, 0)

    o = acc_scr[...] / l_scr[...]
    for h in range(A_HEADS):
        o_ref[:, h * A_LAT:(h + 1) * A_LAT] = o[h * Q_BLOCK:(h + 1) * Q_BLOCK, :].astype(BF16)


def _attn(q4, k, c, bias):
    nq = SEQ // Q_BLOCK
    rows = A_HEADS * Q_BLOCK
    return pl.pallas_call(
        _attn_kernel,
        grid=(BATCH, nq),
        in_specs=[
            pl.BlockSpec((A_HEADS, Q_BLOCK, A_LAT), lambda b, i: (0, b * nq + i, 0)),
            pl.BlockSpec((SEQ, A_LAT), lambda b, i: (b, 0)),
            pl.BlockSpec((SEQ, A_LAT), lambda b, i: (b, 0)),
            pl.BlockSpec((Q_BLOCK, SEQ), lambda b, i: (b * nq + i, 0)),
        ],
        out_specs=pl.BlockSpec((Q_BLOCK, A_Q_COLS), lambda b, i: (b * nq + i, 0)),
        out_shape=jax.ShapeDtypeStruct((TOKENS, A_Q_COLS), BF16),
        scratch_shapes=[
            pltpu.VMEM((rows, 1), F32),
            pltpu.VMEM((rows, 1), F32),
            pltpu.VMEM((rows, A_LAT), F32),
        ],
        compiler_params=_params("parallel", "arbitrary"),
        name="dsa_attn",
    )(q4, k, c, bias)


def _out_proj_kernel(x_ref, a_ref, w_ref, o_ref):
    o_ref[...] = x_ref[...] + _dot(a_ref[...], w_ref[...])


def _out_proj(x2, a, w):
    n = TOKENS // ROW_TILE
    k = a.shape[1]
    return pl.pallas_call(
        _out_proj_kernel,
        grid=(n,),
        in_specs=[
            pl.BlockSpec((ROW_TILE, D_MODEL), lambda i: (i, 0)),
            pl.BlockSpec((ROW_TILE, k), lambda i: (i, 0)),
            _const_spec((k, D_MODEL)),
        ],
        out_specs=pl.BlockSpec((ROW_TILE, D_MODEL), lambda i: (i, 0)),
        out_shape=jax.ShapeDtypeStruct((TOKENS, D_MODEL), F32),
        compiler_params=_params("parallel"),
        name="out_proj",
    )(x2, a, w)


def _ffn_kernel(x_ref, g_ref, wup_ref, cw_ref, cb_ref, wdown_ref, o_ref, ext_scr, carry_scr, acc_scr):
    t = pl.program_id(0)
    x = x_ref[...]
    hb = _rms(x, g_ref[...]).astype(BF16)
    acc_scr[...] = jnp.zeros(acc_scr.shape, F32)
    first_of_sequence = (t % (SEQ // ROW_TILE)) == 0

    def body(c, carry):
        u = _dot(hb, wup_ref[c])
        prev = jnp.where(first_of_sequence, 0.0, carry_scr[c])
        ext_scr[0:SUBLANES, :] = prev
        ext_scr[SUBLANES:, :] = u
        carry_scr[c] = u[ROW_TILE - SUBLANES:, :]
        cw = cw_ref[c]
        v = (u * cw[2:3, :]
             + ext_scr[SUBLANES - 1:SUBLANES - 1 + ROW_TILE, :] * cw[1:2, :]
             + ext_scr[SUBLANES - 2:SUBLANES - 2 + ROW_TILE, :] * cw[0:1, :]
             + cb_ref[c])
        a = v[:, :FF_CHUNK]
        b = v[:, FF_CHUNK:]
        gate = (a * jax.nn.sigmoid(a) * b).astype(BF16)
        acc_scr[...] += _dot(gate, wdown_ref[c])
        return carry
    lax.fori_loop(0, N_FF_CHUNKS, body, 0)
    o_ref[...] = x + acc_scr[...]


def _ffn(x2, g, wup3, cw3, cb3, wdown3):
    n = TOKENS // ROW_TILE
    return pl.pallas_call(
        _ffn_kernel,
        grid=(n,),
        in_specs=[
            pl.BlockSpec((ROW_TILE, D_MODEL), lambda i: (i, 0)),
            _const_spec((1, D_MODEL)),
            _const_spec((N_FF_CHUNKS, D_MODEL, 2 * FF_CHUNK)),
            _const_spec((N_FF_CHUNKS, CONV_W, 2 * FF_CHUNK)),
            _const_spec((N_FF_CHUNKS, 1, 2 * FF_CHUNK)),
            _const_spec((N_FF_CHUNKS, FF_CHUNK, D_MODEL)),
        ],
        out_specs=pl.BlockSpec((ROW_TILE, D_MODEL), lambda i: (i, 0)),
        out_shape=jax.ShapeDtypeStruct((TOKENS, D_MODEL), F32),
        scratch_shapes=[
            pltpu.VMEM((ROW_TILE + SUBLANES, 2 * FF_CHUNK), F32),
            pltpu.VMEM((N_FF_CHUNKS, SUBLANES, 2 * FF_CHUNK), F32),
            pltpu.VMEM((ROW_TILE, D_MODEL), F32),
        ],
        compiler_params=_params("arbitrary"),
        name="conv_ffn",
    )(x2, g, wup3, cw3, cb3, wdown3)


def _proj_b_kernel(x_ref, g_ref, w_ref, cos_ref, sin_ref, q_out, k_out, v_out, g_out):
    hb = _rms(x_ref[...], g_ref[...]).astype(BF16)
    cos = cos_ref[...]
    sin = sin_ref[...]
    half = B_DK // 2

    def rotary(z):
        z1, z2 = z[:, :half], z[:, half:]
        return z1 * cos - z2 * sin, z1 * sin + z2 * cos

    dq = B_HEADS * B_DK
    for h in range(B_HEADS):
        zq = _dot(hb, w_ref[:, h * B_DK:(h + 1) * B_DK])
        r1, r2 = rotary(zq)
        q_out[:, h * B_DK:h * B_DK + half] = r1.astype(BF16)
        q_out[:, h * B_DK + half:(h + 1) * B_DK] = r2.astype(BF16)
        zk = _dot(hb, w_ref[:, dq + h * B_DK:dq + (h + 1) * B_DK])
        r1, r2 = rotary(zk)
        k_out[:, h * B_DK:h * B_DK + half] = (r1 * (B_DK ** -0.5)).astype(BF16)
        k_out[:, h * B_DK + half:(h + 1) * B_DK] = (r2 * (B_DK ** -0.5)).astype(BF16)
    dv = B_HEADS * B_DV
    for h in range(B_HEADS):
        v_out[:, h * B_DV:(h + 1) * B_DV] = _dot(
            hb, w_ref[:, 2 * dq + h * B_DV:2 * dq + (h + 1) * B_DV]).astype(BF16)
        g_out[:, h * B_DV:(h + 1) * B_DV] = _dot(
            hb, w_ref[:, 2 * dq + dv + h * B_DV:2 * dq + dv + (h + 1) * B_DV]).astype(BF16)


def _proj_b(x2, g, w, cos, sin):
    n = TOKENS // ROW_TILE
    per_seq = SEQ // ROW_TILE
    dq = B_HEADS * B_DK
    dv = B_HEADS * B_DV
    return pl.pallas_call(
        _proj_b_kernel,
        grid=(n,),
        in_specs=[
            pl.BlockSpec((ROW_TILE, D_MODEL), lambda i: (i, 0)),
            _const_spec((1, D_MODEL)),
            _const_spec((D_MODEL, 2 * dq + 2 * dv)),
            pl.BlockSpec((ROW_TILE, B_DK // 2), lambda i: (i % per_seq, 0)),
            pl.BlockSpec((ROW_TILE, B_DK // 2), lambda i: (i % per_seq, 0)),
        ],
        out_specs=[
            pl.BlockSpec((ROW_TILE, dq), lambda i: (i, 0)),
            pl.BlockSpec((ROW_TILE, dq), lambda i: (i, 0)),
            pl.BlockSpec((ROW_TILE, dv), lambda i: (i, 0)),
            pl.BlockSpec((ROW_TILE, dv), lambda i: (i, 0)),
        ],
        out_shape=[
            jax.ShapeDtypeStruct((TOKENS, dq), BF16),
            jax.ShapeDtypeStruct((TOKENS, dq), BF16),
            jax.ShapeDtypeStruct((TOKENS, dv), BF16),
            jax.ShapeDtypeStruct((TOKENS, dv), BF16),
        ],
        compiler_params=_params("parallel"),
        name="ret_in_proj",
    )(x2, g, w, cos, sin)


def _retention_kernel(q_ref, k_ref, v_ref, g_ref, decay_ref, xi_ref, zeta_ref, gc_ref, og_ref,
                      y_ref, state_scr):
    state_scr[...] = jnp.zeros(state_scr.shape, F32)
    decay = decay_ref[0]
    xi = xi_ref[0]
    zeta = zeta_ref[0]
    gc = gc_ref[0]
    og = og_ref[...]

    def body(n, carry):
        r0 = pl.multiple_of(n * RET_CHUNK, RET_CHUNK)
        qc = q_ref[pl.ds(r0, RET_CHUNK), :]
        kc = k_ref[pl.ds(r0, RET_CHUNK), :]
        vc = v_ref[pl.ds(r0, RET_CHUNK), :]
        state = state_scr[...]
        s = (_dot_nt(qc, kc) * decay).astype(BF16)
        ret = _dot(s, vc) + _dot(qc, state.astype(BF16)) * xi
        kz = (kc.astype(F32) * zeta).astype(BF16)
        state_scr[...] = state * gc + _dot_tn(kz, vc)
        gate = g_ref[pl.ds(r0, RET_CHUNK), :].astype(F32)
        y = _rms(ret, og) * (gate * jax.nn.sigmoid(gate))
        y_ref[pl.ds(r0, RET_CHUNK), :] = y.astype(BF16)
        return carry
    lax.fori_loop(0, SEQ // RET_CHUNK, body, 0)


def _retention(q, k, v, g, decay, xi, zeta, gc, og):
    return pl.pallas_call(
        _retention_kernel,
        grid=(BATCH, B_HEADS),
        in_specs=[
            pl.BlockSpec((SEQ, B_DK), lambda b, h: (b, h)),
            pl.BlockSpec((SEQ, B_DK), lambda b, h: (b, h)),
            pl.BlockSpec((SEQ, B_DV), lambda b, h: (b, h)),
            pl.BlockSpec((SEQ, B_DV), lambda b, h: (b, h)),
            pl.BlockSpec((1, RET_CHUNK, RET_CHUNK), lambda b, h: (h, 0, 0)),
            pl.BlockSpec((1, RET_CHUNK, B_DV), lambda b, h: (h, 0, 0)),
            pl.BlockSpec((1, RET_CHUNK, B_DK), lambda b, h: (h, 0, 0)),
            pl.BlockSpec((1, 1, B_DV), lambda b, h: (h, 0, 0)),
            pl.BlockSpec((1, B_DV), lambda b, h: (0, h)),
        ],
        out_specs=pl.BlockSpec((SEQ, B_DV), lambda b, h: (b, h)),
        out_shape=jax.ShapeDtypeStruct((TOKENS, B_HEADS * B_DV), BF16),
        scratch_shapes=[pltpu.VMEM((B_DK, B_DV), F32)],
        compiler_params=_params("parallel", "parallel"),
        name="retention",
    )(q, k, v, g, decay, xi, zeta, gc, og)


def _retention_tables():
    log_gamma = jnp.log1p(-(2.0 ** (-5.0 - jnp.arange(B_HEADS, dtype=F32))))
    pos = jnp.arange(RET_CHUNK, dtype=F32)
    diff = pos[:, None] - pos[None, :]
    decay = jnp.where(diff[None] >= 0,
                      jnp.exp(jnp.maximum(diff, 0.0)[None] * log_gamma[:, None, None]), 0.0)
    xi = jnp.exp((pos + 1.0)[None, :] * log_gamma[:, None])
    zeta = jnp.exp((RET_CHUNK - 1.0 - pos)[None, :] * log_gamma[:, None])
    gc = jnp.exp(RET_CHUNK * log_gamma)
    xi_b = jnp.broadcast_to(xi[:, :, None], (B_HEADS, RET_CHUNK, B_DV))
    zeta_b = jnp.broadcast_to(zeta[:, :, None], (B_HEADS, RET_CHUNK, B_DK))
    gc_b = jnp.broadcast_to(gc[:, None, None], (B_HEADS, 1, B_DV))
    return decay, xi_b, zeta_b, gc_b


def _rotary_tables():
    inv = 1.0 / (ROPE_BASE ** (jnp.arange(0, B_DK, 2, dtype=F32) / B_DK))
    ang = jnp.arange(SEQ, dtype=F32)[:, None] * inv[None, :]
    return jnp.cos(ang), jnp.sin(ang)


def _ffn_weights(w_up, conv_w, conv_b, w_down):
    def pair(m):
        a = m[..., :D_FF].reshape(*m.shape[:-1], N_FF_CHUNKS, FF_CHUNK)
        b = m[..., D_FF:].reshape(*m.shape[:-1], N_FF_CHUNKS, FF_CHUNK)
        ab = jnp.concatenate([a, b], axis=-1)
        return jnp.moveaxis(ab, -2, 0)
    wup3 = pair(w_up).astype(BF16)
    cw3 = pair(conv_w)
    cb3 = pair(conv_b[None, :])
    wdown3 = w_down.reshape(N_FF_CHUNKS, FF_CHUNK, D_MODEL).astype(BF16)
    return wup3, cw3, cb3, wdown3


def kernel(x, norm_mix_g, norm_ffn_g, a_w_in, a_q_g, a_k_g, a_iq_g, a_ik_g, a_w_out,
           b_w_in, b_out_g, b_w_out, f_w_up, f_conv_w, f_conv_b, f_w_down):
    x2 = x.reshape(TOKENS, D_MODEL)

    w = a_w_in[0]
    o1 = A_Q_COLS
    o2 = o1 + A_LAT
    o3 = o2 + IDX_HEADS * IDX_DIM
    o4 = o3 + IDX_DIM
    wq = w[:, :o1].astype(BF16)
    wck = jnp.concatenate(
        [w[:, o1:o2], w[:, o3:o4], jnp.zeros((D_MODEL, A_LAT - IDX_DIM), F32)], axis=1).astype(BF16)
    wt = jnp.concatenate([w[:, o2:o3], w[:, o4:]], axis=1).T.astype(BF16)
    q4, k, c, qit, ki, wit = _proj_a(
        x2, norm_mix_g[0][None, :], wq, wck, wt,
        a_q_g[0][None, :], a_k_g[0][None, :], a_iq_g[0][:, None], a_ik_g[0][None, :])
    bias = _index(qit, wit, ki)
    o = _attn(q4, k, c, bias)
    x2 = _out_proj(x2, o, a_w_out[0].astype(BF16))
    x2 = _ffn(x2, norm_ffn_g[0][None, :], *_ffn_weights(f_w_up[0], f_conv_w[0], f_conv_b[0], f_w_down[0]))

    cos, sin = _rotary_tables()
    q, kk, v, g = _proj_b(x2, norm_mix_g[1][None, :], b_w_in[0].astype(BF16), cos, sin)
    y = _retention(q, kk, v, g, *_retention_tables(), b_out_g[0][None, :])
    x2 = _out_proj(x2, y, b_w_out[0].astype(BF16))
    x2 = _ffn(x2, norm_ffn_g[1][None, :], *_ffn_weights(f_w_up[1], f_conv_w[1], f_conv_b[1], f_w_down[1]))
    return x2.reshape(BATCH, SEQ, D_MODEL)
```

```python
import functools

import jax
import jax.numpy as jnp
import numpy as np
from jax import lax
from jax.experimental import pallas as pl
from jax.experimental.pallas import tpu as pltpu

D_MODEL = 1024
BATCH = 8
SEQ = 2048
TOKENS = BATCH * SEQ
CHUNK = 64
EPS = 1e-6
A_HEADS = 8
A_LAT = 128
IDX_HEADS = 8
IDX_DIM = 64
TOPK = 256
A_Q_COLS = A_HEADS * A_LAT
B_HEADS = 4
B_DK = 256
B_DV = 512
ROPE_BASE = 10000.0
RET_CHUNK = 256
D_FF = 2816
CONV_W = 3
FF_CHUNK = 256
N_FF_CHUNKS = D_FF // FF_CHUNK

LANES = 128
SUBLANES = 8
ROW_TILE = 512
Q_BLOCK = 128
KEY_CHUNK = 256
COUNT_ROWS = 32
INDEX_BITS = 12
VMEM_LIMIT = 56 * 1024 * 1024
MASK_NEG = -1e30

BF16 = jnp.bfloat16
F32 = jnp.float32


def _params(*sem):
    return pltpu.CompilerParams(dimension_semantics=sem, vmem_limit_bytes=VMEM_LIMIT)


def _const_spec(shape):
    nd = len(shape)
    return pl.BlockSpec(shape, lambda *_: (0,) * nd, pipeline_mode=pl.Buffered(1))


def _rms(x, g):
    return x * lax.rsqrt(jnp.mean(x * x, axis=-1, keepdims=True) + EPS) * g


def _dot(a, b):
    return jnp.dot(a, b, preferred_element_type=F32)


def _dot_nt(a, b):
    return lax.dot_general(a, b, (((1,), (1,)), ((), ())), preferred_element_type=F32)


def _dot_tn(a, b):
    return lax.dot_general(a, b, (((0,), (0,)), ((), ())), preferred_element_type=F32)


def _proj_a_kernel(x_ref, g_ref, wq_ref, wck_ref, wt_ref, qg_ref, kg_ref, iqg_ref, ikg_ref,
                   q_out, k_out, c_out, qit_out, ki_out, wit_out):
    hb = _rms(x_ref[...], g_ref[...]).astype(BF16)
    q = _dot(hb, wq_ref[...])
    qg = qg_ref[...] * (A_LAT ** -0.5)
    for h in range(A_HEADS):
        q_out[h] = _rms(q[:, h * A_LAT:(h + 1) * A_LAT], qg).astype(BF16)
    ck = _dot(hb, wck_ref[...])
    c = ck[:, :A_LAT]
    c_out[...] = c.astype(BF16)
    k_out[...] = _rms(c, kg_ref[...]).astype(BF16)
    ki = ck[:, A_LAT:A_LAT + IDX_DIM]
    ki_out[...] = _rms(ki, ikg_ref[...]).astype(BF16)
    t = _dot_nt(wt_ref[...], hb)
    for h in range(IDX_HEADS):
        qi = t[h * IDX_DIM:(h + 1) * IDX_DIM, :]
        r = lax.rsqrt(jnp.mean(qi * qi, axis=0, keepdims=True) + EPS)
        qit_out[h * IDX_DIM:(h + 1) * IDX_DIM, :] = (qi * r * iqg_ref[...]).astype(BF16)
    wit_out[...] = t[IDX_HEADS * IDX_DIM:, :] * (IDX_HEADS ** -0.5 * IDX_DIM ** -0.5)


def _proj_a(x2, g, wq, wck, wt, qg, kg, iqg, ikg):
    n = TOKENS // ROW_TILE
    nt = IDX_HEADS * IDX_DIM + SUBLANES
    return pl.pallas_call(
        _proj_a_kernel,
        grid=(n,),
        in_specs=[
            pl.BlockSpec((ROW_TILE, D_MODEL), lambda i: (i, 0)),
            _const_spec((1, D_MODEL)),
            _const_spec((D_MODEL, A_Q_COLS)),
            _const_spec((D_MODEL, 2 * A_LAT)),
            _const_spec((nt, D_MODEL)),
            _const_spec((1, A_LAT)),
            _const_spec((1, A_LAT)),
            _const_spec((IDX_DIM, 1)),
            _const_spec((1, IDX_DIM)),
        ],
        out_specs=[
            pl.BlockSpec((A_HEADS, ROW_TILE, A_LAT), lambda i: (0, i, 0)),
            pl.BlockSpec((ROW_TILE, A_LAT), lambda i: (i, 0)),
            pl.BlockSpec((ROW_TILE, A_LAT), lambda i: (i, 0)),
            pl.BlockSpec((IDX_HEADS * IDX_DIM, ROW_TILE), lambda i: (0, i)),
            pl.BlockSpec((ROW_TILE, IDX_DIM), lambda i: (i, 0)),
            pl.BlockSpec((SUBLANES, ROW_TILE), lambda i: (0, i)),
        ],
        out_shape=[
            jax.ShapeDtypeStruct((A_HEADS, TOKENS, A_LAT), BF16),
            jax.ShapeDtypeStruct((TOKENS, A_LAT), BF16),
            jax.ShapeDtypeStruct((TOKENS, A_LAT), BF16),
            jax.ShapeDtypeStruct((IDX_HEADS * IDX_DIM, TOKENS), BF16),
            jax.ShapeDtypeStruct((TOKENS, IDX_DIM), BF16),
            jax.ShapeDtypeStruct((SUBLANES, TOKENS), F32),
        ],
        compiler_params=_params("parallel"),
        name="dsa_in_proj",
    )(x2, g, wq, wck, wt, qg, kg, iqg, ikg)


def _ordered_bits_to_float(u):
    k = u ^ jnp.int32(-2 ** 31)
    bits = k ^ ((k >> 31) & jnp.int32(0x7FFFFFFF))
    return pltpu.bitcast(bits, F32)


def _index_kernel(qit_ref, wit_ref, ki_ref, bias_ref, score_scr, thr_scr, cut_scr):
    i = pl.program_id(1)
    n_chunks = (i + 2) // 2
    qpos = i * Q_BLOCK + lax.broadcasted_iota(jnp.int32, (1, Q_BLOCK), 1)
    limit = ((qpos >> 6) + 1) << 6

    def key_pos(r0, rows):
        return r0 + lax.broadcasted_iota(jnp.int32, (rows, Q_BLOCK), 0)

    def chunk_start(j):
        return pl.multiple_of(j * KEY_CHUNK, KEY_CHUNK)

    def count(n, pred_fn):
        acc = None
        for j in range(n):
            r0 = j * KEY_CHUNK
            hit = pred_fn(score_scr[r0:r0 + KEY_CHUNK, :], key_pos(r0, KEY_CHUNK))
            part = jnp.sum(hit.reshape(KEY_CHUNK // COUNT_ROWS, COUNT_ROWS, Q_BLOCK), axis=0)
            acc = part if acc is None else acc + part
        return jnp.sum(acc, axis=0, keepdims=True)

    @pl.when(i < 2)
    def _():
        score_scr[pl.ds(0, KEY_CHUNK), :] = jnp.zeros((KEY_CHUNK, Q_BLOCK), F32)
        thr_scr[...] = jnp.full(thr_scr.shape, -jnp.inf, F32)
        cut_scr[...] = jnp.full(cut_scr.shape, 2 * SEQ, jnp.int32)

    @pl.when(i >= 2)
    def _():
        def score_body(j, carry):
            r0 = chunk_start(j)
            kic = ki_ref[pl.ds(r0, KEY_CHUNK), :]
            acc = jnp.zeros((KEY_CHUNK, Q_BLOCK), F32)
            for h in range(0, IDX_HEADS, 2):
                pair = jnp.concatenate([qit_ref[h * IDX_DIM:(h + 1) * IDX_DIM, :],
                                        qit_ref[(h + 1) * IDX_DIM:(h + 2) * IDX_DIM, :]], axis=1)
                rel = jnp.maximum(_dot(kic, pair), 0.0)
                acc = (acc + rel[:, :Q_BLOCK] * wit_ref[h:h + 1, :]
                       + rel[:, Q_BLOCK:] * wit_ref[h + 1:h + 2, :])
            score_scr[pl.ds(r0, KEY_CHUNK), :] = jnp.where(key_pos(r0, KEY_CHUNK) < limit, acc, -jnp.inf)
            return carry
        lax.fori_loop(0, n_chunks, score_body, 0)
        cut_scr[...] = jnp.full(cut_scr.shape, 2 * SEQ, jnp.int32)

    def search(n):
        def bit_body(b, u):
            cand = u | (jnp.int32(1) << (31 - b))
            cf = _ordered_bits_to_float(cand)
            n_ge = count(n, lambda s, _: jnp.where(s >= cf, 1.0, 0.0))
            return jnp.where(n_ge >= float(TOPK), cand, u)
        u = lax.fori_loop(0, 32, bit_body, jnp.zeros((1, Q_BLOCK), jnp.int32))
        thr = _ordered_bits_to_float(u)
        thr_scr[...] = jnp.broadcast_to(thr, thr_scr.shape)

        n_ge = count(n, lambda s, _: jnp.where(s >= thr, 1.0, 0.0))
        any_tie = jnp.max(jnp.where(n_ge > float(TOPK), 1.0, 0.0))

        @pl.when(any_tie > 0.0)
        def _():
            n_gt = count(n, lambda s, _: jnp.where(s > thr, 1.0, 0.0))
            need = float(TOPK) - n_gt

            def cut_body(b, cut):
                cand = cut | (jnp.int32(1) << (INDEX_BITS - 1 - b))
                n_eq = count(n, lambda s, kp: jnp.where(s == thr, jnp.where(kp < cand, 1.0, 0.0), 0.0))
                return jnp.where(n_eq <= need, cand, cut)
            cut = lax.fori_loop(0, INDEX_BITS, cut_body, jnp.zeros((1, Q_BLOCK), jnp.int32))
            cut_scr[...] = jnp.broadcast_to(cut, cut_scr.shape)

    for n in range(2, SEQ // KEY_CHUNK + 1):
        pl.when(n_chunks == n)(functools.partial(search, n))

    bias_ref[...] = jnp.full(bias_ref.shape, MASK_NEG, BF16)
    thr = thr_scr[0:1, :]
    cut = cut_scr[0:1, :]

    def out_body(j, carry):
        r0 = pl.multiple_of(j * Q_BLOCK, Q_BLOCK)
        s = score_scr[pl.ds(r0, Q_BLOCK), :]
        kp = key_pos(r0, Q_BLOCK)
        keep = jnp.where(s > thr, 1.0, jnp.where(s == thr, jnp.where(kp < cut, 1.0, 0.0), 0.0))
        keep = jnp.where(kp < limit, keep, 0.0)
        bias_t = jnp.where(keep > 0.0, 0.0, MASK_NEG)
        bias_ref[:, pl.ds(r0, Q_BLOCK)] = bias_t.T.astype(BF16)
        return carry
    lax.fori_loop(0, i + 1, out_body, 0)


def _index(qit, wit, ki):
    nq = SEQ // Q_BLOCK
    return pl.pallas_call(
        _index_kernel,
        grid=(BATCH, nq),
        in_specs=[
            pl.BlockSpec((IDX_HEADS * IDX_DIM, Q_BLOCK), lambda b, i: (0, b * nq + i)),
            pl.BlockSpec((SUBLANES, Q_BLOCK), lambda b, i: (0, b * nq + i)),
            pl.BlockSpec((SEQ, IDX_DIM), lambda b, i: (b, 0)),
        ],
        out_specs=pl.BlockSpec((Q_BLOCK, SEQ), lambda b, i: (b * nq + i, 0)),
        out_shape=jax.ShapeDtypeStruct((TOKENS, SEQ), BF16),
        scratch_shapes=[
            pltpu.VMEM((SEQ, Q_BLOCK), F32),
            pltpu.VMEM((SUBLANES, Q_BLOCK), F32),
            pltpu.VMEM((SUBLANES, Q_BLOCK), jnp.int32),
        ],
        compiler_params=_params("parallel", "arbitrary"),
        name="dsa_index",
    )(qit, wit, ki)


def _attn_kernel(q_ref, k_ref, c_ref, bias_ref, o_ref, s_scr, m_scr, l_scr, acc_scr):
    i = pl.program_id(1)
    n_chunks = (i + 2) // 2
    rows = A_HEADS * Q_BLOCK
    q = q_ref[...].reshape(rows, A_LAT)
    m_scr[...] = jnp.full(m_scr.shape, -jnp.inf, F32)
    l_scr[...] = jnp.zeros(l_scr.shape, F32)
    acc_scr[...] = jnp.zeros(acc_scr.shape, F32)

    def chunk_start(j):
        return pl.multiple_of(j * KEY_CHUNK, KEY_CHUNK)

    def lane_groups(a):
        return [a[:, g * LANES:(g + 1) * LANES] for g in range(KEY_CHUNK // LANES)]

    def logits_body(j, carry):
        r0 = chunk_start(j)
        bias = bias_ref[:, pl.ds(r0, KEY_CHUNK)].astype(F32)
        s = _dot_nt(q, k_ref[pl.ds(r0, KEY_CHUNK), :])
        s = (s.reshape(A_HEADS, Q_BLOCK, KEY_CHUNK) + bias[None]).reshape(rows, KEY_CHUNK)
        s_scr[:, pl.ds(r0, KEY_CHUNK)] = s
        m_scr[...] = functools.reduce(jnp.maximum, lane_groups(s), m_scr[...])
        return carry
    lax.fori_loop(0, n_chunks, logits_body, 0)
    m_scr[...] = jnp.broadcast_to(jnp.max(m_scr[...], axis=-1, keepdims=True), m_scr.shape)

    def pv_body(j, carry):
        r0 = chunk_start(j)
        m = m_scr[...]
        ps = [jnp.exp(sg - m) for sg in lane_groups(s_scr[:, pl.ds(r0, KEY_CHUNK)])]
        l_scr[...] = functools.reduce(jnp.add, ps, l_scr[...])
        p = jnp.concatenate(ps, axis=1).astype(BF16)
        acc_scr[...] += _dot(p, c_ref[pl.ds(r0, KEY_CHUNK), :])
        return carry
    lax.fori_loop(0, n_chunks, pv_body, 0)

    o = acc_scr[...] / jnp.sum(l_scr[...], axis=-1, keepdims=True)
    for h in range(A_HEADS):
        o_ref[:, h * A_LAT:(h + 1) * A_LAT] = o[h * Q_BLOCK:(h + 1) * Q_BLOCK, :].astype(BF16)


def _attn(q4, k, c, bias):
    nq = SEQ // Q_BLOCK
    rows = A_HEADS * Q_BLOCK
    return pl.pallas_call(
        _attn_kernel,
        grid=(BATCH, nq),
        in_specs=[
            pl.BlockSpec((A_HEADS, Q_BLOCK, A_LAT), lambda b, i: (0, b * nq + i, 0)),
            pl.BlockSpec((SEQ, A_LAT), lambda b, i: (b, 0)),
            pl.BlockSpec((SEQ, A_LAT), lambda b, i: (b, 0)),
            pl.BlockSpec((Q_BLOCK, SEQ), lambda b, i: (b * nq + i, 0)),
        ],
        out_specs=pl.BlockSpec((Q_BLOCK, A_Q_COLS), lambda b, i: (b * nq + i, 0)),
        out_shape=jax.ShapeDtypeStruct((TOKENS, A_Q_COLS), BF16),
        scratch_shapes=[
            pltpu.VMEM((rows, SEQ), F32),
            pltpu.VMEM((rows, LANES), F32),
            pltpu.VMEM((rows, LANES), F32),
            pltpu.VMEM((rows, A_LAT), F32),
        ],
        compiler_params=_params("parallel", "arbitrary"),
        name="dsa_attn",
    )(q4, k, c, bias)


def _out_proj_kernel(x_ref, a_ref, w_ref, o_ref):
    o_ref[...] = x_ref[...] + _dot(a_ref[...], w_ref[...])


def _out_proj(x2, a, w):
    n = TOKENS // ROW_TILE
    k = a.shape[1]
    return pl.pallas_call(
        _out_proj_kernel,
        grid=(n,),
        in_specs=[
            pl.BlockSpec((ROW_TILE, D_MODEL), lambda i: (i, 0)),
            pl.BlockSpec((ROW_TILE, k), lambda i: (i, 0)),
            _const_spec((k, D_MODEL)),
        ],
        out_specs=pl.BlockSpec((ROW_TILE, D_MODEL), lambda i: (i, 0)),
        out_shape=jax.ShapeDtypeStruct((TOKENS, D_MODEL), F32),
        compiler_params=_params("parallel"),
        name="out_proj",
    )(x2, a, w)


def _ffn_kernel(x_ref, g_ref, wup_ref, cw_ref, cb_ref, wdown_ref, o_ref, ext_scr, carry_scr, acc_scr):
    t = pl.program_id(0)
    x = x_ref[...]
    hb = _rms(x, g_ref[...]).astype(BF16)
    acc_scr[...] = jnp.zeros(acc_scr.shape, F32)
    first_of_sequence = (t % (SEQ // ROW_TILE)) == 0

    def body(c, carry):
        u = _dot(hb, wup_ref[c])
        prev = jnp.where(first_of_sequence, 0.0, carry_scr[c])
        ext_scr[0:SUBLANES, :] = prev
        ext_scr[SUBLANES:, :] = u
        carry_scr[c] = u[ROW_TILE - SUBLANES:, :]
        cw = cw_ref[c]
        v = (u * cw[2:3, :]
             + ext_scr[SUBLANES - 1:SUBLANES - 1 + ROW_TILE, :] * cw[1:2, :]
             + ext_scr[SUBLANES - 2:SUBLANES - 2 + ROW_TILE, :] * cw[0:1, :]
             + cb_ref[c])
        a = v[:, :FF_CHUNK]
        b = v[:, FF_CHUNK:]
        gate = (a * jax.nn.sigmoid(a) * b).astype(BF16)
        acc_scr[...] += _dot(gate, wdown_ref[c])
        return carry
    lax.fori_loop(0, N_FF_CHUNKS, body, 0)
    o_ref[...] = x + acc_scr[...]


def _ffn(x2, g, wup3, cw3, cb3, wdown3):
    n = TOKENS // ROW_TILE
    return pl.pallas_call(
        _ffn_kernel,
        grid=(n,),
        in_specs=[
            pl.BlockSpec((ROW_TILE, D_MODEL), lambda i: (i, 0)),
            _const_spec((1, D_MODEL)),
            _const_spec((N_FF_CHUNKS, D_MODEL, 2 * FF_CHUNK)),
            _const_spec((N_FF_CHUNKS, CONV_W, 2 * FF_CHUNK)),
            _const_spec((N_FF_CHUNKS, 1, 2 * FF_CHUNK)),
            _const_spec((N_FF_CHUNKS, FF_CHUNK, D_MODEL)),
        ],
        out_specs=pl.BlockSpec((ROW_TILE, D_MODEL), lambda i: (i, 0)),
        out_shape=jax.ShapeDtypeStruct((TOKENS, D_MODEL), F32),
        scratch_shapes=[
            pltpu.VMEM((ROW_TILE + SUBLANES, 2 * FF_CHUNK), F32),
            pltpu.VMEM((N_FF_CHUNKS, SUBLANES, 2 * FF_CHUNK), F32),
            pltpu.VMEM((ROW_TILE, D_MODEL), F32),
        ],
        compiler_params=_params("arbitrary"),
        name="conv_ffn",
    )(x2, g, wup3, cw3, cb3, wdown3)


def _proj_b_kernel(x_ref, g_ref, w_ref, cos_ref, sin_ref, q_out, k_out, v_out, g_out):
    hb = _rms(x_ref[...], g_ref[...]).astype(BF16)
    cos = cos_ref[...]
    sin = sin_ref[...]
    half = B_DK // 2

    def rotary(z):
        z1, z2 = z[:, :half], z[:, half:]
        return z1 * cos - z2 * sin, z1 * sin + z2 * cos

    dq = B_HEADS * B_DK
    for h in range(B_HEADS):
        zq = _dot(hb, w_ref[:, h * B_DK:(h + 1) * B_DK])
        r1, r2 = rotary(zq)
        q_out[:, h * B_DK:h * B_DK + half] = r1.astype(BF16)
        q_out[:, h * B_DK + half:(h + 1) * B_DK] = r2.astype(BF16)
        zk = _dot(hb, w_ref[:, dq + h * B_DK:dq + (h + 1) * B_DK])
        r1, r2 = rotary(zk)
        k_out[:, h * B_DK:h * B_DK + half] = (r1 * (B_DK ** -0.5)).astype(BF16)
        k_out[:, h * B_DK + half:(h + 1) * B_DK] = (r2 * (B_DK ** -0.5)).astype(BF16)
    dv = B_HEADS * B_DV
    for h in range(B_HEADS):
        v_out[:, h * B_DV:(h + 1) * B_DV] = _dot(
            hb, w_ref[:, 2 * dq + h * B_DV:2 * dq + (h + 1) * B_DV]).astype(BF16)
        g_out[:, h * B_DV:(h + 1) * B_DV] = _dot(
            hb, w_ref[:, 2 * dq + dv + h * B_DV:2 * dq + dv + (h + 1) * B_DV]).astype(BF16)


def _proj_b(x2, g, w, cos, sin):
    n = TOKENS // ROW_TILE
    per_seq = SEQ // ROW_TILE
    dq = B_HEADS * B_DK
    dv = B_HEADS * B_DV
    return pl.pallas_call(
        _proj_b_kernel,
        grid=(n,),
        in_specs=[
            pl.BlockSpec((ROW_TILE, D_MODEL), lambda i: (i, 0)),
            _const_spec((1, D_MODEL)),
            _const_spec((D_MODEL, 2 * dq + 2 * dv)),
            pl.BlockSpec((ROW_TILE, B_DK // 2), lambda i: (i % per_seq, 0)),
            pl.BlockSpec((ROW_TILE, B_DK // 2), lambda i: (i % per_seq, 0)),
        ],
        out_specs=[
            pl.BlockSpec((ROW_TILE, dq), lambda i: (i, 0)),
            pl.BlockSpec((ROW_TILE, dq), lambda i: (i, 0)),
            pl.BlockSpec((ROW_TILE, dv), lambda i: (i, 0)),
            pl.BlockSpec((ROW_TILE, dv), lambda i: (i, 0)),
        ],
        out_shape=[
            jax.ShapeDtypeStruct((TOKENS, dq), BF16),
            jax.ShapeDtypeStruct((TOKENS, dq), BF16),
            jax.ShapeDtypeStruct((TOKENS, dv), BF16),
            jax.ShapeDtypeStruct((TOKENS, dv), BF16),
        ],
        compiler_params=_params("parallel"),
        name="ret_in_proj",
    )(x2, g, w, cos, sin)


def _retention_kernel(q_ref, k_ref, v_ref, g_ref, decay_ref, xi_ref, zeta_ref, gc_ref, og_ref,
                      y_ref, state_scr):
    state_scr[...] = jnp.zeros(state_scr.shape, F32)
    decay = decay_ref[0]
    xi = xi_ref[0]
    zeta = zeta_ref[0]
    gc = gc_ref[0]
    og = og_ref[...]

    def body(n, carry):
        r0 = pl.multiple_of(n * RET_CHUNK, RET_CHUNK)
        qc = q_ref[pl.ds(r0, RET_CHUNK), :]
        kc = k_ref[pl.ds(r0, RET_CHUNK), :]
        vc = v_ref[pl.ds(r0, RET_CHUNK), :]
        state = state_scr[...]
        s = (_dot_nt(qc, kc) * decay).astype(BF16)
        ret = _dot(s, vc) + _dot(qc, state.astype(BF16)) * xi
        kz = (kc.astype(F32) * zeta).astype(BF16)
        state_scr[...] = state * gc + _dot_tn(kz, vc)
        gate = g_ref[pl.ds(r0, RET_CHUNK), :].astype(F32)
        y = _rms(ret, og) * (gate * jax.nn.sigmoid(gate))
        y_ref[pl.ds(r0, RET_CHUNK), :] = y.astype(BF16)
        return carry
    lax.fori_loop(0, SEQ // RET_CHUNK, body, 0)


def _retention(q, k, v, g, decay, xi, zeta, gc, og):
    return pl.pallas_call(
        _retention_kernel,
        grid=(BATCH, B_HEADS),
        in_specs=[
            pl.BlockSpec((SEQ, B_DK), lambda b, h: (b, h)),
            pl.BlockSpec((SEQ, B_DK), lambda b, h: (b, h)),
            pl.BlockSpec((SEQ, B_DV), lambda b, h: (b, h)),
            pl.BlockSpec((SEQ, B_DV), lambda b, h: (b, h)),
            pl.BlockSpec((1, RET_CHUNK, RET_CHUNK), lambda b, h: (h, 0, 0)),
            pl.BlockSpec((1, RET_CHUNK, B_DV), lambda b, h: (h, 0, 0)),
            pl.BlockSpec((1, RET_CHUNK, B_DK), lambda b, h: (h, 0, 0)),
            pl.BlockSpec((1, 1, B_DV), lambda b, h: (h, 0, 0)),
            pl.BlockSpec((1, B_DV), lambda b, h: (0, h)),
        ],
        out_specs=pl.BlockSpec((SEQ, B_DV), lambda b, h: (b, h)),
        out_shape=jax.ShapeDtypeStruct((TOKENS, B_HEADS * B_DV), BF16),
        scratch_shapes=[pltpu.VMEM((B_DK, B_DV), F32)],
        compiler_params=_params("parallel", "parallel"),
        name="retention",
    )(q, k, v, g, decay, xi, zeta, gc, og)


def _retention_tables():
    log_gamma = jnp.log1p(-(2.0 ** (-5.0 - jnp.arange(B_HEADS, dtype=F32))))
    pos = jnp.arange(RET_CHUNK, dtype=F32)
    diff = pos[:, None] - pos[None, :]
    decay = jnp.where(diff[None] >= 0,
                      jnp.exp(jnp.maximum(diff, 0.0)[None] * log_gamma[:, None, None]), 0.0)
    xi = jnp.exp((pos + 1.0)[None, :] * log_gamma[:, None])
    zeta = jnp.exp((RET_CHUNK - 1.0 - pos)[None, :] * log_gamma[:, None])
    gc = jnp.exp(RET_CHUNK * log_gamma)
    xi_b = jnp.broadcast_to(xi[:, :, None], (B_HEADS, RET_CHUNK, B_DV))
    zeta_b = jnp.broadcast_to(zeta[:, :, None], (B_HEADS, RET_CHUNK, B_DK))
    gc_b = jnp.broadcast_to(gc[:, None, None], (B_HEADS, 1, B_DV))
    return decay, xi_b, zeta_b, gc_b


def _rotary_tables():
    inv = 1.0 / (ROPE_BASE ** (jnp.arange(0, B_DK, 2, dtype=F32) / B_DK))
    ang = jnp.arange(SEQ, dtype=F32)[:, None] * inv[None, :]
    return jnp.cos(ang), jnp.sin(ang)


def _ffn_weights(w_up, conv_w, conv_b, w_down):
    def pair(m):
        a = m[..., :D_FF].reshape(*m.shape[:-1], N_FF_CHUNKS, FF_CHUNK)
        b = m[..., D_FF:].reshape(*m.shape[:-1], N_FF_CHUNKS, FF_CHUNK)
        ab = jnp.concatenate([a, b], axis=-1)
        return jnp.moveaxis(ab, -2, 0)
    wup3 = pair(w_up).astype(BF16)
    cw3 = pair(conv_w)
    cb3 = pair(conv_b[None, :])
    wdown3 = w_down.reshape(N_FF_CHUNKS, FF_CHUNK, D_MODEL).astype(BF16)
    return wup3, cw3, cb3, wdown3


def kernel(x, norm_mix_g, norm_ffn_g, a_w_in, a_q_g, a_k_g, a_iq_g, a_ik_g, a_w_out,
           b_w_in, b_out_g, b_w_out, f_w_up, f_conv_w, f_conv_b, f_w_down):
    x2 = x.reshape(TOKENS, D_MODEL)

    w = a_w_in[0]
    o1 = A_Q_COLS
    o2 = o1 + A_LAT
    o3 = o2 + IDX_HEADS * IDX_DIM
    o4 = o3 + IDX_DIM
    wq = w[:, :o1].astype(BF16)
    wck = jnp.concatenate(
        [w[:, o1:o2], w[:, o3:o4], jnp.zeros((D_MODEL, A_LAT - IDX_DIM), F32)], axis=1).astype(BF16)
    wt = jnp.concatenate([w[:, o2:o3], w[:, o4:]], axis=1).T.astype(BF16)
    q4, k, c, qit, ki, wit = _proj_a(
        x2, norm_mix_g[0][None, :], wq, wck, wt,
        a_q_g[0][None, :], a_k_g[0][None, :], a_iq_g[0][:, None], a_ik_g[0][None, :])
    bias = _index(qit, wit, ki)
    o = _attn(q4, k, c, bias)
    x2 = _out_proj(x2, o, a_w_out[0].astype(BF16))
    x2 = _ffn(x2, norm_ffn_g[0][None, :], *_ffn_weights(f_w_up[0], f_conv_w[0], f_conv_b[0], f_w_down[0]))

    cos, sin = _rotary_tables()
    q, kk, v, g = _proj_b(x2, norm_mix_g[1][None, :], b_w_in[0].astype(BF16), cos, sin)
    y = _retention(q, kk, v, g, *_retention_tables(), b_out_g[0][None, :])
    x2 = _out_proj(x2, y, b_w_out[0].astype(BF16))
    x2 = _ffn(x2, norm_ffn_g[1][None, :], *_ffn_weights(f_w_up[1], f_conv_w[1], f_conv_b[1], f_w_down[1]))
    return x2.reshape(BATCH, SEQ, D_MODEL)
```

```python
import functools

import jax
import jax.numpy as jnp
from jax import lax
from jax.experimental import pallas as pl
from jax.experimental.pallas import tpu as pltpu

D_MODEL = 1024
BATCH = 8
SEQ = 2048
TOKENS = BATCH * SEQ
CHUNK = 64
EPS = 1e-6
A_HEADS = 8
A_LAT = 128
IDX_HEADS = 8
IDX_DIM = 64
TOPK = 256
A_Q_COLS = A_HEADS * A_LAT
B_HEADS = 4
B_DK = 256
B_DV = 512
ROPE_BASE = 10000.0
RET_CHUNK = 256
D_FF = 2816
CONV_W = 3
FF_CHUNK = 256
N_FF_CHUNKS = D_FF // FF_CHUNK

LANES = 128
SUBLANES = 8
ROW_TILE = 512
Q_BLOCK = 128
KEY_CHUNK = 256
COUNT_ROWS = 32
INDEX_BITS = 12
VMEM_LIMIT = 56 * 1024 * 1024
MASK_NEG = -1e30

BF16 = jnp.bfloat16
F32 = jnp.float32


def _params(*sem):
    return pltpu.CompilerParams(dimension_semantics=sem, vmem_limit_bytes=VMEM_LIMIT)


def _const_spec(shape):
    nd = len(shape)
    return pl.BlockSpec(shape, lambda *_: (0,) * nd, pipeline_mode=pl.Buffered(1))


def _rms(x, g):
    return x * lax.rsqrt(jnp.mean(x * x, axis=-1, keepdims=True) + EPS) * g


def _dot(a, b):
    return jnp.dot(a, b, preferred_element_type=F32)


def _dot_nt(a, b):
    return lax.dot_general(a, b, (((1,), (1,)), ((), ())), preferred_element_type=F32)


def _dot_tn(a, b):
    return lax.dot_general(a, b, (((0,), (0,)), ((), ())), preferred_element_type=F32)


T_Q = 0
T_C = T_Q + A_Q_COLS
T_QI = T_C + A_LAT
T_WI = T_QI + IDX_HEADS * IDX_DIM
T_ROWS = T_WI + IDX_HEADS


def _proj_a_kernel(x_ref, g_ref, wck_ref, wt_ref, qg_ref, kg_ref, iqg_ref, ikg_ref,
                   qt_out, k_out, ct_out, qit_out, ki_out, wit_out):
    hb = _rms(x_ref[...], g_ref[...]).astype(BF16)
    ck = _dot(hb, wck_ref[...])
    k_out[...] = _rms(ck[:, :A_LAT], kg_ref[...]).astype(BF16)
    ki_out[...] = _rms(ck[:, A_LAT:A_LAT + IDX_DIM], ikg_ref[...]).astype(BF16)

    t = _dot_nt(wt_ref[...], hb)

    def head_norm(rows, gain):
        return rows * lax.rsqrt(jnp.mean(rows * rows, axis=0, keepdims=True) + EPS) * gain

    qg = qg_ref[...] * (A_LAT ** -0.5)
    for h in range(A_HEADS):
        r0 = T_Q + h * A_LAT
        qt_out[h * A_LAT:(h + 1) * A_LAT, :] = head_norm(t[r0:r0 + A_LAT, :], qg).astype(BF16)
    ct_out[...] = t[T_C:T_C + A_LAT, :].astype(BF16)
    for h in range(IDX_HEADS):
        r0 = T_QI + h * IDX_DIM
        qit_out[h * IDX_DIM:(h + 1) * IDX_DIM, :] = head_norm(t[r0:r0 + IDX_DIM, :], iqg_ref[...]).astype(BF16)
    wit_out[...] = t[T_WI:, :] * (IDX_HEADS ** -0.5 * IDX_DIM ** -0.5)


def _proj_a(x2, g, wck, wt, qg, kg, iqg, ikg):
    n = TOKENS // ROW_TILE
    return pl.pallas_call(
        _proj_a_kernel,
        grid=(n,),
        in_specs=[
            pl.BlockSpec((ROW_TILE, D_MODEL), lambda i: (i, 0)),
            _const_spec((1, D_MODEL)),
            _const_spec((D_MODEL, 2 * A_LAT)),
            _const_spec((T_ROWS, D_MODEL)),
            _const_spec((A_LAT, 1)),
            _const_spec((1, A_LAT)),
            _const_spec((IDX_DIM, 1)),
            _const_spec((1, IDX_DIM)),
        ],
        out_specs=[
            pl.BlockSpec((A_Q_COLS, ROW_TILE), lambda i: (0, i)),
            pl.BlockSpec((ROW_TILE, A_LAT), lambda i: (i, 0)),
            pl.BlockSpec((A_LAT, ROW_TILE), lambda i: (0, i)),
            pl.BlockSpec((IDX_HEADS * IDX_DIM, ROW_TILE), lambda i: (0, i)),
            pl.BlockSpec((ROW_TILE, IDX_DIM), lambda i: (i, 0)),
            pl.BlockSpec((IDX_HEADS, ROW_TILE), lambda i: (0, i)),
        ],
        out_shape=[
            jax.ShapeDtypeStruct((A_Q_COLS, TOKENS), BF16),
            jax.ShapeDtypeStruct((TOKENS, A_LAT), BF16),
            jax.ShapeDtypeStruct((A_LAT, TOKENS), BF16),
            jax.ShapeDtypeStruct((IDX_HEADS * IDX_DIM, TOKENS), BF16),
            jax.ShapeDtypeStruct((TOKENS, IDX_DIM), BF16),
            jax.ShapeDtypeStruct((IDX_HEADS, TOKENS), F32),
        ],
        compiler_params=_params("parallel"),
        name="dsa_in_proj",
    )(x2, g, wck, wt, qg, kg, iqg, ikg)


def _ordered_bits_to_float(u):
    k = u ^ jnp.int32(-2 ** 31)
    bits = k ^ ((k >> 31) & jnp.int32(0x7FFFFFFF))
    return pltpu.bitcast(bits, F32)


def _dsa_kernel(qt_ref, qit_ref, wit_ref, ki_ref, k_ref, ct_ref, ot_ref,
                score_scr, thr_scr, cut_scr, s_scr, m_scr, l_scr, acc_scr):
    i = pl.program_id(1)
    n_chunks = (i + 2) // 2
    cols = A_HEADS * Q_BLOCK
    qpos = i * Q_BLOCK + lax.broadcasted_iota(jnp.int32, (1, Q_BLOCK), 1)
    limit = ((qpos >> 6) + 1) << 6

    def key_pos(r0, rows):
        return r0 + lax.broadcasted_iota(jnp.int32, (rows, Q_BLOCK), 0)

    def for_chunks(fn):
        def pair(jj, carry):
            r0 = pl.multiple_of(jj * (2 * KEY_CHUNK), 2 * KEY_CHUNK)
            fn(r0)
            fn(pl.multiple_of(r0 + KEY_CHUNK, KEY_CHUNK))
            return carry
        lax.fori_loop(0, n_chunks // 2, pair, 0)

        @pl.when(n_chunks % 2 == 1)
        def _():
            fn(pl.multiple_of((n_chunks - 1) * KEY_CHUNK, KEY_CHUNK))

    def count(n, pred_fn):
        acc = None
        for j in range(n):
            r0 = j * KEY_CHUNK
            hit = pred_fn(score_scr[r0:r0 + KEY_CHUNK, :], key_pos(r0, KEY_CHUNK))
            part = jnp.sum(hit.reshape(KEY_CHUNK // COUNT_ROWS, COUNT_ROWS, Q_BLOCK), axis=0)
            acc = part if acc is None else acc + part
        return jnp.sum(acc, axis=0, keepdims=True)

    thr_scr[...] = jnp.full(thr_scr.shape, -jnp.inf, F32)
    cut_scr[...] = jnp.full(cut_scr.shape, 2 * SEQ, jnp.int32)

    @pl.when(i < 2)
    def _():
        score_scr[0:KEY_CHUNK, :] = jnp.zeros((KEY_CHUNK, Q_BLOCK), F32)

    @pl.when(i >= 2)
    def _():
        def score_chunk(r0):
            kic = ki_ref[pl.ds(r0, KEY_CHUNK), :]
            acc = jnp.zeros((KEY_CHUNK, Q_BLOCK), F32)
            for h in range(0, IDX_HEADS, 2):
                pair = jnp.concatenate([qit_ref[h * IDX_DIM:(h + 1) * IDX_DIM, :],
                                        qit_ref[(h + 1) * IDX_DIM:(h + 2) * IDX_DIM, :]], axis=1)
                rel = jnp.maximum(_dot(kic, pair), 0.0)
                acc = (acc + rel[:, :Q_BLOCK] * wit_ref[h:h + 1, :]
                       + rel[:, Q_BLOCK:] * wit_ref[h + 1:h + 2, :])
            score_scr[pl.ds(r0, KEY_CHUNK), :] = jnp.where(key_pos(r0, KEY_CHUNK) < limit, acc, -jnp.inf)
        for_chunks(score_chunk)

    def search(n):
        def bit_body(b, u):
            cand = u | (jnp.int32(1) << (31 - b))
            cf = _ordered_bits_to_float(cand)
            n_ge = count(n, lambda s, _: jnp.where(s >= cf, 1.0, 0.0))
            return jnp.where(n_ge >= float(TOPK), cand, u)
        u = lax.fori_loop(0, 32, bit_body, jnp.zeros((1, Q_BLOCK), jnp.int32))
        thr = _ordered_bits_to_float(u)
        thr_scr[...] = jnp.broadcast_to(thr, thr_scr.shape)

        n_ge = count(n, lambda s, _: jnp.where(s >= thr, 1.0, 0.0))
        any_tie = jnp.max(jnp.where(n_ge > float(TOPK), 1.0, 0.0))

        @pl.when(any_tie > 0.0)
        def _():
            n_gt = count(n, lambda s, _: jnp.where(s > thr, 1.0, 0.0))
            need = float(TOPK) - n_gt

            def cut_body(b, cut):
                cand = cut | (jnp.int32(1) << (INDEX_BITS - 1 - b))
                n_eq = count(n, lambda s, kp: jnp.where(s == thr, jnp.where(kp < cand, 1.0, 0.0), 0.0))
                return jnp.where(n_eq <= need, cand, cut)
            cut = lax.fori_loop(0, INDEX_BITS, cut_body, jnp.zeros((1, Q_BLOCK), jnp.int32))
            cut_scr[...] = jnp.broadcast_to(cut, cut_scr.shape)

    for n in range(2, SEQ // KEY_CHUNK + 1):
        pl.when(n_chunks == n)(functools.partial(search, n))

    thr = thr_scr[0:1, :]
    cut = cut_scr[0:1, :]
    qt = jnp.concatenate([qt_ref[h * A_LAT:(h + 1) * A_LAT, :] for h in range(A_HEADS)], axis=1)

    def row_groups(a):
        return a.reshape(KEY_CHUNK // SUBLANES, SUBLANES, cols)

    m_scr[...] = jnp.full(m_scr.shape, -jnp.inf, F32)

    def logits_chunk(r0):
        st = _dot(k_ref[pl.ds(r0, KEY_CHUNK), :], qt)
        sc = score_scr[pl.ds(r0, KEY_CHUNK), :]
        kp = key_pos(r0, KEY_CHUNK)
        tie_bias = jnp.where(sc == thr, jnp.where(kp < cut, 0.0, MASK_NEG), MASK_NEG)
        bias = jnp.where(kp < limit, jnp.where(sc > thr, 0.0, tie_bias), MASK_NEG)
        st = jnp.concatenate([st[:, h * Q_BLOCK:(h + 1) * Q_BLOCK] + bias for h in range(A_HEADS)], axis=1)
        s_scr[pl.ds(r0, KEY_CHUNK), :] = st
        m_scr[...] = jnp.maximum(m_scr[...], jnp.max(row_groups(st), axis=0))
    for_chunks(logits_chunk)
    m = jnp.max(m_scr[...], axis=0, keepdims=True)

    l_scr[...] = jnp.zeros(l_scr.shape, F32)
    acc_scr[...] = jnp.zeros(acc_scr.shape, F32)

    def pv_chunk(r0):
        p = jnp.exp(s_scr[pl.ds(r0, KEY_CHUNK), :] - m)
        l_scr[...] += jnp.sum(row_groups(p), axis=0)
        acc_scr[...] += _dot(ct_ref[:, pl.ds(r0, KEY_CHUNK)], p.astype(BF16))
    for_chunks(pv_chunk)

    o = acc_scr[...] / jnp.sum(l_scr[...], axis=0, keepdims=True)
    for h in range(A_HEADS):
        ot_ref[h * A_LAT:(h + 1) * A_LAT, :] = o[:, h * Q_BLOCK:(h + 1) * Q_BLOCK].astype(BF16)


def _dsa(qt, qit, wit, ki, k, ct):
    nq = SEQ // Q_BLOCK
    cols = A_HEADS * Q_BLOCK
    return pl.pallas_call(
        _dsa_kernel,
        grid=(BATCH, nq),
        in_specs=[
            pl.BlockSpec((A_Q_COLS, Q_BLOCK), lambda b, i: (0, b * nq + i)),
            pl.BlockSpec((IDX_HEADS * IDX_DIM, Q_BLOCK), lambda b, i: (0, b * nq + i)),
            pl.BlockSpec((IDX_HEADS, Q_BLOCK), lambda b, i: (0, b * nq + i)),
            pl.BlockSpec((SEQ, IDX_DIM), lambda b, i: (b, 0)),
            pl.BlockSpec((SEQ, A_LAT), lambda b, i: (b, 0)),
            pl.BlockSpec((A_LAT, SEQ), lambda b, i: (0, b)),
        ],
        out_specs=pl.BlockSpec((A_Q_COLS, Q_BLOCK), lambda b, i: (0, b * nq + i)),
        out_shape=jax.ShapeDtypeStruct((A_Q_COLS, TOKENS), BF16),
        scratch_shapes=[
            pltpu.VMEM((SEQ, Q_BLOCK), F32),
            pltpu.VMEM((SUBLANES, Q_BLOCK), F32),
            pltpu.VMEM((SUBLANES, Q_BLOCK), jnp.int32),
            pltpu.VMEM((SEQ, cols), F32),
            pltpu.VMEM((SUBLANES, cols), F32),
            pltpu.VMEM((SUBLANES, cols), F32),
            pltpu.VMEM((A_LAT, cols), F32),
        ],
        compiler_params=_params("parallel", "arbitrary"),
        name="dsa_core",
    )(qt, qit, wit, ki, k, ct)


SEG = ROW_TILE // SUBLANES
HALO = (CONV_W - 1) * SUBLANES


def _interleave_rows(a):
    return jnp.transpose(a.reshape(SUBLANES, SEG, a.shape[-1]), (1, 0, 2)).reshape(a.shape)


def _deinterleave_rows(a):
    return jnp.transpose(a.reshape(SEG, SUBLANES, a.shape[-1]), (1, 0, 2)).reshape(a.shape)


def _ffn_kernel(m_transposed, x_ref, m_ref, wout_ref, g_ref, wup_ref, cw_ref, cb_ref, wdown_ref, o_ref,
                hb_scr, acc_scr, u_even, u_odd, gate_even, gate_odd, carry_scr):
    t = pl.program_id(0)
    first_of_sequence = (t % (SEQ // ROW_TILE)) == 0
    mixer = (_dot_tn if m_transposed else _dot)(m_ref[...], wout_ref[...])
    x1 = x_ref[...] + mixer
    o_ref[...] = x1
    hb_scr[...] = _interleave_rows(_rms(x1, g_ref[...])).astype(BF16)
    first_sublane = lax.broadcasted_iota(jnp.int32, (HALO, 2 * FF_CHUNK), 0) % SUBLANES == 0

    def up(c):
        return _dot(hb_scr[...], wup_ref[c])

    def down_one_sublane(a):
        return jnp.concatenate([pltpu.roll(a[r:r + SUBLANES, :], 1, 0)
                                for r in range(0, a.shape[0], SUBLANES)], axis=0)

    def act(c, u):
        tail = u[ROW_TILE - HALO:, :]
        prev = jnp.where(first_of_sequence, 0.0, carry_scr[c])
        carry_scr[c] = tail
        wrap = jnp.where(first_sublane, down_one_sublane(prev), down_one_sublane(tail))
        cw = cw_ref[c]
        v = u * cw[CONV_W - 1:CONV_W, :] + cb_ref[c]
        for d in range(1, CONV_W):
            shifted = jnp.concatenate([wrap[HALO - d * SUBLANES:, :], u[:ROW_TILE - d * SUBLANES, :]], axis=0)
            v = v + shifted * cw[CONV_W - 1 - d:CONV_W - d, :]
        a = v[:, :FF_CHUNK]
        b = v[:, FF_CHUNK:]
        return (a * jax.nn.sigmoid(a) * b).astype(BF16)

    def down(c, gate):
        acc_scr[...] += _dot(gate, wdown_ref[c])

    def stage(c, u_cur, u_next, gate_cur, gate_prev, has_up=True, has_down=True):
        if has_up:
            u_next[...] = up(c + 1)
        gate_cur[...] = act(c, u_cur[...])
        if has_down:
            down(c - 1, gate_prev[...])

    def even_stage(c, **kw):
        stage(c, u_even, u_odd, gate_even, gate_odd, **kw)

    def odd_stage(c, **kw):
        stage(c, u_odd, u_even, gate_odd, gate_even, **kw)

    acc_scr[...] = jnp.zeros(acc_scr.shape, F32)
    u_even[...] = up(0)
    even_stage(0, has_down=False)

    def pair(i, carry):
        c = 2 * i + 1
        odd_stage(c)
        even_stage(c + 1)
        return carry
    n_pairs = (N_FF_CHUNKS - 3) // 2
    lax.fori_loop(0, n_pairs, pair, 0)
    odd_stage(N_FF_CHUNKS - 2)
    even_stage(N_FF_CHUNKS - 1, has_up=False)
    down(N_FF_CHUNKS - 1, gate_even[...])
    o_ref[...] += _deinterleave_rows(acc_scr[...])


def _ffn(x2, m, m_transposed, wout, g, wup3, cw3, cb3, wdown3):
    assert N_FF_CHUNKS % 2 == 1 and N_FF_CHUNKS >= 3
    n = TOKENS // ROW_TILE
    km = wout.shape[0]
    m_spec = (pl.BlockSpec((km, ROW_TILE), lambda i: (0, i)) if m_transposed
              else pl.BlockSpec((ROW_TILE, km), lambda i: (i, 0)))
    return pl.pallas_call(
        functools.partial(_ffn_kernel, m_transposed),
        grid=(n,),
        in_specs=[
            pl.BlockSpec((ROW_TILE, D_MODEL), lambda i: (i, 0)),
            m_spec,
            _const_spec((km, D_MODEL)),
            _const_spec((1, D_MODEL)),
            _const_spec((N_FF_CHUNKS, D_MODEL, 2 * FF_CHUNK)),
            _const_spec((N_FF_CHUNKS, CONV_W, 2 * FF_CHUNK)),
            _const_spec((N_FF_CHUNKS, 1, 2 * FF_CHUNK)),
            _const_spec((N_FF_CHUNKS, FF_CHUNK, D_MODEL)),
        ],
        out_specs=pl.BlockSpec((ROW_TILE, D_MODEL), lambda i: (i, 0)),
        out_shape=jax.ShapeDtypeStruct((TOKENS, D_MODEL), F32),
        scratch_shapes=[
            pltpu.VMEM((ROW_TILE, D_MODEL), BF16),
            pltpu.VMEM((ROW_TILE, D_MODEL), F32),
            pltpu.VMEM((ROW_TILE, 2 * FF_CHUNK), F32),
            pltpu.VMEM((ROW_TILE, 2 * FF_CHUNK), F32),
            pltpu.VMEM((ROW_TILE, FF_CHUNK), BF16),
            pltpu.VMEM((ROW_TILE, FF_CHUNK), BF16),
            pltpu.VMEM((N_FF_CHUNKS, HALO, 2 * FF_CHUNK), F32),
        ],
        compiler_params=_params("arbitrary"),
        name="conv_ffn",
    )(x2, m, wout, g, wup3, cw3, cb3, wdown3)


def _proj_b_kernel(x_ref, g_ref, w_ref, cos_ref, sin_ref, q_out, k_out, v_out, g_out):
    hb = _rms(x_ref[...], g_ref[...]).astype(BF16)
    cos = cos_ref[...]
    sin = sin_ref[...]
    half = B_DK // 2

    def rotary(z):
        z1, z2 = z[:, :half], z[:, half:]
        return z1 * cos - z2 * sin, z1 * sin + z2 * cos

    dq = B_HEADS * B_DK
    for h in range(B_HEADS):
        zq = _dot(hb, w_ref[:, h * B_DK:(h + 1) * B_DK])
        r1, r2 = rotary(zq)
        q_out[:, h * B_DK:h * B_DK + half] = r1.astype(BF16)
        q_out[:, h * B_DK + half:(h + 1) * B_DK] = r2.astype(BF16)
        zk = _dot(hb, w_ref[:, dq + h * B_DK:dq + (h + 1) * B_DK])
        r1, r2 = rotary(zk)
        k_out[:, h * B_DK:h * B_DK + half] = (r1 * (B_DK ** -0.5)).astype(BF16)
        k_out[:, h * B_DK + half:(h + 1) * B_DK] = (r2 * (B_DK ** -0.5)).astype(BF16)
    dv = B_HEADS * B_DV
    for h in range(B_HEADS):
        v_out[:, h * B_DV:(h + 1) * B_DV] = _dot(
            hb, w_ref[:, 2 * dq + h * B_DV:2 * dq + (h + 1) * B_DV]).astype(BF16)
        g_out[:, h * B_DV:(h + 1) * B_DV] = _dot(
            hb, w_ref[:, 2 * dq + dv + h * B_DV:2 * dq + dv + (h + 1) * B_DV]).astype(BF16)


def _proj_b(x2, g, w, cos, sin):
    n = TOKENS // ROW_TILE
    per_seq = SEQ // ROW_TILE
    dq = B_HEADS * B_DK
    dv = B_HEADS * B_DV
    return pl.pallas_call(
        _proj_b_kernel,
        grid=(n,),
        in_specs=[
            pl.BlockSpec((ROW_TILE, D_MODEL), lambda i: (i, 0)),
            _const_spec((1, D_MODEL)),
            _const_spec((D_MODEL, 2 * dq + 2 * dv)),
            pl.BlockSpec((ROW_TILE, B_DK // 2), lambda i: (i % per_seq, 0)),
            pl.BlockSpec((ROW_TILE, B_DK // 2), lambda i: (i % per_seq, 0)),
        ],
        out_specs=[
            pl.BlockSpec((ROW_TILE, dq), lambda i: (i, 0)),
            pl.BlockSpec((ROW_TILE, dq), lambda i: (i, 0)),
            pl.BlockSpec((ROW_TILE, dv), lambda i: (i, 0)),
            pl.BlockSpec((ROW_TILE, dv), lambda i: (i, 0)),
        ],
        out_shape=[
            jax.ShapeDtypeStruct((TOKENS, dq), BF16),
            jax.ShapeDtypeStruct((TOKENS, dq), BF16),
            jax.ShapeDtypeStruct((TOKENS, dv), BF16),
            jax.ShapeDtypeStruct((TOKENS, dv), BF16),
        ],
        compiler_params=_params("parallel"),
        name="ret_in_proj",
    )(x2, g, w, cos, sin)


RET_HEADS_PER_STEP = 2


def _retention_kernel(q_ref, k_ref, v_ref, g_ref, decay_ref, xi_ref, zeta_ref, gc_ref, og_ref,
                      y_ref, state_scr):
    state_scr[...] = jnp.zeros(state_scr.shape, F32)

    def body(n, carry):
        r0 = pl.multiple_of(n * RET_CHUNK, RET_CHUNK)
        for hh in range(RET_HEADS_PER_STEP):
            kcols = slice(hh * B_DK, (hh + 1) * B_DK)
            vcols = slice(hh * B_DV, (hh + 1) * B_DV)
            qc = q_ref[pl.ds(r0, RET_CHUNK), kcols]
            kc = k_ref[pl.ds(r0, RET_CHUNK), kcols]
            vc = v_ref[pl.ds(r0, RET_CHUNK), vcols]
            state = state_scr[hh]
            s = (_dot_nt(qc, kc) * decay_ref[hh]).astype(BF16)
            ret = _dot(s, vc) + _dot(qc, state.astype(BF16)) * xi_ref[hh]
            kz = (kc.astype(F32) * zeta_ref[hh]).astype(BF16)
            state_scr[hh] = state * gc_ref[hh] + _dot_tn(kz, vc)
            gate = g_ref[pl.ds(r0, RET_CHUNK), vcols].astype(F32)
            y = _rms(ret, og_ref[:, vcols]) * (gate * jax.nn.sigmoid(gate))
            y_ref[pl.ds(r0, RET_CHUNK), vcols] = y.astype(BF16)
        return carry
    lax.fori_loop(0, SEQ // RET_CHUNK, body, 0)


def _retention(q, k, v, g, decay, xi, zeta, gc, og):
    hs = RET_HEADS_PER_STEP
    return pl.pallas_call(
        _retention_kernel,
        grid=(BATCH, B_HEADS // hs),
        in_specs=[
            pl.BlockSpec((SEQ, hs * B_DK), lambda b, h: (b, h)),
            pl.BlockSpec((SEQ, hs * B_DK), lambda b, h: (b, h)),
            pl.BlockSpec((SEQ, hs * B_DV), lambda b, h: (b, h)),
            pl.BlockSpec((SEQ, hs * B_DV), lambda b, h: (b, h)),
            pl.BlockSpec((hs, RET_CHUNK, RET_CHUNK), lambda b, h: (h, 0, 0)),
            pl.BlockSpec((hs, RET_CHUNK, B_DV), lambda b, h: (h, 0, 0)),
            pl.BlockSpec((hs, RET_CHUNK, B_DK), lambda b, h: (h, 0, 0)),
            pl.BlockSpec((hs, 1, B_DV), lambda b, h: (h, 0, 0)),
            pl.BlockSpec((1, hs * B_DV), lambda b, h: (0, h)),
        ],
        out_specs=pl.BlockSpec((SEQ, hs * B_DV), lambda b, h: (b, h)),
        out_shape=jax.ShapeDtypeStruct((TOKENS, B_HEADS * B_DV), BF16),
        scratch_shapes=[pltpu.VMEM((hs, B_DK, B_DV), F32)],
        compiler_params=_params("parallel", "parallel"),
        name="retention",
    )(q, k, v, g, decay, xi, zeta, gc, og)


def _retention_tables():
    log_gamma = jnp.log1p(-(2.0 ** (-5.0 - jnp.arange(B_HEADS, dtype=F32))))
    pos = jnp.arange(RET_CHUNK, dtype=F32)
    diff = pos[:, None] - pos[None, :]
    decay = jnp.where(diff[None] >= 0,
                      jnp.exp(jnp.maximum(diff, 0.0)[None] * log_gamma[:, None, None]), 0.0)
    xi = jnp.exp((pos + 1.0)[None, :] * log_gamma[:, None])
    zeta = jnp.exp((RET_CHUNK - 1.0 - pos)[None, :] * log_gamma[:, None])
    gc = jnp.exp(RET_CHUNK * log_gamma)
    xi_b = jnp.broadcast_to(xi[:, :, None], (B_HEADS, RET_CHUNK, B_DV))
    zeta_b = jnp.broadcast_to(zeta[:, :, None], (B_HEADS, RET_CHUNK, B_DK))
    gc_b = jnp.broadcast_to(gc[:, None, None], (B_HEADS, 1, B_DV))
    return decay, xi_b, zeta_b, gc_b


def _rotary_tables():
    inv = 1.0 / (ROPE_BASE ** (jnp.arange(0, B_DK, 2, dtype=F32) / B_DK))
    ang = jnp.arange(SEQ, dtype=F32)[:, None] * inv[None, :]
    return jnp.cos(ang), jnp.sin(ang)


def _ffn_weights(w_up, conv_w, conv_b, w_down):
    def pair(m):
        a = m[..., :D_FF].reshape(*m.shape[:-1], N_FF_CHUNKS, FF_CHUNK)
        b = m[..., D_FF:].reshape(*m.shape[:-1], N_FF_CHUNKS, FF_CHUNK)
        ab = jnp.concatenate([a, b], axis=-1)
        return jnp.moveaxis(ab, -2, 0)
    wup3 = pair(w_up).astype(BF16)
    cw3 = pair(conv_w)
    cb3 = pair(conv_b[None, :])
    wdown3 = w_down.reshape(N_FF_CHUNKS, FF_CHUNK, D_MODEL).astype(BF16)
    return wup3, cw3, cb3, wdown3


def kernel(x, norm_mix_g, norm_ffn_g, a_w_in, a_q_g, a_k_g, a_iq_g, a_ik_g, a_w_out,
           b_w_in, b_out_g, b_w_out, f_w_up, f_conv_w, f_conv_b, f_w_down):
    x2 = x.reshape(TOKENS, D_MODEL)

    w = a_w_in[0]
    o1 = A_Q_COLS
    o2 = o1 + A_LAT
    o3 = o2 + IDX_HEADS * IDX_DIM
    o4 = o3 + IDX_DIM
    wck = jnp.concatenate(
        [w[:, o1:o2], w[:, o3:o4], jnp.zeros((D_MODEL, A_LAT - IDX_DIM), F32)], axis=1).astype(BF16)
    wt = jnp.concatenate([w[:, :o3], w[:, o4:]], axis=1).T.astype(BF16)
    qt, k, ct, qit, ki, wit = _proj_a(
        x2, norm_mix_g[0][None, :], wck, wt,
        a_q_g[0][:, None], a_k_g[0][None, :], a_iq_g[0][:, None], a_ik_g[0][None, :])
    ot = _dsa(qt, qit, wit, ki, k, ct)
    x2 = _ffn(x2, ot, True, a_w_out[0].astype(BF16), norm_ffn_g[0][None, :],
              *_ffn_weights(f_w_up[0], f_conv_w[0], f_conv_b[0], f_w_down[0]))

    cos, sin = _rotary_tables()
    q, kk, v, g = _proj_b(x2, norm_mix_g[1][None, :], b_w_in[0].astype(BF16), cos, sin)
    y = _retention(q, kk, v, g, *_retention_tables(), b_out_g[0][None, :])
    x2 = _ffn(x2, y, False, b_w_out[0].astype(BF16), norm_ffn_g[1][None, :],
              *_ffn_weights(f_w_up[1], f_conv_w[1], f_conv_b[1], f_w_down[1]))
    return x2.reshape(BATCH, SEQ, D_MODEL)
```

```python
import functools

import jax
import jax.numpy as jnp
import numpy as np
from jax import lax
from jax.experimental import pallas as pl
from jax.experimental.pallas import tpu as pltpu

D_MODEL = 1024
BATCH = 8
SEQ = 2048
TOKENS = BATCH * SEQ
CHUNK = 64
EPS = 1e-6
A_HEADS = 8
A_LAT = 128
IDX_HEADS = 8
IDX_DIM = 64
TOPK = 256
A_Q_COLS = A_HEADS * A_LAT
B_HEADS = 4
B_DK = 256
B_DV = 512
ROPE_BASE = 10000.0
RET_CHUNK = 256
D_FF = 2816
CONV_W = 3
FF_CHUNK = 256
N_FF_CHUNKS = D_FF // FF_CHUNK

LANES = 128
SUBLANES = 8
ROW_TILE = 512
Q_BLOCK = 128
KEY_CHUNK = 256
COUNT_ROWS = 32
INDEX_BITS = 12
VMEM_LIMIT = 56 * 1024 * 1024
MASK_NEG = -(2.0 ** 100)
LOG2_E = 1.4426950408889634

BF16 = jnp.bfloat16
F32 = jnp.float32


def _params(*sem):
    return pltpu.CompilerParams(dimension_semantics=sem, vmem_limit_bytes=VMEM_LIMIT)


def _const_spec(shape):
    nd = len(shape)
    return pl.BlockSpec(shape, lambda *_: (0,) * nd, pipeline_mode=pl.Buffered(1))


def _rms(x, g):
    return x * lax.rsqrt(jnp.mean(x * x, axis=-1, keepdims=True) + EPS) * g


def _dot(a, b):
    return jnp.dot(a, b, preferred_element_type=F32)


def _dot_nt(a, b):
    return lax.dot_general(a, b, (((1,), (1,)), ((), ())), preferred_element_type=F32)


def _dot_tn(a, b):
    return lax.dot_general(a, b, (((0,), (0,)), ((), ())), preferred_element_type=F32)


T_Q = 0
T_C = T_Q + A_Q_COLS
T_QI = T_C + A_LAT
T_WI = T_QI + IDX_HEADS * IDX_DIM
T_ROWS = T_WI + IDX_HEADS


def _proj_a_kernel(x_ref, g_ref, wck_ref, wt_ref, qg_ref, kg_ref, iqg_ref, ikg_ref,
                   qt_out, k_out, ct_out, qit_out, ki_out, wit_out):
    hb = _rms(x_ref[...], g_ref[...]).astype(BF16)
    ck = _dot(hb, wck_ref[...])
    k_out[...] = _rms(ck[:, :A_LAT], kg_ref[...]).astype(BF16)
    ki_out[...] = _rms(ck[:, A_LAT:A_LAT + IDX_DIM], ikg_ref[...]).astype(BF16)

    t = _dot_nt(wt_ref[...], hb)

    def head_norm(rows, gain):
        return rows * lax.rsqrt(jnp.mean(rows * rows, axis=0, keepdims=True) + EPS) * gain

    qg = qg_ref[...] * (A_LAT ** -0.5 * LOG2_E)
    for h in range(A_HEADS):
        r0 = T_Q + h * A_LAT
        qt_out[h * A_LAT:(h + 1) * A_LAT, :] = head_norm(t[r0:r0 + A_LAT, :], qg).astype(BF16)
    ct_out[...] = t[T_C:T_C + A_LAT, :].astype(BF16)
    for h in range(IDX_HEADS):
        r0 = T_QI + h * IDX_DIM
        qit_out[h * IDX_DIM:(h + 1) * IDX_DIM, :] = head_norm(t[r0:r0 + IDX_DIM, :], iqg_ref[...]).astype(BF16)
    wit_out[...] = t[T_WI:, :] * (IDX_HEADS ** -0.5 * IDX_DIM ** -0.5)


def _proj_a(x2, g, wck, wt, qg, kg, iqg, ikg):
    n = TOKENS // ROW_TILE
    return pl.pallas_call(
        _proj_a_kernel,
        grid=(n,),
        in_specs=[
            pl.BlockSpec((ROW_TILE, D_MODEL), lambda i: (i, 0)),
            _const_spec((1, D_MODEL)),
            _const_spec((D_MODEL, 2 * A_LAT)),
            _const_spec((T_ROWS, D_MODEL)),
            _const_spec((A_LAT, 1)),
            _const_spec((1, A_LAT)),
            _const_spec((IDX_DIM, 1)),
            _const_spec((1, IDX_DIM)),
        ],
        out_specs=[
            pl.BlockSpec((A_Q_COLS, ROW_TILE), lambda i: (0, i)),
            pl.BlockSpec((ROW_TILE, A_LAT), lambda i: (i, 0)),
            pl.BlockSpec((A_LAT, ROW_TILE), lambda i: (0, i)),
            pl.BlockSpec((IDX_HEADS * IDX_DIM, ROW_TILE), lambda i: (0, i)),
            pl.BlockSpec((ROW_TILE, IDX_DIM), lambda i: (i, 0)),
            pl.BlockSpec((IDX_HEADS, ROW_TILE), lambda i: (0, i)),
        ],
        out_shape=[
            jax.ShapeDtypeStruct((A_Q_COLS, TOKENS), BF16),
            jax.ShapeDtypeStruct((TOKENS, A_LAT), BF16),
            jax.ShapeDtypeStruct((A_LAT, TOKENS), BF16),
            jax.ShapeDtypeStruct((IDX_HEADS * IDX_DIM, TOKENS), BF16),
            jax.ShapeDtypeStruct((TOKENS, IDX_DIM), BF16),
            jax.ShapeDtypeStruct((IDX_HEADS, TOKENS), F32),
        ],
        compiler_params=_params("parallel"),
        name="dsa_in_proj",
    )(x2, g, wck, wt, qg, kg, iqg, ikg)


def _ordered_bits_to_float(u):
    k = u ^ jnp.int32(-2 ** 31)
    bits = k ^ ((k >> 31) & jnp.int32(0x7FFFFFFF))
    return pltpu.bitcast(bits, F32)


def _dsa_kernel(qt_ref, qit_ref, wit_ref, ki_ref, k_ref, ct_ref, ot_ref,
                score_scr, thr_scr, cut_scr, s_scr, m_scr, l_scr, acc_scr):
    i = pl.program_id(1)
    n_chunks = (i + 2) // 2
    cols = A_HEADS * Q_BLOCK
    qpos = i * Q_BLOCK + lax.broadcasted_iota(jnp.int32, (1, Q_BLOCK), 1)
    limit = ((qpos >> 6) + 1) << 6

    def key_pos(r0, rows):
        return r0 + lax.broadcasted_iota(jnp.int32, (rows, Q_BLOCK), 0)

    def for_chunks(fn):
        def pair(jj, carry):
            r0 = pl.multiple_of(jj * (2 * KEY_CHUNK), 2 * KEY_CHUNK)
            fn(r0)
            fn(pl.multiple_of(r0 + KEY_CHUNK, KEY_CHUNK))
            return carry
        lax.fori_loop(0, n_chunks // 2, pair, 0)

        @pl.when(n_chunks % 2 == 1)
        def _():
            fn(pl.multiple_of((n_chunks - 1) * KEY_CHUNK, KEY_CHUNK))

    def count(n, pred_fn):
        acc = None
        for j in range(n):
            r0 = j * KEY_CHUNK
            hit = pred_fn(score_scr[r0:r0 + KEY_CHUNK, :], key_pos(r0, KEY_CHUNK))
            part = jnp.sum(hit.reshape(KEY_CHUNK // COUNT_ROWS, COUNT_ROWS, Q_BLOCK), axis=0)
            acc = part if acc is None else acc + part
        return jnp.sum(acc, axis=0, keepdims=True)

    thr_scr[...] = jnp.full(thr_scr.shape, -jnp.inf, F32)
    cut_scr[...] = jnp.full(cut_scr.shape, 2 * SEQ, jnp.int32)

    @pl.when(i < 2)
    def _():
        score_scr[0:KEY_CHUNK, :] = jnp.zeros((KEY_CHUNK, Q_BLOCK), F32)

    @pl.when(i >= 2)
    def _():
        def score_chunk(r0):
            kic = ki_ref[pl.ds(r0, KEY_CHUNK), :]
            acc = jnp.zeros((KEY_CHUNK, Q_BLOCK), F32)
            for h in range(0, IDX_HEADS, 2):
                pair = jnp.concatenate([qit_ref[h * IDX_DIM:(h + 1) * IDX_DIM, :],
                                        qit_ref[(h + 1) * IDX_DIM:(h + 2) * IDX_DIM, :]], axis=1)
                rel = jnp.maximum(_dot(kic, pair), 0.0)
                acc = (acc + rel[:, :Q_BLOCK] * wit_ref[h:h + 1, :]
                       + rel[:, Q_BLOCK:] * wit_ref[h + 1:h + 2, :])
            score_scr[pl.ds(r0, KEY_CHUNK), :] = jnp.where(key_pos(r0, KEY_CHUNK) < limit, acc, -jnp.inf)
        for_chunks(score_chunk)

    def search(n):
        def bit_body(b, u):
            cand = u | (jnp.int32(1) << (31 - b))
            cf = _ordered_bits_to_float(cand)
            n_ge = count(n, lambda s, _: jnp.where(s >= cf, 1.0, 0.0))
            return jnp.where(n_ge >= float(TOPK), cand, u)
        u = lax.fori_loop(0, 32, bit_body, jnp.zeros((1, Q_BLOCK), jnp.int32))
        thr = _ordered_bits_to_float(u)
        thr_scr[...] = jnp.broadcast_to(thr, thr_scr.shape)

        n_ge = count(n, lambda s, _: jnp.where(s >= thr, 1.0, 0.0))
        any_tie = jnp.max(jnp.where(n_ge > float(TOPK), 1.0, 0.0))

        @pl.when(any_tie > 0.0)
        def _():
            n_gt = count(n, lambda s, _: jnp.where(s > thr, 1.0, 0.0))
            need = float(TOPK) - n_gt

            def cut_body(b, cut):
                cand = cut | (jnp.int32(1) << (INDEX_BITS - 1 - b))
                n_eq = count(n, lambda s, kp: jnp.where(s == thr, jnp.where(kp < cand, 1.0, 0.0), 0.0))
                return jnp.where(n_eq <= need, cand, cut)
            cut = lax.fori_loop(0, INDEX_BITS, cut_body, jnp.zeros((1, Q_BLOCK), jnp.int32))
            cut_scr[...] = jnp.broadcast_to(cut, cut_scr.shape)

    for n in range(2, SEQ // KEY_CHUNK + 1):
        pl.when(n_chunks == n)(functools.partial(search, n))

    thr = thr_scr[0:1, :]
    cut = cut_scr[0:1, :]
    qt = jnp.concatenate([qt_ref[h * A_LAT:(h + 1) * A_LAT, :] for h in range(A_HEADS)], axis=1)
    eye = (lax.broadcasted_iota(jnp.int32, (Q_BLOCK, Q_BLOCK), 0)
           == lax.broadcasted_iota(jnp.int32, (Q_BLOCK, Q_BLOCK), 1))
    eye = jnp.where(eye, 1.0, 0.0).astype(BF16)
    qt_aug = jnp.concatenate([qt, jnp.concatenate([eye] * A_HEADS, axis=1)], axis=0)

    def row_groups(a):
        return a.reshape(KEY_CHUNK // SUBLANES, SUBLANES, cols)

    m_scr[...] = jnp.full(m_scr.shape, -jnp.inf, F32)

    def logits_chunk(r0):
        sc = score_scr[pl.ds(r0, KEY_CHUNK), :]
        kp = key_pos(r0, KEY_CHUNK)
        tie_bias = jnp.where(sc == thr, jnp.where(kp < cut, 0.0, MASK_NEG), MASK_NEG)
        bias = jnp.where(kp < limit, jnp.where(sc > thr, 0.0, tie_bias), MASK_NEG)
        k_aug = jnp.concatenate([k_ref[pl.ds(r0, KEY_CHUNK), :], bias.astype(BF16)], axis=1)
        st = _dot(k_aug, qt_aug)
        s_scr[pl.ds(r0, KEY_CHUNK), :] = st
        m_scr[...] = jnp.maximum(m_scr[...], jnp.max(row_groups(st), axis=0))
    for_chunks(logits_chunk)
    m = jnp.max(m_scr[...], axis=0, keepdims=True)

    l_scr[...] = jnp.zeros(l_scr.shape, F32)
    acc_scr[...] = jnp.zeros(acc_scr.shape, F32)

    def pv_chunk(r0):
        p = jnp.exp2(s_scr[pl.ds(r0, KEY_CHUNK), :] - m)
        l_scr[...] += jnp.sum(row_groups(p), axis=0)
        acc_scr[...] += _dot(ct_ref[:, pl.ds(r0, KEY_CHUNK)], p.astype(BF16))
    for_chunks(pv_chunk)

    o = acc_scr[...] / jnp.sum(l_scr[...], axis=0, keepdims=True)
    for h in range(A_HEADS):
        ot_ref[h * A_LAT:(h + 1) * A_LAT, :] = o[:, h * Q_BLOCK:(h + 1) * Q_BLOCK].astype(BF16)


def _dsa(qt, qit, wit, ki, k, ct):
    nq = SEQ // Q_BLOCK
    cols = A_HEADS * Q_BLOCK
    return pl.pallas_call(
        _dsa_kernel,
        grid=(BATCH, nq),
        in_specs=[
            pl.BlockSpec((A_Q_COLS, Q_BLOCK), lambda b, i: (0, b * nq + i)),
            pl.BlockSpec((IDX_HEADS * IDX_DIM, Q_BLOCK), lambda b, i: (0, b * nq + i)),
            pl.BlockSpec((IDX_HEADS, Q_BLOCK), lambda b, i: (0, b * nq + i)),
            pl.BlockSpec((SEQ, IDX_DIM), lambda b, i: (b, 0)),
            pl.BlockSpec((SEQ, A_LAT), lambda b, i: (b, 0)),
            pl.BlockSpec((A_LAT, SEQ), lambda b, i: (0, b)),
        ],
        out_specs=pl.BlockSpec((A_Q_COLS, Q_BLOCK), lambda b, i: (0, b * nq + i)),
        out_shape=jax.ShapeDtypeStruct((A_Q_COLS, TOKENS), BF16),
        scratch_shapes=[
            pltpu.VMEM((SEQ, Q_BLOCK), F32),
            pltpu.VMEM((SUBLANES, Q_BLOCK), F32),
            pltpu.VMEM((SUBLANES, Q_BLOCK), jnp.int32),
            pltpu.VMEM((SEQ, cols), F32),
            pltpu.VMEM((SUBLANES, cols), F32),
            pltpu.VMEM((SUBLANES, cols), F32),
            pltpu.VMEM((A_LAT, cols), F32),
        ],
        compiler_params=_params("parallel", "arbitrary"),
        name="dsa_core",
    )(qt, qit, wit, ki, k, ct)


SEG = ROW_TILE // SUBLANES
HALO = (CONV_W - 1) * SUBLANES


def _interleave_rows(a):
    return jnp.transpose(a.reshape(SUBLANES, SEG, a.shape[-1]), (1, 0, 2)).reshape(a.shape)


def _deinterleave_rows(a):
    return jnp.transpose(a.reshape(SEG, SUBLANES, a.shape[-1]), (1, 0, 2)).reshape(a.shape)


def _ffn_kernel(m_transposed, x_ref, m_ref, wout_ref, g_ref, wup_ref, cw_ref, cb_ref, wdown_ref, o_ref,
                hb_scr, u_even, u_odd, gate_scr, carry_scr):
    t = pl.program_id(0)
    first_of_sequence = (t % (SEQ // ROW_TILE)) == 0
    mixer = (_dot_tn if m_transposed else _dot)(m_ref[...], wout_ref[...])
    x1 = x_ref[...] + mixer
    o_ref[...] = x1
    hb_scr[...] = _interleave_rows(_rms(x1, g_ref[...])).astype(BF16)
    first_sublane = lax.broadcasted_iota(jnp.int32, (HALO, 2 * FF_CHUNK), 0) % SUBLANES == 0

    def chunk_cols(ref, c):
        if isinstance(c, int):
            a0, b0 = c * FF_CHUNK, D_FF + c * FF_CHUNK
        else:
            a0 = pl.multiple_of(c * FF_CHUNK, FF_CHUNK)
            b0 = pl.multiple_of(D_FF + c * FF_CHUNK, FF_CHUNK)
        return ref[:, pl.ds(a0, FF_CHUNK)], ref[:, pl.ds(b0, FF_CHUNK)]

    def up(c):
        hb = hb_scr[...]
        wa, wb = chunk_cols(wup_ref, c)
        return jnp.concatenate([_dot(hb, wa), _dot(hb, wb)], axis=1)

    def down_one_sublane(a):
        return jnp.concatenate([pltpu.roll(a[r:r + SUBLANES, :], 1, 0)
                                for r in range(0, a.shape[0], SUBLANES)], axis=0)

    def act(c, u):
        tail = u[ROW_TILE - HALO:, :]
        prev = jnp.where(first_of_sequence, 0.0, carry_scr[c])
        carry_scr[c] = tail
        wrap = jnp.where(first_sublane, down_one_sublane(prev), down_one_sublane(tail))
        cw = jnp.concatenate(chunk_cols(cw_ref, c), axis=1)
        cb = jnp.concatenate(chunk_cols(cb_ref, c), axis=1)
        v = u * cw[CONV_W - 1:CONV_W, :] + cb
        for d in range(1, CONV_W):
            shifted = jnp.concatenate([wrap[HALO - d * SUBLANES:, :], u[:ROW_TILE - d * SUBLANES, :]], axis=0)
            v = v + shifted * cw[CONV_W - 1 - d:CONV_W - d, :]
        a = v[:, :FF_CHUNK]
        b = v[:, FF_CHUNK:]
        return (a * jax.nn.sigmoid(a) * b).astype(BF16)

    def stage(c, u_cur, u_next, has_up=True):
        if has_up:
            u_next[...] = up(c + 1)
        g0 = c * FF_CHUNK if isinstance(c, int) else pl.multiple_of(c * FF_CHUNK, FF_CHUNK)
        gate_scr[:, pl.ds(g0, FF_CHUNK)] = act(c, u_cur[...])

    def even_stage(c, **kw):
        stage(c, u_even, u_odd, **kw)

    def odd_stage(c, **kw):
        stage(c, u_odd, u_even, **kw)

    u_even[...] = up(0)
    even_stage(0)

    def pair(i, carry):
        c = 2 * i + 1
        odd_stage(c)
        even_stage(c + 1)
        return carry
    n_pairs = (N_FF_CHUNKS - 3) // 2
    lax.fori_loop(0, n_pairs, pair, 0)
    odd_stage(N_FF_CHUNKS - 2)
    even_stage(N_FF_CHUNKS - 1, has_up=False)
    o_ref[...] += _deinterleave_rows(_dot(gate_scr[...], wdown_ref[...]))


def _ffn(x2, m, m_transposed, wout, g, wup, cw, cb, wdown):
    assert N_FF_CHUNKS % 2 == 1 and N_FF_CHUNKS >= 3
    n = TOKENS // ROW_TILE
    km = wout.shape[0]
    m_spec = (pl.BlockSpec((km, ROW_TILE), lambda i: (0, i)) if m_transposed
              else pl.BlockSpec((ROW_TILE, km), lambda i: (i, 0)))
    return pl.pallas_call(
        functools.partial(_ffn_kernel, m_transposed),
        grid=(n,),
        in_specs=[
            pl.BlockSpec((ROW_TILE, D_MODEL), lambda i: (i, 0)),
            m_spec,
            _const_spec((km, D_MODEL)),
            _const_spec((1, D_MODEL)),
            _const_spec((D_MODEL, 2 * D_FF)),
            _const_spec((CONV_W, 2 * D_FF)),
            _const_spec((1, 2 * D_FF)),
            _const_spec((D_FF, D_MODEL)),
        ],
        out_specs=pl.BlockSpec((ROW_TILE, D_MODEL), lambda i: (i, 0)),
        out_shape=jax.ShapeDtypeStruct((TOKENS, D_MODEL), F32),
        scratch_shapes=[
            pltpu.VMEM((ROW_TILE, D_MODEL), BF16),
            pltpu.VMEM((ROW_TILE, 2 * FF_CHUNK), F32),
            pltpu.VMEM((ROW_TILE, 2 * FF_CHUNK), F32),
            pltpu.VMEM((ROW_TILE, D_FF), BF16),
            pltpu.VMEM((N_FF_CHUNKS, HALO, 2 * FF_CHUNK), F32),
        ],
        compiler_params=_params("arbitrary"),
        name="conv_ffn",
    )(x2, m, wout, g, wup, cw, cb, wdown)


def _proj_b_kernel(x_ref, g_ref, w_ref, cos_ref, sin_ref, q_out, k_out, v_out, g_out):
    hb = _rms(x_ref[...], g_ref[...]).astype(BF16)
    cos = cos_ref[...]
    sin = sin_ref[...]
    half = B_DK // 2

    def rotary(z):
        z1, z2 = z[:, :half], z[:, half:]
        return z1 * cos - z2 * sin, z1 * sin + z2 * cos

    dq = B_HEADS * B_DK
    for h in range(B_HEADS):
        zq = _dot(hb, w_ref[:, h * B_DK:(h + 1) * B_DK])
        r1, r2 = rotary(zq)
        q_out[:, h * B_DK:h * B_DK + half] = r1.astype(BF16)
        q_out[:, h * B_DK + half:(h + 1) * B_DK] = r2.astype(BF16)
        zk = _dot(hb, w_ref[:, dq + h * B_DK:dq + (h + 1) * B_DK])
        r1, r2 = rotary(zk)
        k_out[:, h * B_DK:h * B_DK + half] = (r1 * (B_DK ** -0.5)).astype(BF16)
        k_out[:, h * B_DK + half:(h + 1) * B_DK] = (r2 * (B_DK ** -0.5)).astype(BF16)
    dv = B_HEADS * B_DV
    for h in range(B_HEADS):
        v_out[:, h * B_DV:(h + 1) * B_DV] = _dot(
            hb, w_ref[:, 2 * dq + h * B_DV:2 * dq + (h + 1) * B_DV]).astype(BF16)
        g_out[:, h * B_DV:(h + 1) * B_DV] = _dot(
            hb, w_ref[:, 2 * dq + dv + h * B_DV:2 * dq + dv + (h + 1) * B_DV]).astype(BF16)


def _proj_b(x2, g, w, cos, sin):
    n = TOKENS // ROW_TILE
    per_seq = SEQ // ROW_TILE
    dq = B_HEADS * B_DK
    dv = B_HEADS * B_DV
    return pl.pallas_call(
        _proj_b_kernel,
        grid=(n,),
        in_specs=[
            pl.BlockSpec((ROW_TILE, D_MODEL), lambda i: (i, 0)),
            _const_spec((1, D_MODEL)),
            _const_spec((D_MODEL, 2 * dq + 2 * dv)),
            pl.BlockSpec((ROW_TILE, B_DK // 2), lambda i: (i % per_seq, 0)),
            pl.BlockSpec((ROW_TILE, B_DK // 2), lambda i: (i % per_seq, 0)),
        ],
        out_specs=[
            pl.BlockSpec((ROW_TILE, dq), lambda i: (i, 0)),
            pl.BlockSpec((ROW_TILE, dq), lambda i: (i, 0)),
            pl.BlockSpec((ROW_TILE, dv), lambda i: (i, 0)),
            pl.BlockSpec((ROW_TILE, dv), lambda i: (i, 0)),
        ],
        out_shape=[
            jax.ShapeDtypeStruct((TOKENS, dq), BF16),
            jax.ShapeDtypeStruct((TOKENS, dq), BF16),
            jax.ShapeDtypeStruct((TOKENS, dv), BF16),
            jax.ShapeDtypeStruct((TOKENS, dv), BF16),
        ],
        compiler_params=_params("parallel"),
        name="ret_in_proj",
    )(x2, g, w, cos, sin)


RET_HEADS_PER_STEP = 2


def _retention_kernel(q_ref, k_ref, v_ref, g_ref, decay_ref, xi_ref, zeta_ref, gc_ref, og_ref,
                      y_ref, state_scr):
    state_scr[...] = jnp.zeros(state_scr.shape, F32)

    def body(n, carry):
        r0 = pl.multiple_of(n * RET_CHUNK, RET_CHUNK)
        for hh in range(RET_HEADS_PER_STEP):
            kcols = slice(hh * B_DK, (hh + 1) * B_DK)
            vcols = slice(hh * B_DV, (hh + 1) * B_DV)
            qc = q_ref[pl.ds(r0, RET_CHUNK), kcols]
            kc = k_ref[pl.ds(r0, RET_CHUNK), kcols]
            vc = v_ref[pl.ds(r0, RET_CHUNK), vcols]
            state = state_scr[hh]
            s = (_dot_nt(qc, kc) * decay_ref[hh]).astype(BF16)
            ret = _dot(s, vc) + _dot(qc, state.astype(BF16)) * xi_ref[hh]
            kz = (kc.astype(F32) * zeta_ref[hh]).astype(BF16)
            state_scr[hh] = state * gc_ref[hh] + _dot_tn(kz, vc)
            gate = g_ref[pl.ds(r0, RET_CHUNK), vcols].astype(F32)
            y = _rms(ret, og_ref[:, vcols]) * (gate * jax.nn.sigmoid(gate))
            y_ref[pl.ds(r0, RET_CHUNK), vcols] = y.astype(BF16)
        return carry
    lax.fori_loop(0, SEQ // RET_CHUNK, body, 0)


def _retention(q, k, v, g, decay, xi, zeta, gc, og):
    hs = RET_HEADS_PER_STEP
    return pl.pallas_call(
        _retention_kernel,
        grid=(BATCH, B_HEADS // hs),
        in_specs=[
            pl.BlockSpec((SEQ, hs * B_DK), lambda b, h: (b, h)),
            pl.BlockSpec((SEQ, hs * B_DK), lambda b, h: (b, h)),
            pl.BlockSpec((SEQ, hs * B_DV), lambda b, h: (b, h)),
            pl.BlockSpec((SEQ, hs * B_DV), lambda b, h: (b, h)),
            pl.BlockSpec((hs, RET_CHUNK, RET_CHUNK), lambda b, h: (h, 0, 0)),
            pl.BlockSpec((hs, RET_CHUNK, B_DV), lambda b, h: (h, 0, 0)),
            pl.BlockSpec((hs, RET_CHUNK, B_DK), lambda b, h: (h, 0, 0)),
            pl.BlockSpec((hs, 1, B_DV), lambda b, h: (h, 0, 0)),
            pl.BlockSpec((1, hs * B_DV), lambda b, h: (0, h)),
        ],
        out_specs=pl.BlockSpec((SEQ, hs * B_DV), lambda b, h: (b, h)),
        out_shape=jax.ShapeDtypeStruct((TOKENS, B_HEADS * B_DV), BF16),
        scratch_shapes=[pltpu.VMEM((hs, B_DK, B_DV), F32)],
        compiler_params=_params("parallel", "parallel"),
        name="retention",
    )(q, k, v, g, decay, xi, zeta, gc, og)


def _retention_tables():
    log_gamma = np.log1p(-(2.0 ** (-5.0 - np.arange(B_HEADS, dtype=np.float32)))).astype(np.float32)
    pos = np.arange(RET_CHUNK, dtype=np.float32)
    diff = pos[:, None] - pos[None, :]
    decay = np.where(diff[None] >= 0,
                     np.exp(np.maximum(diff, 0.0)[None] * log_gamma[:, None, None]), 0.0)
    xi = np.exp((pos + 1.0)[None, :] * log_gamma[:, None])
    zeta = np.exp((RET_CHUNK - 1.0 - pos)[None, :] * log_gamma[:, None])
    gc = np.exp(RET_CHUNK * log_gamma)
    xi_b = np.broadcast_to(xi[:, :, None], (B_HEADS, RET_CHUNK, B_DV))
    zeta_b = np.broadcast_to(zeta[:, :, None], (B_HEADS, RET_CHUNK, B_DK))
    gc_b = np.broadcast_to(gc[:, None, None], (B_HEADS, 1, B_DV))
    return tuple(jnp.asarray(a, F32) for a in (decay, xi_b, zeta_b, gc_b))


def _rotary_tables():
    inv = (1.0 / (ROPE_BASE ** (np.arange(0, B_DK, 2, dtype=np.float32) / B_DK))).astype(np.float32)
    ang = np.arange(SEQ, dtype=np.float32)[:, None] * inv[None, :]
    return jnp.asarray(np.cos(ang), F32), jnp.asarray(np.sin(ang), F32)


def _ffn_weights(w_up, conv_w, conv_b, w_down):
    return w_up.astype(BF16), conv_w, conv_b[None, :], w_down.astype(BF16)


def kernel(x, norm_mix_g, norm_ffn_g, a_w_in, a_q_g, a_k_g, a_iq_g, a_ik_g, a_w_out,
           b_w_in, b_out_g, b_w_out, f_w_up, f_conv_w, f_conv_b, f_w_down):
    x2 = x.reshape(TOKENS, D_MODEL)

    w = a_w_in[0]
    o1 = A_Q_COLS
    o2 = o1 + A_LAT
    o3 = o2 + IDX_HEADS * IDX_DIM
    o4 = o3 + IDX_DIM
    wck = jnp.concatenate(
        [w[:, o1:o2], w[:, o3:o4], jnp.zeros((D_MODEL, A_LAT - IDX_DIM), F32)], axis=1).astype(BF16)
    wt = jnp.concatenate([w[:, :o3], w[:, o4:]], axis=1).T.astype(BF16)
    qt, k, ct, qit, ki, wit = _proj_a(
        x2, norm_mix_g[0][None, :], wck, wt,
        a_q_g[0][:, None], a_k_g[0][None, :], a_iq_g[0][:, None], a_ik_g[0][None, :])
    ot = _dsa(qt, qit, wit, ki, k, ct)
    x2 = _ffn(x2, ot, True, a_w_out[0].astype(BF16), norm_ffn_g[0][None, :],
              *_ffn_weights(f_w_up[0], f_conv_w[0], f_conv_b[0], f_w_down[0]))

    cos, sin = _rotary_tables()
    q, kk, v, g = _proj_b(x2, norm_mix_g[1][None, :], b_w_in[0].astype(BF16), cos, sin)
    y = _retention(q, kk, v, g, *_retention_tables(), b_out_g[0][None, :])
    x2 = _ffn(x2, y, False, b_w_out[0].astype(BF16), norm_ffn_g[1][None, :],
              *_ffn_weights(f_w_up[1], f_conv_w[1], f_conv_b[1], f_w_down[1]))
    return x2.reshape(BATCH, SEQ, D_MODEL)
```

```python
import functools

import jax
import jax.numpy as jnp
import numpy as np
from jax import lax
from jax.experimental import pallas as pl
from jax.experimental.pallas import tpu as pltpu

D_MODEL = 1024
BATCH = 8
SEQ = 2048
TOKENS = BATCH * SEQ
CHUNK = 64
EPS = 1e-6
A_HEADS = 8
A_LAT = 128
IDX_HEADS = 8
IDX_DIM = 64
TOPK = 256
A_Q_COLS = A_HEADS * A_LAT
B_HEADS = 4
B_DK = 256
B_DV = 512
ROPE_BASE = 10000.0
RET_CHUNK = 256
D_FF = 2816
CONV_W = 3
FF_CHUNK = 256
N_FF_CHUNKS = D_FF // FF_CHUNK

LANES = 128
SUBLANES = 8
ROW_TILE = 512
Q_BLOCK = 128
Q_GROUPS = 2
Q_STEP = Q_GROUPS * Q_BLOCK
KEY_CHUNK = 256
COUNT_ROWS = 32
INDEX_BITS = 12
VMEM_LIMIT = 56 * 1024 * 1024
MASK_NEG = -(2.0 ** 100)
LOG2_E = 1.4426950408889634

BF16 = jnp.bfloat16
F32 = jnp.float32


def _params(*sem):
    return pltpu.CompilerParams(dimension_semantics=sem, vmem_limit_bytes=VMEM_LIMIT)


def _const_spec(shape):
    nd = len(shape)
    return pl.BlockSpec(shape, lambda *_: (0,) * nd, pipeline_mode=pl.Buffered(1))


def _rms(x, g):
    return x * lax.rsqrt(jnp.mean(x * x, axis=-1, keepdims=True) + EPS) * g


def _dot(a, b):
    return jnp.dot(a, b, preferred_element_type=F32)


def _dot_nt(a, b):
    return lax.dot_general(a, b, (((1,), (1,)), ((), ())), preferred_element_type=F32)


def _dot_tn(a, b):
    return lax.dot_general(a, b, (((0,), (0,)), ((), ())), preferred_element_type=F32)


T_Q = 0
T_C = T_Q + A_Q_COLS
T_QI = T_C + A_LAT
T_WI = T_QI + IDX_HEADS * IDX_DIM
T_ROWS = T_WI + IDX_HEADS


def _proj_a_kernel(x_ref, g_ref, wck_ref, wt_ref, qg_ref, kg_ref, iqg_ref, ikg_ref,
                   qt_out, k_out, ct_out, qit_out, ki_out, wit_out):
    hb = _rms(x_ref[...], g_ref[...]).astype(BF16)
    ck = _dot(hb, wck_ref[...])
    k_out[...] = _rms(ck[:, :A_LAT], kg_ref[...]).astype(BF16)
    ki_out[...] = _rms(ck[:, A_LAT:A_LAT + IDX_DIM], ikg_ref[...]).astype(BF16)

    t = _dot_nt(wt_ref[...], hb)

    def head_norm(rows, gain):
        return rows * lax.rsqrt(jnp.mean(rows * rows, axis=0, keepdims=True) + EPS) * gain

    qg = qg_ref[...] * (A_LAT ** -0.5 * LOG2_E)
    for h in range(A_HEADS):
        r0 = T_Q + h * A_LAT
        qt_out[h * A_LAT:(h + 1) * A_LAT, :] = head_norm(t[r0:r0 + A_LAT, :], qg).astype(BF16)
    ct_out[...] = t[T_C:T_C + A_LAT, :].astype(BF16)
    for h in range(IDX_HEADS):
        r0 = T_QI + h * IDX_DIM
        qit_out[h * IDX_DIM:(h + 1) * IDX_DIM, :] = head_norm(t[r0:r0 + IDX_DIM, :], iqg_ref[...]).astype(BF16)
    wit_out[...] = t[T_WI:, :] * (IDX_HEADS ** -0.5 * IDX_DIM ** -0.5)


def _proj_a(x2, g, wck, wt, qg, kg, iqg, ikg):
    n = TOKENS // ROW_TILE
    return pl.pallas_call(
        _proj_a_kernel,
        grid=(n,),
        in_specs=[
            pl.BlockSpec((ROW_TILE, D_MODEL), lambda i: (i, 0)),
            _const_spec((1, D_MODEL)),
            _const_spec((D_MODEL, 2 * A_LAT)),
            _const_spec((T_ROWS, D_MODEL)),
            _const_spec((A_LAT, 1)),
            _const_spec((1, A_LAT)),
            _const_spec((IDX_DIM, 1)),
            _const_spec((1, IDX_DIM)),
        ],
        out_specs=[
            pl.BlockSpec((A_Q_COLS, ROW_TILE), lambda i: (0, i)),
            pl.BlockSpec((ROW_TILE, A_LAT), lambda i: (i, 0)),
            pl.BlockSpec((A_LAT, ROW_TILE), lambda i: (0, i)),
            pl.BlockSpec((IDX_HEADS * IDX_DIM, ROW_TILE), lambda i: (0, i)),
            pl.BlockSpec((ROW_TILE, IDX_DIM), lambda i: (i, 0)),
            pl.BlockSpec((IDX_HEADS, ROW_TILE), lambda i: (0, i)),
        ],
        out_shape=[
            jax.ShapeDtypeStruct((A_Q_COLS, TOKENS), BF16),
            jax.ShapeDtypeStruct((TOKENS, A_LAT), BF16),
            jax.ShapeDtypeStruct((A_LAT, TOKENS), BF16),
            jax.ShapeDtypeStruct((IDX_HEADS * IDX_DIM, TOKENS), BF16),
            jax.ShapeDtypeStruct((TOKENS, IDX_DIM), BF16),
            jax.ShapeDtypeStruct((IDX_HEADS, TOKENS), F32),
        ],
        compiler_params=_params("parallel"),
        name="dsa_in_proj",
    )(x2, g, wck, wt, qg, kg, iqg, ikg)


def _ordered_bits_to_float(u):
    k = u ^ jnp.int32(-2 ** 31)
    bits = k ^ ((k >> 31) & jnp.int32(0x7FFFFFFF))
    return pltpu.bitcast(bits, F32)


def _dsa_kernel(qt_ref, qit_ref, wit_ref, ki_ref, k_ref, ct_ref, ot_ref,
                score_scr, thr_scr, cut_scr, s_scr, m_scr, l_scr, acc_scr):
    step = pl.program_id(1)
    n_chunks = step + 1
    cols = A_HEADS * Q_BLOCK
    qpos = step * Q_STEP + lax.broadcasted_iota(jnp.int32, (1, Q_STEP), 1)
    limit = ((qpos >> 6) + 1) << 6

    def block_lanes(g):
        return slice(g * Q_BLOCK, (g + 1) * Q_BLOCK)

    def block_cols(g):
        return slice(g * cols, (g + 1) * cols)

    def key_pos(r0, rows):
        return r0 + lax.broadcasted_iota(jnp.int32, (rows, Q_STEP), 0)

    def for_chunks(fn):
        def pair(jj, carry):
            r0 = pl.multiple_of(jj * (2 * KEY_CHUNK), 2 * KEY_CHUNK)
            fn(r0)
            fn(pl.multiple_of(r0 + KEY_CHUNK, KEY_CHUNK))
            return carry
        lax.fori_loop(0, n_chunks // 2, pair, 0)

        @pl.when(n_chunks % 2 == 1)
        def _():
            fn(pl.multiple_of((n_chunks - 1) * KEY_CHUNK, KEY_CHUNK))

    def count(n, pred_fn):
        acc = None
        for j in range(n):
            r0 = j * KEY_CHUNK
            hit = pred_fn(score_scr[r0:r0 + KEY_CHUNK, :], key_pos(r0, KEY_CHUNK))
            part = jnp.sum(hit.reshape(KEY_CHUNK // COUNT_ROWS, COUNT_ROWS, Q_STEP), axis=0)
            acc = part if acc is None else acc + part
        return jnp.sum(acc, axis=0, keepdims=True)

    thr_scr[...] = jnp.full(thr_scr.shape, -jnp.inf, F32)
    cut_scr[...] = jnp.full(cut_scr.shape, 2 * SEQ, jnp.int32)

    @pl.when(n_chunks * KEY_CHUNK <= TOPK)
    def _():
        score_scr[0:KEY_CHUNK, :] = jnp.zeros((KEY_CHUNK, Q_STEP), F32)

    @pl.when(n_chunks * KEY_CHUNK > TOPK)
    def _():
        def score_chunk(r0):
            kic = ki_ref[pl.ds(r0, KEY_CHUNK), :]
            acc = jnp.zeros((KEY_CHUNK, Q_STEP), F32)
            for h in range(IDX_HEADS):
                rel = _dot(kic, qit_ref[h * IDX_DIM:(h + 1) * IDX_DIM, :])
                acc = acc + jnp.maximum(rel, 0.0) * wit_ref[h:h + 1, :]
            score_scr[pl.ds(r0, KEY_CHUNK), :] = jnp.where(key_pos(r0, KEY_CHUNK) < limit, acc, -jnp.inf)
        for_chunks(score_chunk)

    def search(n):
        def bit_body(b, u):
            cand = u | (jnp.int32(1) << (31 - b))
            cf = _ordered_bits_to_float(cand)
            n_ge = count(n, lambda s, _: jnp.where(s >= cf, 1.0, 0.0))
            return jnp.where(n_ge >= float(TOPK), cand, u)
        u = lax.fori_loop(0, 32, bit_body, jnp.zeros((1, Q_STEP), jnp.int32))
        thr = _ordered_bits_to_float(u)
        thr_scr[...] = jnp.broadcast_to(thr, thr_scr.shape)

        n_ge = count(n, lambda s, _: jnp.where(s >= thr, 1.0, 0.0))
        any_tie = jnp.max(jnp.where(n_ge > float(TOPK), 1.0, 0.0))

        @pl.when(any_tie > 0.0)
        def _():
            n_gt = count(n, lambda s, _: jnp.where(s > thr, 1.0, 0.0))
            need = float(TOPK) - n_gt

            def cut_body(b, cut):
                cand = cut | (jnp.int32(1) << (INDEX_BITS - 1 - b))
                n_eq = count(n, lambda s, kp: jnp.where(s == thr, jnp.where(kp < cand, 1.0, 0.0), 0.0))
                return jnp.where(n_eq <= need, cand, cut)
            cut = lax.fori_loop(0, INDEX_BITS, cut_body, jnp.zeros((1, Q_STEP), jnp.int32))
            cut_scr[...] = jnp.broadcast_to(cut, cut_scr.shape)

    for n in range(TOPK // KEY_CHUNK + 1, SEQ // KEY_CHUNK + 1):
        pl.when(n_chunks == n)(functools.partial(search, n))

    thr = thr_scr[0:1, :]
    cut = cut_scr[0:1, :]
    eye = (lax.broadcasted_iota(jnp.int32, (Q_BLOCK, Q_BLOCK), 0)
           == lax.broadcasted_iota(jnp.int32, (Q_BLOCK, Q_BLOCK), 1))
    eye = jnp.concatenate([jnp.where(eye, 1.0, 0.0).astype(BF16)] * A_HEADS, axis=1)
    qt_aug = []
    for g in range(Q_GROUPS):
        qt = jnp.concatenate([qt_ref[h * A_LAT:(h + 1) * A_LAT, block_lanes(g)] for h in range(A_HEADS)], axis=1)
        qt_aug.append(jnp.concatenate([qt, eye], axis=0))

    def row_groups(a):
        return a.reshape(KEY_CHUNK // SUBLANES, SUBLANES, a.shape[-1])

    m_scr[...] = jnp.full(m_scr.shape, -jnp.inf, F32)

    def logits_chunk(r0):
        sc = score_scr[pl.ds(r0, KEY_CHUNK), :]
        kp = key_pos(r0, KEY_CHUNK)
        tie_bias = jnp.where(sc == thr, jnp.where(kp < cut, 0.0, MASK_NEG), MASK_NEG)
        bias = jnp.where(kp < limit, jnp.where(sc > thr, 0.0, tie_bias), MASK_NEG).astype(BF16)
        kc = k_ref[pl.ds(r0, KEY_CHUNK), :]
        for g in range(Q_GROUPS):
            st = _dot(jnp.concatenate([kc, bias[:, block_lanes(g)]], axis=1), qt_aug[g])
            s_scr[pl.ds(r0, KEY_CHUNK), block_cols(g)] = st
            m_scr[:, block_cols(g)] = jnp.maximum(m_scr[:, block_cols(g)], jnp.max(row_groups(st), axis=0))
    for_chunks(logits_chunk)
    m = jnp.max(m_scr[...], axis=0, keepdims=True)

    l_scr[...] = jnp.zeros(l_scr.shape, F32)
    acc_scr[...] = jnp.zeros(acc_scr.shape, F32)

    def pv_chunk(r0):
        p = jnp.exp2(s_scr[pl.ds(r0, KEY_CHUNK), :] - m)
        l_scr[...] += jnp.sum(row_groups(p), axis=0)
        acc_scr[...] += _dot(ct_ref[:, pl.ds(r0, KEY_CHUNK)], p.astype(BF16))
    for_chunks(pv_chunk)

    o = acc_scr[...] / jnp.sum(l_scr[...], axis=0, keepdims=True)
    for g in range(Q_GROUPS):
        for h in range(A_HEADS):
            c0 = g * cols + h * Q_BLOCK
            ot_ref[h * A_LAT:(h + 1) * A_LAT, block_lanes(g)] = o[:, c0:c0 + Q_BLOCK].astype(BF16)


def _dsa(qt, qit, wit, ki, k, ct):
    assert Q_STEP == KEY_CHUNK
    nq = SEQ // Q_STEP
    cols = Q_GROUPS * A_HEADS * Q_BLOCK
    return pl.pallas_call(
        _dsa_kernel,
        grid=(BATCH, nq),
        in_specs=[
            pl.BlockSpec((A_Q_COLS, Q_STEP), lambda b, i: (0, b * nq + i)),
            pl.BlockSpec((IDX_HEADS * IDX_DIM, Q_STEP), lambda b, i: (0, b * nq + i)),
            pl.BlockSpec((IDX_HEADS, Q_STEP), lambda b, i: (0, b * nq + i)),
            pl.BlockSpec((SEQ, IDX_DIM), lambda b, i: (b, 0)),
            pl.BlockSpec((SEQ, A_LAT), lambda b, i: (b, 0)),
            pl.BlockSpec((A_LAT, SEQ), lambda b, i: (0, b)),
        ],
        out_specs=pl.BlockSpec((A_Q_COLS, Q_STEP), lambda b, i: (0, b * nq + i)),
        out_shape=jax.ShapeDtypeStruct((A_Q_COLS, TOKENS), BF16),
        scratch_shapes=[
            pltpu.VMEM((SEQ, Q_STEP), F32),
            pltpu.VMEM((SUBLANES, Q_STEP), F32),
            pltpu.VMEM((SUBLANES, Q_STEP), jnp.int32),
            pltpu.VMEM((SEQ, cols), F32),
            pltpu.VMEM((SUBLANES, cols), F32),
            pltpu.VMEM((SUBLANES, cols), F32),
            pltpu.VMEM((A_LAT, cols), F32),
        ],
        compiler_params=_params("parallel", "arbitrary"),
        name="dsa_core",
    )(qt, qit, wit, ki, k, ct)


SEG = ROW_TILE // SUBLANES
HALO = (CONV_W - 1) * SUBLANES


def _interleave_rows(a):
    return jnp.transpose(a.reshape(SUBLANES, SEG, a.shape[-1]), (1, 0, 2)).reshape(a.shape)


def _deinterleave_rows(a):
    return jnp.transpose(a.reshape(SEG, SUBLANES, a.shape[-1]), (1, 0, 2)).reshape(a.shape)


def _ffn_kernel(m_transposed, x_ref, m_ref, wout_ref, g_ref, wup_ref, cw_ref, cb_ref, wdown_ref, o_ref,
                hb_scr, u_even, u_odd, gate_scr, carry_scr):
    t = pl.program_id(0)
    first_of_sequence = (t % (SEQ // ROW_TILE)) == 0
    mixer = (_dot_tn if m_transposed else _dot)(m_ref[...], wout_ref[...])
    x1 = x_ref[...] + mixer
    o_ref[...] = x1
    hb_scr[...] = _interleave_rows(_rms(x1, g_ref[...])).astype(BF16)
    first_sublane = lax.broadcasted_iota(jnp.int32, (HALO, 2 * FF_CHUNK), 0) % SUBLANES == 0

    def chunk_cols(ref, c):
        if isinstance(c, int):
            a0, b0 = c * FF_CHUNK, D_FF + c * FF_CHUNK
        else:
            a0 = pl.multiple_of(c * FF_CHUNK, FF_CHUNK)
            b0 = pl.multiple_of(D_FF + c * FF_CHUNK, FF_CHUNK)
        return ref[:, pl.ds(a0, FF_CHUNK)], ref[:, pl.ds(b0, FF_CHUNK)]

    def up(c):
        hb = hb_scr[...]
        wa, wb = chunk_cols(wup_ref, c)
        return jnp.concatenate([_dot(hb, wa), _dot(hb, wb)], axis=1)

    def down_one_sublane(a):
        return jnp.concatenate([pltpu.roll(a[r:r + SUBLANES, :], 1, 0)
                                for r in range(0, a.shape[0], SUBLANES)], axis=0)

    def act(c, u):
        tail = u[ROW_TILE - HALO:, :]
        prev = jnp.where(first_of_sequence, 0.0, carry_scr[c])
        carry_scr[c] = tail
        wrap = jnp.where(first_sublane, down_one_sublane(prev), down_one_sublane(tail))
        cw = jnp.concatenate(chunk_cols(cw_ref, c), axis=1)
        cb = jnp.concatenate(chunk_cols(cb_ref, c), axis=1)
        v = u * cw[CONV_W - 1:CONV_W, :] + cb
        for d in range(1, CONV_W):
            shifted = jnp.concatenate([wrap[HALO - d * SUBLANES:, :], u[:ROW_TILE - d * SUBLANES, :]], axis=0)
            v = v + shifted * cw[CONV_W - 1 - d:CONV_W - d, :]
        a = v[:, :FF_CHUNK]
        b = v[:, FF_CHUNK:]
        return (a * jax.nn.sigmoid(a) * b).astype(BF16)

    def stage(c, u_cur, u_next, has_up=True):
        if has_up:
            u_next[...] = up(c + 1)
        g0 = c * FF_CHUNK if isinstance(c, int) else pl.multiple_of(c * FF_CHUNK, FF_CHUNK)
        gate_scr[:, pl.ds(g0, FF_CHUNK)] = act(c, u_cur[...])

    def even_stage(c, **kw):
        stage(c, u_even, u_odd, **kw)

    def odd_stage(c, **kw):
        stage(c, u_odd, u_even, **kw)

    u_even[...] = up(0)
    even_stage(0)

    def pair(i, carry):
        c = 2 * i + 1
        odd_stage(c)
        even_stage(c + 1)
        return carry
    n_pairs = (N_FF_CHUNKS - 3) // 2
    lax.fori_loop(0, n_pairs, pair, 0)
    odd_stage(N_FF_CHUNKS - 2)
    even_stage(N_FF_CHUNKS - 1, has_up=False)
    o_ref[...] += _deinterleave_rows(_dot(gate_scr[...], wdown_ref[...]))


def _ffn(x2, m, m_transposed, wout, g, wup, cw, cb, wdown):
    assert N_FF_CHUNKS % 2 == 1 and N_FF_CHUNKS >= 3
    n = TOKENS // ROW_TILE
    km = wout.shape[0]
    m_spec = (pl.BlockSpec((km, ROW_TILE), lambda i: (0, i)) if m_transposed
              else pl.BlockSpec((ROW_TILE, km), lambda i: (i, 0)))
    return pl.pallas_call(
        functools.partial(_ffn_kernel, m_transposed),
        grid=(n,),
        in_specs=[
            pl.BlockSpec((ROW_TILE, D_MODEL), lambda i: (i, 0)),
            m_spec,
            _const_spec((km, D_MODEL)),
            _const_spec((1, D_MODEL)),
            _const_spec((D_MODEL, 2 * D_FF)),
            _const_spec((CONV_W, 2 * D_FF)),
            _const_spec((1, 2 * D_FF)),
            _const_spec((D_FF, D_MODEL)),
        ],
        out_specs=pl.BlockSpec((ROW_TILE, D_MODEL), lambda i: (i, 0)),
        out_shape=jax.ShapeDtypeStruct((TOKENS, D_MODEL), F32),
        scratch_shapes=[
            pltpu.VMEM((ROW_TILE, D_MODEL), BF16),
            pltpu.VMEM((ROW_TILE, 2 * FF_CHUNK), F32),
            pltpu.VMEM((ROW_TILE, 2 * FF_CHUNK), F32),
            pltpu.VMEM((ROW_TILE, D_FF), BF16),
            pltpu.VMEM((N_FF_CHUNKS, HALO, 2 * FF_CHUNK), F32),
        ],
        compiler_params=_params("arbitrary"),
        name="conv_ffn",
    )(x2, m, wout, g, wup, cw, cb, wdown)


def _proj_b_kernel(x_ref, g_ref, w_ref, cos_ref, sin_ref, q_out, k_out, v_out, g_out):
    hb = _rms(x_ref[...], g_ref[...]).astype(BF16)
    cos = cos_ref[...]
    sin = sin_ref[...]
    half = B_DK // 2

    def rotary(z):
        z1, z2 = z[:, :half], z[:, half:]
        return z1 * cos - z2 * sin, z1 * sin + z2 * cos

    dq = B_HEADS * B_DK
    for h in range(B_HEADS):
        zq = _dot(hb, w_ref[:, h * B_DK:(h + 1) * B_DK])
        r1, r2 = rotary(zq)
        q_out[:, h * B_DK:h * B_DK + half] = r1.astype(BF16)
        q_out[:, h * B_DK + half:(h + 1) * B_DK] = r2.astype(BF16)
        zk = _dot(hb, w_ref[:, dq + h * B_DK:dq + (h + 1) * B_DK])
        r1, r2 = rotary(zk)
        k_out[:, h * B_DK:h * B_DK + half] = (r1 * (B_DK ** -0.5)).astype(BF16)
        k_out[:, h * B_DK + half:(h + 1) * B_DK] = (r2 * (B_DK ** -0.5)).astype(BF16)
    dv = B_HEADS * B_DV
    for h in range(B_HEADS):
        v_out[:, h * B_DV:(h + 1) * B_DV] = _dot(
            hb, w_ref[:, 2 * dq + h * B_DV:2 * dq + (h + 1) * B_DV]).astype(BF16)
        g_out[:, h * B_DV:(h + 1) * B_DV] = _dot(
            hb, w_ref[:, 2 * dq + dv + h * B_DV:2 * dq + dv + (h + 1) * B_DV]).astype(BF16)


def _proj_b(x2, g, w, cos, sin):
    n = TOKENS // ROW_TILE
    per_seq = SEQ // ROW_TILE
    dq = B_HEADS * B_DK
    dv = B_HEADS * B_DV
    return pl.pallas_call(
        _proj_b_kernel,
        grid=(n,),
        in_specs=[
            pl.BlockSpec((ROW_TILE, D_MODEL), lambda i: (i, 0)),
            _const_spec((1, D_MODEL)),
            _const_spec((D_MODEL, 2 * dq + 2 * dv)),
            pl.BlockSpec((ROW_TILE, B_DK // 2), lambda i: (i % per_seq, 0)),
            pl.BlockSpec((ROW_TILE, B_DK // 2), lambda i: (i % per_seq, 0)),
        ],
        out_specs=[
            pl.BlockSpec((ROW_TILE, dq), lambda i: (i, 0)),
            pl.BlockSpec((ROW_TILE, dq), lambda i: (i, 0)),
            pl.BlockSpec((ROW_TILE, dv), lambda i: (i, 0)),
            pl.BlockSpec((ROW_TILE, dv), lambda i: (i, 0)),
        ],
        out_shape=[
            jax.ShapeDtypeStruct((TOKENS, dq), BF16),
            jax.ShapeDtypeStruct((TOKENS, dq), BF16),
            jax.ShapeDtypeStruct((TOKENS, dv), BF16),
            jax.ShapeDtypeStruct((TOKENS, dv), BF16),
        ],
        compiler_params=_params("parallel"),
        name="ret_in_proj",
    )(x2, g, w, cos, sin)


RET_HEADS_PER_STEP = 2


def _retention_kernel(q_ref, k_ref, v_ref, g_ref, decay_ref, xi_ref, zeta_ref, gc_ref, og_ref,
                      y_ref, state_scr):
    state_scr[...] = jnp.zeros(state_scr.shape, F32)

    def body(n, carry):
        r0 = pl.multiple_of(n * RET_CHUNK, RET_CHUNK)
        for hh in range(RET_HEADS_PER_STEP):
            kcols = slice(hh * B_DK, (hh + 1) * B_DK)
            vcols = slice(hh * B_DV, (hh + 1) * B_DV)
            qc = q_ref[pl.ds(r0, RET_CHUNK), kcols]
            kc = k_ref[pl.ds(r0, RET_CHUNK), kcols]
            vc = v_ref[pl.ds(r0, RET_CHUNK), vcols]
            state = state_scr[hh]
            s = (_dot_nt(qc, kc) * decay_ref[hh]).astype(BF16)
            ret = _dot(s, vc) + _dot(qc, state.astype(BF16)) * xi_ref[hh]
            kz = (kc.astype(F32) * zeta_ref[hh]).astype(BF16)
            state_scr[hh] = state * gc_ref[hh] + _dot_tn(kz, vc)
            gate = g_ref[pl.ds(r0, RET_CHUNK), vcols].astype(F32)
            y = _rms(ret, og_ref[:, vcols]) * (gate * jax.nn.sigmoid(gate))
            y_ref[pl.ds(r0, RET_CHUNK), vcols] = y.astype(BF16)
        return carry
    lax.fori_loop(0, SEQ // RET_CHUNK, body, 0)


def _retention(q, k, v, g, decay, xi, zeta, gc, og):
    hs = RET_HEADS_PER_STEP
    return pl.pallas_call(
        _retention_kernel,
        grid=(BATCH, B_HEADS // hs),
        in_specs=[
            pl.BlockSpec((SEQ, hs * B_DK), lambda b, h: (b, h)),
            pl.BlockSpec((SEQ, hs * B_DK), lambda b, h: (b, h)),
            pl.BlockSpec((SEQ, hs * B_DV), lambda b, h: (b, h)),
            pl.BlockSpec((SEQ, hs * B_DV), lambda b, h: (b, h)),
            pl.BlockSpec((hs, RET_CHUNK, RET_CHUNK), lambda b, h: (h, 0, 0)),
            pl.BlockSpec((hs, RET_CHUNK, B_DV), lambda b, h: (h, 0, 0)),
            pl.BlockSpec((hs, RET_CHUNK, B_DK), lambda b, h: (h, 0, 0)),
            pl.BlockSpec((hs, 1, B_DV), lambda b, h: (h, 0, 0)),
            pl.BlockSpec((1, hs * B_DV), lambda b, h: (0, h)),
        ],
        out_specs=pl.BlockSpec((SEQ, hs * B_DV), lambda b, h: (b, h)),
        out_shape=jax.ShapeDtypeStruct((TOKENS, B_HEADS * B_DV), BF16),
        scratch_shapes=[pltpu.VMEM((hs, B_DK, B_DV), F32)],
        compiler_params=_params("parallel", "parallel"),
        name="retention",
    )(q, k, v, g, decay, xi, zeta, gc, og)


def _retention_tables():
    log_gamma = np.log1p(-(2.0 ** (-5.0 - np.arange(B_HEADS, dtype=np.float32)))).astype(np.float32)
    pos = np.arange(RET_CHUNK, dtype=np.float32)
    diff = pos[:, None] - pos[None, :]
    decay = np.where(diff[None] >= 0,
                     np.exp(np.maximum(diff, 0.0)[None] * log_gamma[:, None, None]), 0.0)
    xi = np.exp((pos + 1.0)[None, :] * log_gamma[:, None])
    zeta = np.exp((RET_CHUNK - 1.0 - pos)[None, :] * log_gamma[:, None])
    gc = np.exp(RET_CHUNK * log_gamma)
    xi_b = np.broadcast_to(xi[:, :, None], (B_HEADS, RET_CHUNK, B_DV))
    zeta_b = np.broadcast_to(zeta[:, :, None], (B_HEADS, RET_CHUNK, B_DK))
    gc_b = np.broadcast_to(gc[:, None, None], (B_HEADS, 1, B_DV))
    return tuple(jnp.asarray(a, F32) for a in (decay, xi_b, zeta_b, gc_b))


def _rotary_tables():
    inv = (1.0 / (ROPE_BASE ** (np.arange(0, B_DK, 2, dtype=np.float32) / B_DK))).astype(np.float32)
    ang = np.arange(SEQ, dtype=np.float32)[:, None] * inv[None, :]
    return jnp.asarray(np.cos(ang), F32), jnp.asarray(np.sin(ang), F32)


def _ffn_weights(w_up, conv_w, conv_b, w_down):
    return w_up.astype(BF16), conv_w, conv_b[None, :], w_down.astype(BF16)


def kernel(x, norm_mix_g, norm_ffn_g, a_w_in, a_q_g, a_k_g, a_iq_g, a_ik_g, a_w_out,
           b_w_in, b_out_g, b_w_out, f_w_up, f_conv_w, f_conv_b, f_w_down):
    x2 = x.reshape(TOKENS, D_MODEL)

    w = a_w_in[0]
    o1 = A_Q_COLS
    o2 = o1 + A_LAT
    o3 = o2 + IDX_HEADS * IDX_DIM
    o4 = o3 + IDX_DIM
    wck = jnp.concatenate(
        [w[:, o1:o2], w[:, o3:o4], jnp.zeros((D_MODEL, A_LAT - IDX_DIM), F32)], axis=1).astype(BF16)
    wt = jnp.concatenate([w[:, :o3], w[:, o4:]], axis=1).T.astype(BF16)
    qt, k, ct, qit, ki, wit = _proj_a(
        x2, norm_mix_g[0][None, :], wck, wt,
        a_q_g[0][:, None], a_k_g[0][None, :], a_iq_g[0][:, None], a_ik_g[0][None, :])
    ot = _dsa(qt, qit, wit, ki, k, ct)
    x2 = _ffn(x2, ot, True, a_w_out[0].astype(BF16), norm_ffn_g[0][None, :],
              *_ffn_weights(f_w_up[0], f_conv_w[0], f_conv_b[0], f_w_down[0]))

    cos, sin = _rotary_tables()
    q, kk, v, g = _proj_b(x2, norm_mix_g[1][None, :], b_w_in[0].astype(BF16), cos, sin)
    y = _retention(q, kk, v, g, *_retention_tables(), b_out_g[0][None, :])
    x2 = _ffn(x2, y, False, b_w_out[0].astype(BF16), norm_ffn_g[1][None, :],
              *_ffn_weights(f_w_up[1], f_conv_w[1], f_conv_b[1], f_w_down[1]))
    return x2.reshape(BATCH, SEQ, D_MODEL)
```

```python
import functools

import jax
import jax.numpy as jnp
import numpy as np
from jax import lax
from jax.experimental import pallas as pl
from jax.experimental.pallas import tpu as pltpu

D_MODEL = 1024
BATCH = 8
SEQ = 2048
TOKENS = BATCH * SEQ
CHUNK = 64
EPS = 1e-6
A_HEADS = 8
A_LAT = 128
IDX_HEADS = 8
IDX_DIM = 64
TOPK = 256
A_Q_COLS = A_HEADS * A_LAT
B_HEADS = 4
B_DK = 256
B_DV = 512
ROPE_BASE = 10000.0
RET_CHUNK = 256
D_FF = 2816
CONV_W = 3
FF_CHUNK = 256
N_FF_CHUNKS = D_FF // FF_CHUNK

LANES = 128
SUBLANES = 8
ROW_TILE = 512
Q_BLOCK = 128
Q_GROUPS = 2
Q_STEP = Q_GROUPS * Q_BLOCK
KEY_CHUNK = 256
COUNT_ROWS = 32
INDEX_BITS = 12
VMEM_LIMIT = 56 * 1024 * 1024
MASK_NEG = -(2.0 ** 100)
LOG2_E = 1.4426950408889634

BF16 = jnp.bfloat16
F32 = jnp.float32


def _params(*sem):
    return pltpu.CompilerParams(dimension_semantics=sem, vmem_limit_bytes=VMEM_LIMIT)


def _const_spec(shape):
    nd = len(shape)
    return pl.BlockSpec(shape, lambda *_: (0,) * nd, pipeline_mode=pl.Buffered(1))


def _rms(x, g):
    return x * lax.rsqrt(jnp.mean(x * x, axis=-1, keepdims=True) + EPS) * g


def _dot(a, b):
    return jnp.dot(a, b, preferred_element_type=F32)


def _dot_nt(a, b):
    return lax.dot_general(a, b, (((1,), (1,)), ((), ())), preferred_element_type=F32)


def _dot_tn(a, b):
    return lax.dot_general(a, b, (((0,), (0,)), ((), ())), preferred_element_type=F32)


T_Q = 0
T_C = T_Q + A_Q_COLS
T_QI = T_C + A_LAT
T_WI = T_QI + IDX_HEADS * IDX_DIM
T_ROWS = T_WI + IDX_HEADS


def _proj_a_kernel(x_ref, g_ref, wck_ref, wt_ref, qg_ref, kg_ref, iqg_ref, ikg_ref,
                   qt_out, k_out, ct_out, qit_out, ki_out, wit_out):
    hb = _rms(x_ref[...], g_ref[...]).astype(BF16)
    ck = _dot(hb, wck_ref[...])
    k_out[...] = _rms(ck[:, :A_LAT], kg_ref[...]).astype(BF16)
    ki_out[...] = _rms(ck[:, A_LAT:A_LAT + IDX_DIM], ikg_ref[...]).astype(BF16)

    t = _dot_nt(wt_ref[...], hb)

    def head_norm(rows, gain):
        return rows * lax.rsqrt(jnp.mean(rows * rows, axis=0, keepdims=True) + EPS) * gain

    qg = qg_ref[...] * (A_LAT ** -0.5 * LOG2_E)
    for h in range(A_HEADS):
        r0 = T_Q + h * A_LAT
        qt_out[h * A_LAT:(h + 1) * A_LAT, :] = head_norm(t[r0:r0 + A_LAT, :], qg).astype(BF16)
    ct_out[...] = t[T_C:T_C + A_LAT, :].astype(BF16)
    for h in range(IDX_HEADS):
        r0 = T_QI + h * IDX_DIM
        qit_out[h * IDX_DIM:(h + 1) * IDX_DIM, :] = head_norm(t[r0:r0 + IDX_DIM, :], iqg_ref[...]).astype(BF16)
    wit_out[...] = t[T_WI:, :] * (IDX_HEADS ** -0.5 * IDX_DIM ** -0.5)


def _proj_a(x2, g, wck, wt, qg, kg, iqg, ikg):
    n = TOKENS // ROW_TILE
    return pl.pallas_call(
        _proj_a_kernel,
        grid=(n,),
        in_specs=[
            pl.BlockSpec((ROW_TILE, D_MODEL), lambda i: (i, 0)),
            _const_spec((1, D_MODEL)),
            _const_spec((D_MODEL, 2 * A_LAT)),
            _const_spec((T_ROWS, D_MODEL)),
            _const_spec((A_LAT, 1)),
            _const_spec((1, A_LAT)),
            _const_spec((IDX_DIM, 1)),
            _const_spec((1, IDX_DIM)),
        ],
        out_specs=[
            pl.BlockSpec((A_Q_COLS, ROW_TILE), lambda i: (0, i)),
            pl.BlockSpec((ROW_TILE, A_LAT), lambda i: (i, 0)),
            pl.BlockSpec((A_LAT, ROW_TILE), lambda i: (0, i)),
            pl.BlockSpec((IDX_HEADS * IDX_DIM, ROW_TILE), lambda i: (0, i)),
            pl.BlockSpec((ROW_TILE, IDX_DIM), lambda i: (i, 0)),
            pl.BlockSpec((IDX_HEADS, ROW_TILE), lambda i: (0, i)),
        ],
        out_shape=[
            jax.ShapeDtypeStruct((A_Q_COLS, TOKENS), BF16),
            jax.ShapeDtypeStruct((TOKENS, A_LAT), BF16),
            jax.ShapeDtypeStruct((A_LAT, TOKENS), BF16),
            jax.ShapeDtypeStruct((IDX_HEADS * IDX_DIM, TOKENS), BF16),
            jax.ShapeDtypeStruct((TOKENS, IDX_DIM), BF16),
            jax.ShapeDtypeStruct((IDX_HEADS, TOKENS), F32),
        ],
        compiler_params=_params("parallel"),
        name="dsa_in_proj",
    )(x2, g, wck, wt, qg, kg, iqg, ikg)


def _ordered_bits_to_float(u):
    k = u ^ jnp.int32(-2 ** 31)
    bits = k ^ ((k >> 31) & jnp.int32(0x7FFFFFFF))
    return pltpu.bitcast(bits, F32)


def _dsa_kernel(qt_ref, qit_ref, wit_ref, ki_ref, k_ref, ct_ref, ot_ref,
                score_scr, thr_scr, cut_scr, s_scr, m_scr, l_scr, acc_scr):
    step = pl.program_id(1)
    n_chunks = step + 1
    cols = A_HEADS * Q_BLOCK
    qpos = step * Q_STEP + lax.broadcasted_iota(jnp.int32, (1, Q_STEP), 1)
    limit = ((qpos >> 6) + 1) << 6

    def block_lanes(g):
        return slice(g * Q_BLOCK, (g + 1) * Q_BLOCK)

    def block_cols(g):
        return slice(g * cols, (g + 1) * cols)

    def key_pos(r0, rows):
        return r0 + lax.broadcasted_iota(jnp.int32, (rows, Q_STEP), 0)

    def for_chunks(fn):
        def pair(jj, carry):
            r0 = pl.multiple_of(jj * (2 * KEY_CHUNK), 2 * KEY_CHUNK)
            fn(r0)
            fn(pl.multiple_of(r0 + KEY_CHUNK, KEY_CHUNK))
            return carry
        lax.fori_loop(0, n_chunks // 2, pair, 0)

        @pl.when(n_chunks % 2 == 1)
        def _():
            fn(pl.multiple_of((n_chunks - 1) * KEY_CHUNK, KEY_CHUNK))

    def count(n, pred_fn):
        acc = None
        for j in range(n):
            r0 = j * KEY_CHUNK
            hit = pred_fn(score_scr[r0:r0 + KEY_CHUNK, :], key_pos(r0, KEY_CHUNK))
            part = jnp.sum(hit.reshape(KEY_CHUNK // COUNT_ROWS, COUNT_ROWS, Q_STEP), axis=0)
            acc = part if acc is None else acc + part
        return jnp.sum(acc, axis=0, keepdims=True)

    thr_scr[...] = jnp.full(thr_scr.shape, -jnp.inf, F32)
    cut_scr[...] = jnp.full(cut_scr.shape, 2 * SEQ, jnp.int32)

    @pl.when(n_chunks * KEY_CHUNK <= TOPK)
    def _():
        score_scr[0:KEY_CHUNK, :] = jnp.zeros((KEY_CHUNK, Q_STEP), F32)

    @pl.when(n_chunks * KEY_CHUNK > TOPK)
    def _():
        def score_chunk(r0):
            kic = ki_ref[pl.ds(r0, KEY_CHUNK), :]
            acc = jnp.zeros((KEY_CHUNK, Q_STEP), F32)
            for h in range(IDX_HEADS):
                rel = _dot(kic, qit_ref[h * IDX_DIM:(h + 1) * IDX_DIM, :])
                acc = acc + jnp.maximum(rel, 0.0) * wit_ref[h:h + 1, :]
            score_scr[pl.ds(r0, KEY_CHUNK), :] = jnp.where(key_pos(r0, KEY_CHUNK) < limit, acc, -jnp.inf)
        for_chunks(score_chunk)

    def search(n):
        def bit_body(b, u):
            cand = u | (jnp.int32(1) << (31 - b))
            cf = _ordered_bits_to_float(cand)
            n_ge = count(n, lambda s, _: jnp.where(s >= cf, 1.0, 0.0))
            return jnp.where(n_ge >= float(TOPK), cand, u)
        u = lax.fori_loop(0, 32, bit_body, jnp.zeros((1, Q_STEP), jnp.int32))
        thr = _ordered_bits_to_float(u)
        thr_scr[...] = jnp.broadcast_to(thr, thr_scr.shape)

        n_ge = count(n, lambda s, _: jnp.where(s >= thr, 1.0, 0.0))
        any_tie = jnp.max(jnp.where(n_ge > float(TOPK), 1.0, 0.0))

        @pl.when(any_tie > 0.0)
        def _():
            n_gt = count(n, lambda s, _: jnp.where(s > thr, 1.0, 0.0))
            need = float(TOPK) - n_gt

            def cut_body(b, cut):
                cand = cut | (jnp.int32(1) << (INDEX_BITS - 1 - b))
                n_eq = count(n, lambda s, kp: jnp.where(s == thr, jnp.where(kp < cand, 1.0, 0.0), 0.0))
                return jnp.where(n_eq <= need, cand, cut)
            cut = lax.fori_loop(0, INDEX_BITS, cut_body, jnp.zeros((1, Q_STEP), jnp.int32))
            cut_scr[...] = jnp.broadcast_to(cut, cut_scr.shape)

    for n in range(TOPK // KEY_CHUNK + 1, SEQ // KEY_CHUNK + 1):
        pl.when(n_chunks == n)(functools.partial(search, n))

    thr = thr_scr[0:1, :]
    cut = cut_scr[0:1, :]
    eye = (lax.broadcasted_iota(jnp.int32, (Q_BLOCK, Q_BLOCK), 0)
           == lax.broadcasted_iota(jnp.int32, (Q_BLOCK, Q_BLOCK), 1))
    eye = jnp.concatenate([jnp.where(eye, 1.0, 0.0).astype(BF16)] * A_HEADS, axis=1)
    qt_aug = []
    for g in range(Q_GROUPS):
        qt = jnp.concatenate([qt_ref[h * A_LAT:(h + 1) * A_LAT, block_lanes(g)] for h in range(A_HEADS)], axis=1)
        qt_aug.append(jnp.concatenate([qt, eye], axis=0))

    def row_groups(a):
        return a.reshape(KEY_CHUNK // SUBLANES, SUBLANES, a.shape[-1])

    m_scr[...] = jnp.full(m_scr.shape, -jnp.inf, F32)

    def logits_chunk(r0):
        sc = score_scr[pl.ds(r0, KEY_CHUNK), :]
        kp = key_pos(r0, KEY_CHUNK)
        tie_bias = jnp.where(sc == thr, jnp.where(kp < cut, 0.0, MASK_NEG), MASK_NEG)
        bias = jnp.where(kp < limit, jnp.where(sc > thr, 0.0, tie_bias), MASK_NEG).astype(BF16)
        kc = k_ref[pl.ds(r0, KEY_CHUNK), :]
        for g in range(Q_GROUPS):
            st = _dot(jnp.concatenate([kc, bias[:, block_lanes(g)]], axis=1), qt_aug[g])
            s_scr[pl.ds(r0, KEY_CHUNK), block_cols(g)] = st
            m_scr[:, block_cols(g)] = jnp.maximum(m_scr[:, block_cols(g)], jnp.max(row_groups(st), axis=0))
    for_chunks(logits_chunk)
    m = jnp.max(m_scr[...], axis=0, keepdims=True)

    l_scr[...] = jnp.zeros(l_scr.shape, F32)
    acc_scr[...] = jnp.zeros(acc_scr.shape, F32)

    def pv_chunk(r0):
        p = jnp.exp2(s_scr[pl.ds(r0, KEY_CHUNK), :] - m)
        l_scr[...] += jnp.sum(row_groups(p), axis=0)
        acc_scr[...] += _dot(ct_ref[:, pl.ds(r0, KEY_CHUNK)], p.astype(BF16))
    for_chunks(pv_chunk)

    o = acc_scr[...] / jnp.sum(l_scr[...], axis=0, keepdims=True)
    for g in range(Q_GROUPS):
        for h in range(A_HEADS):
            c0 = g * cols + h * Q_BLOCK
            ot_ref[h * A_LAT:(h + 1) * A_LAT, block_lanes(g)] = o[:, c0:c0 + Q_BLOCK].astype(BF16)


def _dsa(qt, qit, wit, ki, k, ct):
    assert Q_STEP == KEY_CHUNK
    nq = SEQ // Q_STEP
    cols = Q_GROUPS * A_HEADS * Q_BLOCK
    return pl.pallas_call(
        _dsa_kernel,
        grid=(BATCH, nq),
        in_specs=[
            pl.BlockSpec((A_Q_COLS, Q_STEP), lambda b, i: (0, b * nq + i)),
            pl.BlockSpec((IDX_HEADS * IDX_DIM, Q_STEP), lambda b, i: (0, b * nq + i)),
            pl.BlockSpec((IDX_HEADS, Q_STEP), lambda b, i: (0, b * nq + i)),
            pl.BlockSpec((SEQ, IDX_DIM), lambda b, i: (b, 0)),
            pl.BlockSpec((SEQ, A_LAT), lambda b, i: (b, 0)),
            pl.BlockSpec((A_LAT, SEQ), lambda b, i: (0, b)),
        ],
        out_specs=pl.BlockSpec((A_Q_COLS, Q_STEP), lambda b, i: (0, b * nq + i)),
        out_shape=jax.ShapeDtypeStruct((A_Q_COLS, TOKENS), BF16),
        scratch_shapes=[
            pltpu.VMEM((SEQ, Q_STEP), F32),
            pltpu.VMEM((SUBLANES, Q_STEP), F32),
            pltpu.VMEM((SUBLANES, Q_STEP), jnp.int32),
            pltpu.VMEM((SEQ, cols), F32),
            pltpu.VMEM((SUBLANES, cols), F32),
            pltpu.VMEM((SUBLANES, cols), F32),
            pltpu.VMEM((A_LAT, cols), F32),
        ],
        compiler_params=_params("parallel", "arbitrary"),
        name="dsa_core",
    )(qt, qit, wit, ki, k, ct)


SEG = ROW_TILE // SUBLANES
HALO = (CONV_W - 1) * SUBLANES


def _interleave_rows(a):
    return jnp.transpose(a.reshape(SUBLANES, SEG, a.shape[-1]), (1, 0, 2)).reshape(a.shape)


def _deinterleave_rows(a):
    return jnp.transpose(a.reshape(SEG, SUBLANES, a.shape[-1]), (1, 0, 2)).reshape(a.shape)


def _ffn_kernel(m_transposed, x_ref, m_ref, wout_ref, g_ref, wup_ref, cw_ref, cb_ref, wdown_ref, o_ref,
                hb_scr, u_even, u_odd, gate_scr, carry_scr):
    t = pl.program_id(0)
    first_of_sequence = (t % (SEQ // ROW_TILE)) == 0
    mixer = (_dot_tn if m_transposed else _dot)(m_ref[...], wout_ref[...])
    x1 = x_ref[...] + mixer
    o_ref[...] = x1
    hb_scr[...] = _interleave_rows(_rms(x1, g_ref[...])).astype(BF16)
    first_sublane = lax.broadcasted_iota(jnp.int32, (HALO, 2 * FF_CHUNK), 0) % SUBLANES == 0

    def chunk_cols(ref, c):
        if isinstance(c, int):
            a0, b0 = c * FF_CHUNK, D_FF + c * FF_CHUNK
        else:
            a0 = pl.multiple_of(c * FF_CHUNK, FF_CHUNK)
            b0 = pl.multiple_of(D_FF + c * FF_CHUNK, FF_CHUNK)
        return ref[:, pl.ds(a0, FF_CHUNK)], ref[:, pl.ds(b0, FF_CHUNK)]

    def up(c):
        hb = hb_scr[...]
        wa, wb = chunk_cols(wup_ref, c)
        return jnp.concatenate([_dot(hb, wa), _dot(hb, wb)], axis=1)

    def down_one_sublane(a):
        return jnp.concatenate([pltpu.roll(a[r:r + SUBLANES, :], 1, 0)
                                for r in range(0, a.shape[0], SUBLANES)], axis=0)

    def act(c, u):
        tail = u[ROW_TILE - HALO:, :]
        prev = jnp.where(first_of_sequence, 0.0, carry_scr[c])
        carry_scr[c] = tail
        wrap = jnp.where(first_sublane, down_one_sublane(prev), down_one_sublane(tail))
        cw = jnp.concatenate(chunk_cols(cw_ref, c), axis=1)
        cb = jnp.concatenate(chunk_cols(cb_ref, c), axis=1)
        v = u * cw[CONV_W - 1:CONV_W, :] + cb
        for d in range(1, CONV_W):
            shifted = jnp.concatenate([wrap[HALO - d * SUBLANES:, :], u[:ROW_TILE - d * SUBLANES, :]], axis=0)
            v = v + shifted * cw[CONV_W - 1 - d:CONV_W - d, :]
        a = v[:, :FF_CHUNK]
        b = v[:, FF_CHUNK:]
        return (a * jax.nn.sigmoid(a) * b).astype(BF16)

    def stage(c, u_cur, u_next, has_up=True):
        if has_up:
            u_next[...] = up(c + 1)
        g0 = c * FF_CHUNK if isinstance(c, int) else pl.multiple_of(c * FF_CHUNK, FF_CHUNK)
        gate_scr[:, pl.ds(g0, FF_CHUNK)] = act(c, u_cur[...])

    def even_stage(c, **kw):
        stage(c, u_even, u_odd, **kw)

    def odd_stage(c, **kw):
        stage(c, u_odd, u_even, **kw)

    u_even[...] = up(0)
    for c in range(N_FF_CHUNKS):
        (even_stage if c % 2 == 0 else odd_stage)(c, has_up=c + 1 < N_FF_CHUNKS)
    o_ref[...] += _deinterleave_rows(_dot(gate_scr[...], wdown_ref[...]))


def _ffn(x2, m, m_transposed, wout, g, wup, cw, cb, wdown):
    assert N_FF_CHUNKS % 2 == 1 and N_FF_CHUNKS >= 3
    n = TOKENS // ROW_TILE
    km = wout.shape[0]
    m_spec = (pl.BlockSpec((km, ROW_TILE), lambda i: (0, i)) if m_transposed
              else pl.BlockSpec((ROW_TILE, km), lambda i: (i, 0)))
    return pl.pallas_call(
        functools.partial(_ffn_kernel, m_transposed),
        grid=(n,),
        in_specs=[
            pl.BlockSpec((ROW_TILE, D_MODEL), lambda i: (i, 0)),
            m_spec,
            _const_spec((km, D_MODEL)),
            _const_spec((1, D_MODEL)),
            _const_spec((D_MODEL, 2 * D_FF)),
            _const_spec((CONV_W, 2 * D_FF)),
            _const_spec((1, 2 * D_FF)),
            _const_spec((D_FF, D_MODEL)),
        ],
        out_specs=pl.BlockSpec((ROW_TILE, D_MODEL), lambda i: (i, 0)),
        out_shape=jax.ShapeDtypeStruct((TOKENS, D_MODEL), F32),
        scratch_shapes=[
            pltpu.VMEM((ROW_TILE, D_MODEL), BF16),
            pltpu.VMEM((ROW_TILE, 2 * FF_CHUNK), F32),
            pltpu.VMEM((ROW_TILE, 2 * FF_CHUNK), F32),
            pltpu.VMEM((ROW_TILE, D_FF), BF16),
            pltpu.VMEM((N_FF_CHUNKS, HALO, 2 * FF_CHUNK), F32),
        ],
        compiler_params=_params("arbitrary"),
        name="conv_ffn",
    )(x2, m, wout, g, wup, cw, cb, wdown)


def _proj_b_kernel(x_ref, g_ref, w_ref, cos_ref, sin_ref, zeta_ref, q_out, k_out, kz_out, v_out, sg_out):
    hb = _rms(x_ref[...], g_ref[...]).astype(BF16)
    cos = cos_ref[...]
    sin = sin_ref[...]
    half = B_DK // 2

    def rotary(z):
        z1, z2 = z[:, :half], z[:, half:]
        return z1 * cos - z2 * sin, z1 * sin + z2 * cos

    dq = B_HEADS * B_DK
    for h in range(B_HEADS):
        zq = _dot(hb, w_ref[:, h * B_DK:(h + 1) * B_DK])
        for part, r in enumerate(rotary(zq)):
            cols = slice(h * B_DK + part * half, h * B_DK + (part + 1) * half)
            q_out[:, cols] = r.astype(BF16)
        zk = _dot(hb, w_ref[:, dq + h * B_DK:dq + (h + 1) * B_DK])
        for part, r in enumerate(rotary(zk)):
            cols = slice(h * B_DK + part * half, h * B_DK + (part + 1) * half)
            kr = r * (B_DK ** -0.5)
            k_out[:, cols] = kr.astype(BF16)
            kz_out[:, cols] = (kr * zeta_ref[:, cols]).astype(BF16)
    dv = B_HEADS * B_DV
    for h in range(B_HEADS):
        v_out[:, h * B_DV:(h + 1) * B_DV] = _dot(
            hb, w_ref[:, 2 * dq + h * B_DV:2 * dq + (h + 1) * B_DV]).astype(BF16)
        gate = _dot(hb, w_ref[:, 2 * dq + dv + h * B_DV:2 * dq + dv + (h + 1) * B_DV])
        sg_out[:, h * B_DV:(h + 1) * B_DV] = (gate * jax.nn.sigmoid(gate)).astype(BF16)


def _proj_b(x2, g, w, cos, sin, zeta_rows):
    n = TOKENS // ROW_TILE
    per_seq = SEQ // ROW_TILE
    dq = B_HEADS * B_DK
    dv = B_HEADS * B_DV
    return pl.pallas_call(
        _proj_b_kernel,
        grid=(n,),
        in_specs=[
            pl.BlockSpec((ROW_TILE, D_MODEL), lambda i: (i, 0)),
            _const_spec((1, D_MODEL)),
            _const_spec((D_MODEL, 2 * dq + 2 * dv)),
            pl.BlockSpec((ROW_TILE, B_DK // 2), lambda i: (i % per_seq, 0)),
            pl.BlockSpec((ROW_TILE, B_DK // 2), lambda i: (i % per_seq, 0)),
            _const_spec((ROW_TILE, dq)),
        ],
        out_specs=[
            pl.BlockSpec((ROW_TILE, dq), lambda i: (i, 0)),
            pl.BlockSpec((ROW_TILE, dq), lambda i: (i, 0)),
            pl.BlockSpec((ROW_TILE, dq), lambda i: (i, 0)),
            pl.BlockSpec((ROW_TILE, dv), lambda i: (i, 0)),
            pl.BlockSpec((ROW_TILE, dv), lambda i: (i, 0)),
        ],
        out_shape=[
            jax.ShapeDtypeStruct((TOKENS, dq), BF16),
            jax.ShapeDtypeStruct((TOKENS, dq), BF16),
            jax.ShapeDtypeStruct((TOKENS, dq), BF16),
            jax.ShapeDtypeStruct((TOKENS, dv), BF16),
            jax.ShapeDtypeStruct((TOKENS, dv), BF16),
        ],
        compiler_params=_params("parallel"),
        name="ret_in_proj",
    )(x2, g, w, cos, sin, zeta_rows)


RET_HEADS_PER_STEP = 2


def _retention_kernel(q_ref, k_ref, kz_ref, v_ref, sg_ref, decay_ref, xi_ref, gc_ref, og_ref,
                      y_ref, state_scr):
    state_scr[...] = jnp.zeros(state_scr.shape, F32)

    for n in range(SEQ // RET_CHUNK):
        r0 = n * RET_CHUNK
        for hh in range(RET_HEADS_PER_STEP):
            kcols = slice(hh * B_DK, (hh + 1) * B_DK)
            vcols = slice(hh * B_DV, (hh + 1) * B_DV)
            qc = q_ref[pl.ds(r0, RET_CHUNK), kcols]
            vc = v_ref[pl.ds(r0, RET_CHUNK), vcols]
            state = state_scr[hh]
            s = (_dot_nt(qc, k_ref[pl.ds(r0, RET_CHUNK), kcols]) * decay_ref[hh]).astype(BF16)
            ret = _dot(s, vc) + _dot(qc, state.astype(BF16)) * xi_ref[hh]
            state_scr[hh] = state * gc_ref[hh] + _dot_tn(kz_ref[pl.ds(r0, RET_CHUNK), kcols], vc)
            y = _rms(ret, og_ref[:, vcols]) * sg_ref[pl.ds(r0, RET_CHUNK), vcols].astype(F32)
            y_ref[pl.ds(r0, RET_CHUNK), vcols] = y.astype(BF16)


def _retention(q, k, kz, v, sg, decay, xi, gc, og):
    hs = RET_HEADS_PER_STEP
    return pl.pallas_call(
        _retention_kernel,
        grid=(BATCH, B_HEADS // hs),
        in_specs=[
            pl.BlockSpec((SEQ, hs * B_DK), lambda b, h: (b, h)),
            pl.BlockSpec((SEQ, hs * B_DK), lambda b, h: (b, h)),
            pl.BlockSpec((SEQ, hs * B_DK), lambda b, h: (b, h)),
            pl.BlockSpec((SEQ, hs * B_DV), lambda b, h: (b, h)),
            pl.BlockSpec((SEQ, hs * B_DV), lambda b, h: (b, h)),
            pl.BlockSpec((hs, RET_CHUNK, RET_CHUNK), lambda b, h: (h, 0, 0)),
            pl.BlockSpec((hs, RET_CHUNK, B_DV), lambda b, h: (h, 0, 0)),
            pl.BlockSpec((hs, 1, B_DV), lambda b, h: (h, 0, 0)),
            pl.BlockSpec((1, hs * B_DV), lambda b, h: (0, h)),
        ],
        out_specs=pl.BlockSpec((SEQ, hs * B_DV), lambda b, h: (b, h)),
        out_shape=jax.ShapeDtypeStruct((TOKENS, B_HEADS * B_DV), BF16),
        scratch_shapes=[pltpu.VMEM((hs, B_DK, B_DV), F32)],
        compiler_params=_params("parallel", "parallel"),
        name="retention",
    )(q, k, kz, v, sg, decay, xi, gc, og)


def _retention_tables():
    log_gamma = np.log1p(-(2.0 ** (-5.0 - np.arange(B_HEADS, dtype=np.float32)))).astype(np.float32)
    pos = np.arange(RET_CHUNK, dtype=np.float32)
    diff = pos[:, None] - pos[None, :]
    decay = np.where(diff[None] >= 0,
                     np.exp(np.maximum(diff, 0.0)[None] * log_gamma[:, None, None]), 0.0)
    xi = np.exp((pos + 1.0)[None, :] * log_gamma[:, None])
    zeta = np.exp((RET_CHUNK - 1.0 - pos)[None, :] * log_gamma[:, None])
    gc = np.exp(RET_CHUNK * log_gamma)
    xi_b = np.broadcast_to(xi[:, :, None], (B_HEADS, RET_CHUNK, B_DV))
    gc_b = np.broadcast_to(gc[:, None, None], (B_HEADS, 1, B_DV))
    zeta_rows = np.tile(np.repeat(zeta.T, B_DK, axis=1), (ROW_TILE // RET_CHUNK, 1))
    return tuple(jnp.asarray(a, F32) for a in (decay, xi_b, gc_b, zeta_rows))


def _rotary_tables():
    inv = (1.0 / (ROPE_BASE ** (np.arange(0, B_DK, 2, dtype=np.float32) / B_DK))).astype(np.float32)
    ang = np.arange(SEQ, dtype=np.float32)[:, None] * inv[None, :]
    return jnp.asarray(np.cos(ang), F32), jnp.asarray(np.sin(ang), F32)


def _ffn_weights(w_up, conv_w, conv_b, w_down):
    return w_up.astype(BF16), conv_w, conv_b[None, :], w_down.astype(BF16)


def kernel(x, norm_mix_g, norm_ffn_g, a_w_in, a_q_g, a_k_g, a_iq_g, a_ik_g, a_w_out,
           b_w_in, b_out_g, b_w_out, f_w_up, f_conv_w, f_conv_b, f_w_down):
    x2 = x.reshape(TOKENS, D_MODEL)

    w = a_w_in[0]
    o1 = A_Q_COLS
    o2 = o1 + A_LAT
    o3 = o2 + IDX_HEADS * IDX_DIM
    o4 = o3 + IDX_DIM
    wck = jnp.concatenate(
        [w[:, o1:o2], w[:, o3:o4], jnp.zeros((D_MODEL, A_LAT - IDX_DIM), F32)], axis=1).astype(BF16)
    wt = jnp.concatenate([w[:, :o3], w[:, o4:]], axis=1).T.astype(BF16)
    qt, k, ct, qit, ki, wit = _proj_a(
        x2, norm_mix_g[0][None, :], wck, wt,
        a_q_g[0][:, None], a_k_g[0][None, :], a_iq_g[0][:, None], a_ik_g[0][None, :])
    ot = _dsa(qt, qit, wit, ki, k, ct)
    x2 = _ffn(x2, ot, True, a_w_out[0].astype(BF16), norm_ffn_g[0][None, :],
              *_ffn_weights(f_w_up[0], f_conv_w[0], f_conv_b[0], f_w_down[0]))

    cos, sin = _rotary_tables()
    decay, xi, gc, zeta_rows = _retention_tables()
    q, kk, kz, v, sg = _proj_b(x2, norm_mix_g[1][None, :], b_w_in[0].astype(BF16), cos, sin, zeta_rows)
    y = _retention(q, kk, kz, v, sg, decay, xi, gc, b_out_g[0][None, :])
    x2 = _ffn(x2, y, False, b_w_out[0].astype(BF16), norm_ffn_g[1][None, :],
              *_ffn_weights(f_w_up[1], f_conv_w[1], f_conv_b[1], f_w_down[1]))
    return x2.reshape(BATCH, SEQ, D_MODEL)
```

```python
import functools

import jax
import jax.numpy as jnp
import numpy as np
from jax import lax
from jax.experimental import pallas as pl
from jax.experimental.pallas import tpu as pltpu

D_MODEL = 1024
BATCH = 8
SEQ = 2048
TOKENS = BATCH * SEQ
CHUNK = 64
EPS = 1e-6
A_HEADS = 8
A_LAT = 128
IDX_HEADS = 8
IDX_DIM = 64
TOPK = 256
A_Q_COLS = A_HEADS * A_LAT
B_HEADS = 4
B_DK = 256
B_DV = 512
ROPE_BASE = 10000.0
RET_CHUNK = 256
D_FF = 2816
CONV_W = 3
FF_CHUNK = 256
N_FF_CHUNKS = D_FF // FF_CHUNK

LANES = 128
SUBLANES = 8
ROW_TILE = 512
Q_BLOCK = 128
Q_GROUPS = 2
Q_STEP = Q_GROUPS * Q_BLOCK
KEY_CHUNK = 256
COUNT_ROWS = 32
INDEX_BITS = 12
COARSE_BITS = 16
FINE_BITS = 17
VMEM_LIMIT = 56 * 1024 * 1024
MASK_NEG = -(2.0 ** 100)
LOG2_E = 1.4426950408889634

BF16 = jnp.bfloat16
F32 = jnp.float32


def _params(*sem):
    return pltpu.CompilerParams(dimension_semantics=sem, vmem_limit_bytes=VMEM_LIMIT)


def _const_spec(shape):
    nd = len(shape)
    return pl.BlockSpec(shape, lambda *_: (0,) * nd, pipeline_mode=pl.Buffered(1))


def _rms(x, g):
    return x * lax.rsqrt(jnp.mean(x * x, axis=-1, keepdims=True) + EPS) * g


def _dot(a, b):
    return jnp.dot(a, b, preferred_element_type=F32)


def _dot_nt(a, b):
    return lax.dot_general(a, b, (((1,), (1,)), ((), ())), preferred_element_type=F32)


def _dot_tn(a, b):
    return lax.dot_general(a, b, (((0,), (0,)), ((), ())), preferred_element_type=F32)


T_Q = 0
T_C = T_Q + A_Q_COLS
T_QI = T_C + A_LAT
T_WI = T_QI + IDX_HEADS * IDX_DIM
T_ROWS = T_WI + IDX_HEADS


def _proj_a_kernel(x_ref, g_ref, wck_ref, wt_ref, qg_ref, kg_ref, iqg_ref, ikg_ref,
                   qt_out, k_out, ct_out, qit_out, ki_out, wit_out):
    hb = _rms(x_ref[...], g_ref[...]).astype(BF16)
    ck = _dot(hb, wck_ref[...])
    k_out[...] = _rms(ck[:, :A_LAT], kg_ref[...]).astype(BF16)
    ki_out[...] = _rms(ck[:, A_LAT:A_LAT + IDX_DIM], ikg_ref[...]).astype(BF16)

    t = _dot_nt(wt_ref[...], hb)

    def head_norm(rows, gain):
        return rows * lax.rsqrt(jnp.mean(rows * rows, axis=0, keepdims=True) + EPS) * gain

    qg = qg_ref[...] * (A_LAT ** -0.5 * LOG2_E)
    for h in range(A_HEADS):
        r0 = T_Q + h * A_LAT
        qt_out[h * A_LAT:(h + 1) * A_LAT, :] = head_norm(t[r0:r0 + A_LAT, :], qg).astype(BF16)
    ct_out[...] = t[T_C:T_C + A_LAT, :].astype(BF16)
    for h in range(IDX_HEADS):
        r0 = T_QI + h * IDX_DIM
        qit_out[h * IDX_DIM:(h + 1) * IDX_DIM, :] = head_norm(t[r0:r0 + IDX_DIM, :], iqg_ref[...]).astype(BF16)
    wit_out[...] = t[T_WI:, :] * (IDX_HEADS ** -0.5 * IDX_DIM ** -0.5)


def _proj_a(x2, g, wck, wt, qg, kg, iqg, ikg):
    n = TOKENS // ROW_TILE
    return pl.pallas_call(
        _proj_a_kernel,
        grid=(n,),
        in_specs=[
            pl.BlockSpec((ROW_TILE, D_MODEL), lambda i: (i, 0)),
            _const_spec((1, D_MODEL)),
            _const_spec((D_MODEL, 2 * A_LAT)),
            _const_spec((T_ROWS, D_MODEL)),
            _const_spec((A_LAT, 1)),
            _const_spec((1, A_LAT)),
            _const_spec((IDX_DIM, 1)),
            _const_spec((1, IDX_DIM)),
        ],
        out_specs=[
            pl.BlockSpec((A_Q_COLS, ROW_TILE), lambda i: (0, i)),
            pl.BlockSpec((ROW_TILE, A_LAT), lambda i: (i, 0)),
            pl.BlockSpec((A_LAT, ROW_TILE), lambda i: (0, i)),
            pl.BlockSpec((IDX_HEADS * IDX_DIM, ROW_TILE), lambda i: (0, i)),
            pl.BlockSpec((ROW_TILE, IDX_DIM), lambda i: (i, 0)),
            pl.BlockSpec((IDX_HEADS, ROW_TILE), lambda i: (0, i)),
        ],
        out_shape=[
            jax.ShapeDtypeStruct((A_Q_COLS, TOKENS), BF16),
            jax.ShapeDtypeStruct((TOKENS, A_LAT), BF16),
            jax.ShapeDtypeStruct((A_LAT, TOKENS), BF16),
            jax.ShapeDtypeStruct((IDX_HEADS * IDX_DIM, TOKENS), BF16),
            jax.ShapeDtypeStruct((TOKENS, IDX_DIM), BF16),
            jax.ShapeDtypeStruct((IDX_HEADS, TOKENS), F32),
        ],
        compiler_params=_params("parallel"),
        name="dsa_in_proj",
    )(x2, g, wck, wt, qg, kg, iqg, ikg)


def _ordered_bits_to_float(u):
    k = u ^ jnp.int32(-2 ** 31)
    bits = k ^ ((k >> 31) & jnp.int32(0x7FFFFFFF))
    return pltpu.bitcast(bits, F32)


def _dsa_kernel(qt_ref, qit_ref, wit_ref, ki_ref, k_ref, ct_ref, ot_ref,
                score_scr, score16_scr, thr_scr, cut_scr, s_scr, m_scr, l_scr, acc_scr):
    step = pl.program_id(1)
    n_chunks = step + 1
    cols = A_HEADS * Q_BLOCK
    qpos = step * Q_STEP + lax.broadcasted_iota(jnp.int32, (1, Q_STEP), 1)
    limit = ((qpos >> 6) + 1) << 6

    def block_lanes(g):
        return slice(g * Q_BLOCK, (g + 1) * Q_BLOCK)

    def block_cols(g):
        return slice(g * cols, (g + 1) * cols)

    def key_pos(r0, rows):
        return r0 + lax.broadcasted_iota(jnp.int32, (rows, Q_STEP), 0)

    def for_chunks(fn):
        def pair(jj, carry):
            r0 = pl.multiple_of(jj * (2 * KEY_CHUNK), 2 * KEY_CHUNK)
            fn(r0)
            fn(pl.multiple_of(r0 + KEY_CHUNK, KEY_CHUNK))
            return carry
        lax.fori_loop(0, n_chunks // 2, pair, 0)

        @pl.when(n_chunks % 2 == 1)
        def _():
            fn(pl.multiple_of((n_chunks - 1) * KEY_CHUNK, KEY_CHUNK))

    def count(n, pred_fn):
        acc = None
        for j in range(n):
            r0 = j * KEY_CHUNK
            hit = pred_fn(score_scr[r0:r0 + KEY_CHUNK, :], key_pos(r0, KEY_CHUNK))
            part = jnp.sum(hit.reshape(KEY_CHUNK // COUNT_ROWS, COUNT_ROWS, Q_STEP), axis=0)
            acc = part if acc is None else acc + part
        return jnp.sum(acc, axis=0, keepdims=True)

    thr_scr[...] = jnp.full(thr_scr.shape, -jnp.inf, F32)
    cut_scr[...] = jnp.full(cut_scr.shape, 2 * SEQ, jnp.int32)

    @pl.when(n_chunks * KEY_CHUNK <= TOPK)
    def _():
        score_scr[0:KEY_CHUNK, :] = jnp.zeros((KEY_CHUNK, Q_STEP), F32)

    @pl.when(n_chunks * KEY_CHUNK > TOPK)
    def _():
        def score_chunk(r0):
            kic = ki_ref[pl.ds(r0, KEY_CHUNK), :]
            acc = jnp.zeros((KEY_CHUNK, Q_STEP), F32)
            for h in range(IDX_HEADS):
                rel = _dot(kic, qit_ref[h * IDX_DIM:(h + 1) * IDX_DIM, :])
                acc = acc + jnp.maximum(rel, 0.0) * wit_ref[h:h + 1, :]
            masked = jnp.where(key_pos(r0, KEY_CHUNK) < limit, acc, -jnp.inf)
            score_scr[pl.ds(r0, KEY_CHUNK), :] = masked
            score16_scr[pl.ds(r0, KEY_CHUNK), :] = masked.astype(BF16)
        for_chunks(score_chunk)

    def count_coarse(n, cand):
        one = jnp.ones((), BF16)
        zero = jnp.zeros((), BF16)
        accs = [None, None]
        for j in range(n):
            r0 = j * KEY_CHUNK
            hit = jnp.where(score16_scr[r0:r0 + KEY_CHUNK, :] >= cand, one, zero)
            for i, r in enumerate(range(0, KEY_CHUNK, COUNT_ROWS)):
                part = hit[r:r + COUNT_ROWS, :]
                accs[i % 2] = part if accs[i % 2] is None else accs[i % 2] + part
        return jnp.sum(accs[0].astype(F32) + accs[1].astype(F32), axis=0, keepdims=True)

    def search(n):
        def coarse_body(b, u):
            cand = u | (jnp.int32(1) << (31 - b))
            n_ge = count_coarse(n, _ordered_bits_to_float(cand).astype(BF16))
            return jnp.where(n_ge >= float(TOPK), cand, u)
        u16 = lax.fori_loop(0, COARSE_BITS, coarse_body, jnp.zeros((1, Q_STEP), jnp.int32))

        base = u16 - jnp.int32(1 << (31 - COARSE_BITS))

        def fine_body(b, d):
            cand = d | (jnp.int32(1) << (FINE_BITS - 1 - b))
            cf = _ordered_bits_to_float(base + cand)
            n_ge = count(n, lambda s, _: jnp.where(s >= cf, 1.0, 0.0))
            return jnp.where(n_ge >= float(TOPK), cand, d)
        d = lax.fori_loop(0, FINE_BITS, fine_body, jnp.zeros((1, Q_STEP), jnp.int32))
        thr = _ordered_bits_to_float(base + d)
        thr_scr[...] = jnp.broadcast_to(thr, thr_scr.shape)

        n_ge = count(n, lambda s, _: jnp.where(s >= thr, 1.0, 0.0))
        any_tie = jnp.max(jnp.where(n_ge > float(TOPK), 1.0, 0.0))

        @pl.when(any_tie > 0.0)
        def _():
            n_gt = count(n, lambda s, _: jnp.where(s > thr, 1.0, 0.0))
            need = float(TOPK) - n_gt

            def cut_body(b, cut):
                cand = cut | (jnp.int32(1) << (INDEX_BITS - 1 - b))
                n_eq = count(n, lambda s, kp: jnp.where(s == thr, jnp.where(kp < cand, 1.0, 0.0), 0.0))
                return jnp.where(n_eq <= need, cand, cut)
            cut = lax.fori_loop(0, INDEX_BITS, cut_body, jnp.zeros((1, Q_STEP), jnp.int32))
            cut_scr[...] = jnp.broadcast_to(cut, cut_scr.shape)

    for n in range(TOPK // KEY_CHUNK + 1, SEQ // KEY_CHUNK + 1):
        pl.when(n_chunks == n)(functools.partial(search, n))

    thr = thr_scr[0:1, :]
    cut = cut_scr[0:1, :]
    eye = (lax.broadcasted_iota(jnp.int32, (Q_BLOCK, Q_BLOCK), 0)
           == lax.broadcasted_iota(jnp.int32, (Q_BLOCK, Q_BLOCK), 1))
    eye = jnp.concatenate([jnp.where(eye, 1.0, 0.0).astype(BF16)] * A_HEADS, axis=1)
    qt_aug = []
    for g in range(Q_GROUPS):
        qt = jnp.concatenate([qt_ref[h * A_LAT:(h + 1) * A_LAT, block_lanes(g)] for h in range(A_HEADS)], axis=1)
        qt_aug.append(jnp.concatenate([qt, eye], axis=0))

    def row_groups(a):
        return a.reshape(KEY_CHUNK // SUBLANES, SUBLANES, a.shape[-1])

    m_scr[...] = jnp.full(m_scr.shape, -jnp.inf, F32)

    def logits_chunk(r0):
        sc = score_scr[pl.ds(r0, KEY_CHUNK), :]
        kp = key_pos(r0, KEY_CHUNK)
        tie_bias = jnp.where(sc == thr, jnp.where(kp < cut, 0.0, MASK_NEG), MASK_NEG)
        bias = jnp.where(kp < limit, jnp.where(sc > thr, 0.0, tie_bias), MASK_NEG).astype(BF16)
        kc = k_ref[pl.ds(r0, KEY_CHUNK), :]
        for g in range(Q_GROUPS):
            st = _dot(jnp.concatenate([kc, bias[:, block_lanes(g)]], axis=1), qt_aug[g])
            s_scr[pl.ds(r0, KEY_CHUNK), block_cols(g)] = st
            m_scr[:, block_cols(g)] = jnp.maximum(m_scr[:, block_cols(g)], jnp.max(row_groups(st), axis=0))
    for_chunks(logits_chunk)
    m = jnp.max(m_scr[...], axis=0, keepdims=True)

    l_scr[...] = jnp.zeros(l_scr.shape, F32)
    acc_scr[...] = jnp.zeros(acc_scr.shape, F32)

    def pv_chunk(r0):
        p = jnp.exp2(s_scr[pl.ds(r0, KEY_CHUNK), :] - m)
        l_scr[...] += jnp.sum(row_groups(p), axis=0)
        acc_scr[...] += _dot(ct_ref[:, pl.ds(r0, KEY_CHUNK)], p.astype(BF16))
    for_chunks(pv_chunk)

    o = acc_scr[...] / jnp.sum(l_scr[...], axis=0, keepdims=True)
    for g in range(Q_GROUPS):
        for h in range(A_HEADS):
            c0 = g * cols + h * Q_BLOCK
            ot_ref[h * A_LAT:(h + 1) * A_LAT, block_lanes(g)] = o[:, c0:c0 + Q_BLOCK].astype(BF16)


def _dsa(qt, qit, wit, ki, k, ct):
    assert Q_STEP == KEY_CHUNK
    nq = SEQ // Q_STEP
    cols = Q_GROUPS * A_HEADS * Q_BLOCK
    return pl.pallas_call(
        _dsa_kernel,
        grid=(BATCH, nq),
        in_specs=[
            pl.BlockSpec((A_Q_COLS, Q_STEP), lambda b, i: (0, b * nq + i)),
            pl.BlockSpec((IDX_HEADS * IDX_DIM, Q_STEP), lambda b, i: (0, b * nq + i)),
            pl.BlockSpec((IDX_HEADS, Q_STEP), lambda b, i: (0, b * nq + i)),
            pl.BlockSpec((SEQ, IDX_DIM), lambda b, i: (b, 0)),
            pl.BlockSpec((SEQ, A_LAT), lambda b, i: (b, 0)),
            pl.BlockSpec((A_LAT, SEQ), lambda b, i: (0, b)),
        ],
        out_specs=pl.BlockSpec((A_Q_COLS, Q_STEP), lambda b, i: (0, b * nq + i)),
        out_shape=jax.ShapeDtypeStruct((A_Q_COLS, TOKENS), BF16),
        scratch_shapes=[
            pltpu.VMEM((SEQ, Q_STEP), F32),
            pltpu.VMEM((SEQ, Q_STEP), BF16),
            pltpu.VMEM((SUBLANES, Q_STEP), F32),
            pltpu.VMEM((SUBLANES, Q_STEP), jnp.int32),
            pltpu.VMEM((SEQ, cols), F32),
            pltpu.VMEM((SUBLANES, cols), F32),
            pltpu.VMEM((SUBLANES, cols), F32),
            pltpu.VMEM((A_LAT, cols), F32),
        ],
        compiler_params=_params("parallel", "arbitrary"),
        name="dsa_core",
    )(qt, qit, wit, ki, k, ct)


SEG = ROW_TILE // SUBLANES
HALO = (CONV_W - 1) * SUBLANES


def _interleave_rows(a):
    return jnp.transpose(a.reshape(SUBLANES, SEG, a.shape[-1]), (1, 0, 2)).reshape(a.shape)


def _deinterleave_rows(a):
    return jnp.transpose(a.reshape(SEG, SUBLANES, a.shape[-1]), (1, 0, 2)).reshape(a.shape)


def _ffn_kernel(m_transposed, x_ref, m_ref, wout_ref, g_ref, wup_ref, cw_ref, cb_ref, wdown_ref, o_ref,
                hb_scr, u_even, u_odd, gate_scr, carry_scr):
    t = pl.program_id(0)
    first_of_sequence = (t % (SEQ // ROW_TILE)) == 0
    mixer = (_dot_tn if m_transposed else _dot)(m_ref[...], wout_ref[...])
    x1 = x_ref[...] + mixer
    o_ref[...] = x1
    hb_scr[...] = _interleave_rows(_rms(x1, g_ref[...])).astype(BF16)
    first_sublane = lax.broadcasted_iota(jnp.int32, (HALO, 2 * FF_CHUNK), 0) % SUBLANES == 0

    def chunk_cols(ref, c):
        if isinstance(c, int):
            a0, b0 = c * FF_CHUNK, D_FF + c * FF_CHUNK
        else:
            a0 = pl.multiple_of(c * FF_CHUNK, FF_CHUNK)
            b0 = pl.multiple_of(D_FF + c * FF_CHUNK, FF_CHUNK)
        return ref[:, pl.ds(a0, FF_CHUNK)], ref[:, pl.ds(b0, FF_CHUNK)]

    def up(c):
        hb = hb_scr[...]
        wa, wb = chunk_cols(wup_ref, c)
        return jnp.concatenate([_dot(hb, wa), _dot(hb, wb)], axis=1)

    def down_one_sublane(a):
        return jnp.concatenate([pltpu.roll(a[r:r + SUBLANES, :], 1, 0)
                                for r in range(0, a.shape[0], SUBLANES)], axis=0)

    def act(c, u):
        tail = u[ROW_TILE - HALO:, :]
        prev = jnp.where(first_of_sequence, 0.0, carry_scr[c])
        carry_scr[c] = tail
        wrap = jnp.where(first_sublane, down_one_sublane(prev), down_one_sublane(tail))
        cw = jnp.concatenate(chunk_cols(cw_ref, c), axis=1)
        cb = jnp.concatenate(chunk_cols(cb_ref, c), axis=1)
        v = u * cw[CONV_W - 1:CONV_W, :] + cb
        for d in range(1, CONV_W):
            shifted = jnp.concatenate([wrap[HALO - d * SUBLANES:, :], u[:ROW_TILE - d * SUBLANES, :]], axis=0)
            v = v + shifted * cw[CONV_W - 1 - d:CONV_W - d, :]
        a = v[:, :FF_CHUNK]
        b = v[:, FF_CHUNK:]
        return (a * jax.nn.sigmoid(a) * b).astype(BF16)

    def stage(c, u_cur, u_next, has_up=True):
        if has_up:
            u_next[...] = up(c + 1)
        g0 = c * FF_CHUNK if isinstance(c, int) else pl.multiple_of(c * FF_CHUNK, FF_CHUNK)
        gate_scr[:, pl.ds(g0, FF_CHUNK)] = act(c, u_cur[...])

    def even_stage(c, **kw):
        stage(c, u_even, u_odd, **kw)

    def odd_stage(c, **kw):
        stage(c, u_odd, u_even, **kw)

    u_even[...] = up(0)
    for c in range(N_FF_CHUNKS):
        (even_stage if c % 2 == 0 else odd_stage)(c, has_up=c + 1 < N_FF_CHUNKS)
    o_ref[...] += _deinterleave_rows(_dot(gate_scr[...], wdown_ref[...]))


def _ffn(x2, m, m_transposed, wout, g, wup, cw, cb, wdown):
    assert N_FF_CHUNKS % 2 == 1 and N_FF_CHUNKS >= 3
    n = TOKENS // ROW_TILE
    km = wout.shape[0]
    m_spec = (pl.BlockSpec((km, ROW_TILE), lambda i: (0, i)) if m_transposed
              else pl.BlockSpec((ROW_TILE, km), lambda i: (i, 0)))
    return pl.pallas_call(
        functools.partial(_ffn_kernel, m_transposed),
        grid=(n,),
        in_specs=[
            pl.BlockSpec((ROW_TILE, D_MODEL), lambda i: (i, 0)),
            m_spec,
            _const_spec((km, D_MODEL)),
            _const_spec((1, D_MODEL)),
            _const_spec((D_MODEL, 2 * D_FF)),
            _const_spec((CONV_W, 2 * D_FF)),
            _const_spec((1, 2 * D_FF)),
            _const_spec((D_FF, D_MODEL)),
        ],
        out_specs=pl.BlockSpec((ROW_TILE, D_MODEL), lambda i: (i, 0)),
        out_shape=jax.ShapeDtypeStruct((TOKENS, D_MODEL), F32),
        scratch_shapes=[
            pltpu.VMEM((ROW_TILE, D_MODEL), BF16),
            pltpu.VMEM((ROW_TILE, 2 * FF_CHUNK), F32),
            pltpu.VMEM((ROW_TILE, 2 * FF_CHUNK), F32),
            pltpu.VMEM((ROW_TILE, D_FF), BF16),
            pltpu.VMEM((N_FF_CHUNKS, HALO, 2 * FF_CHUNK), F32),
        ],
        compiler_params=_params("arbitrary"),
        name="conv_ffn",
    )(x2, m, wout, g, wup, cw, cb, wdown)


def _proj_b_kernel(x_ref, g_ref, w_ref, cos_ref, sin_ref, zeta_ref, q_out, k_out, kz_out, v_out, sg_out):
    hb = _rms(x_ref[...], g_ref[...]).astype(BF16)
    cos = cos_ref[...]
    sin = sin_ref[...]
    half = B_DK // 2

    def rotary(z):
        z1, z2 = z[:, :half], z[:, half:]
        return z1 * cos - z2 * sin, z1 * sin + z2 * cos

    dq = B_HEADS * B_DK
    for h in range(B_HEADS):
        zq = _dot(hb, w_ref[:, h * B_DK:(h + 1) * B_DK])
        for part, r in enumerate(rotary(zq)):
            cols = slice(h * B_DK + part * half, h * B_DK + (part + 1) * half)
            q_out[:, cols] = r.astype(BF16)
        zk = _dot(hb, w_ref[:, dq + h * B_DK:dq + (h + 1) * B_DK])
        for part, r in enumerate(rotary(zk)):
            cols = slice(h * B_DK + part * half, h * B_DK + (part + 1) * half)
            kr = r * (B_DK ** -0.5)
            k_out[:, cols] = kr.astype(BF16)
            kz_out[:, cols] = (kr * zeta_ref[:, cols]).astype(BF16)
    dv = B_HEADS * B_DV
    for h in range(B_HEADS):
        v_out[:, h * B_DV:(h + 1) * B_DV] = _dot(
            hb, w_ref[:, 2 * dq + h * B_DV:2 * dq + (h + 1) * B_DV]).astype(BF16)
        gate = _dot(hb, w_ref[:, 2 * dq + dv + h * B_DV:2 * dq + dv + (h + 1) * B_DV])
        sg_out[:, h * B_DV:(h + 1) * B_DV] = (gate * jax.nn.sigmoid(gate)).astype(BF16)


def _proj_b(x2, g, w, cos, sin, zeta_rows):
    n = TOKENS // ROW_TILE
    per_seq = SEQ // ROW_TILE
    dq = B_HEADS * B_DK
    dv = B_HEADS * B_DV
    return pl.pallas_call(
        _proj_b_kernel,
        grid=(n,),
        in_specs=[
            pl.BlockSpec((ROW_TILE, D_MODEL), lambda i: (i, 0)),
            _const_spec((1, D_MODEL)),
            _const_spec((D_MODEL, 2 * dq + 2 * dv)),
            pl.BlockSpec((ROW_TILE, B_DK // 2), lambda i: (i % per_seq, 0)),
            pl.BlockSpec((ROW_TILE, B_DK // 2), lambda i: (i % per_seq, 0)),
            _const_spec((ROW_TILE, dq)),
        ],
        out_specs=[
            pl.BlockSpec((ROW_TILE, dq), lambda i: (i, 0)),
            pl.BlockSpec((ROW_TILE, dq), lambda i: (i, 0)),
            pl.BlockSpec((ROW_TILE, dq), lambda i: (i, 0)),
            pl.BlockSpec((ROW_TILE, dv), lambda i: (i, 0)),
            pl.BlockSpec((ROW_TILE, dv), lambda i: (i, 0)),
        ],
        out_shape=[
            jax.ShapeDtypeStruct((TOKENS, dq), BF16),
            jax.ShapeDtypeStruct((TOKENS, dq), BF16),
            jax.ShapeDtypeStruct((TOKENS, dq), BF16),
            jax.ShapeDtypeStruct((TOKENS, dv), BF16),
            jax.ShapeDtypeStruct((TOKENS, dv), BF16),
        ],
        compiler_params=_params("parallel"),
        name="ret_in_proj",
    )(x2, g, w, cos, sin, zeta_rows)


RET_HEADS_PER_STEP = 2


def _retention_kernel(q_ref, k_ref, kz_ref, v_ref, sg_ref, decay_ref, xi_ref, gc_ref, og_ref,
                      y_ref, state_scr):
    state_scr[...] = jnp.zeros(state_scr.shape, F32)

    for n in range(SEQ // RET_CHUNK):
        r0 = n * RET_CHUNK
        for hh in range(RET_HEADS_PER_STEP):
            kcols = slice(hh * B_DK, (hh + 1) * B_DK)
            vcols = slice(hh * B_DV, (hh + 1) * B_DV)
            qc = q_ref[pl.ds(r0, RET_CHUNK), kcols]
            vc = v_ref[pl.ds(r0, RET_CHUNK), vcols]
            state = state_scr[hh]
            s = (_dot_nt(qc, k_ref[pl.ds(r0, RET_CHUNK), kcols]) * decay_ref[hh]).astype(BF16)
            ret = _dot(s, vc) + _dot(qc, state.astype(BF16)) * xi_ref[hh]
            state_scr[hh] = state * gc_ref[hh] + _dot_tn(kz_ref[pl.ds(r0, RET_CHUNK), kcols], vc)
            y = _rms(ret, og_ref[:, vcols]) * sg_ref[pl.ds(r0, RET_CHUNK), vcols].astype(F32)
            y_ref[pl.ds(r0, RET_CHUNK), vcols] = y.astype(BF16)


def _retention(q, k, kz, v, sg, decay, xi, gc, og):
    hs = RET_HEADS_PER_STEP
    return pl.pallas_call(
        _retention_kernel,
        grid=(BATCH, B_HEADS // hs),
        in_specs=[
            pl.BlockSpec((SEQ, hs * B_DK), lambda b, h: (b, h)),
            pl.BlockSpec((SEQ, hs * B_DK), lambda b, h: (b, h)),
            pl.BlockSpec((SEQ, hs * B_DK), lambda b, h: (b, h)),
            pl.BlockSpec((SEQ, hs * B_DV), lambda b, h: (b, h)),
            pl.BlockSpec((SEQ, hs * B_DV), lambda b, h: (b, h)),
            pl.BlockSpec((hs, RET_CHUNK, RET_CHUNK), lambda b, h: (h, 0, 0)),
            pl.BlockSpec((hs, RET_CHUNK, B_DV), lambda b, h: (h, 0, 0)),
            pl.BlockSpec((hs, 1, B_DV), lambda b, h: (h, 0, 0)),
            pl.BlockSpec((1, hs * B_DV), lambda b, h: (0, h)),
        ],
        out_specs=pl.BlockSpec((SEQ, hs * B_DV), lambda b, h: (b, h)),
        out_shape=jax.ShapeDtypeStruct((TOKENS, B_HEADS * B_DV), BF16),
        scratch_shapes=[pltpu.VMEM((hs, B_DK, B_DV), F32)],
        compiler_params=_params("parallel", "parallel"),
        name="retention",
    )(q, k, kz, v, sg, decay, xi, gc, og)


def _retention_tables():
    log_gamma = np.log1p(-(2.0 ** (-5.0 - np.arange(B_HEADS, dtype=np.float32)))).astype(np.float32)
    pos = np.arange(RET_CHUNK, dtype=np.float32)
    diff = pos[:, None] - pos[None, :]
    decay = np.where(diff[None] >= 0,
                     np.exp(np.maximum(diff, 0.0)[None] * log_gamma[:, None, None]), 0.0)
    xi = np.exp((pos + 1.0)[None, :] * log_gamma[:, None])
    zeta = np.exp((RET_CHUNK - 1.0 - pos)[None, :] * log_gamma[:, None])
    gc = np.exp(RET_CHUNK * log_gamma)
    xi_b = np.broadcast_to(xi[:, :, None], (B_HEADS, RET_CHUNK, B_DV))
    gc_b = np.broadcast_to(gc[:, None, None], (B_HEADS, 1, B_DV))
    zeta_rows = np.tile(np.repeat(zeta.T, B_DK, axis=1), (ROW_TILE // RET_CHUNK, 1))
    return tuple(jnp.asarray(a, F32) for a in (decay, xi_b, gc_b, zeta_rows))


def _rotary_tables():
    inv = (1.0 / (ROPE_BASE ** (np.arange(0, B_DK, 2, dtype=np.float32) / B_DK))).astype(np.float32)
    ang = np.arange(SEQ, dtype=np.float32)[:, None] * inv[None, :]
    return jnp.asarray(np.cos(ang), F32), jnp.asarray(np.sin(ang), F32)


def _ffn_weights(w_up, conv_w, conv_b, w_down):
    return w_up.astype(BF16), conv_w, conv_b[None, :], w_down.astype(BF16)


def kernel(x, norm_mix_g, norm_ffn_g, a_w_in, a_q_g, a_k_g, a_iq_g, a_ik_g, a_w_out,
           b_w_in, b_out_g, b_w_out, f_w_up, f_conv_w, f_conv_b, f_w_down):
    x2 = x.reshape(TOKENS, D_MODEL)

    w = a_w_in[0]
    o1 = A_Q_COLS
    o2 = o1 + A_LAT
    o3 = o2 + IDX_HEADS * IDX_DIM
    o4 = o3 + IDX_DIM
    wck = jnp.concatenate(
        [w[:, o1:o2], w[:, o3:o4], jnp.zeros((D_MODEL, A_LAT - IDX_DIM), F32)], axis=1).astype(BF16)
    wt = jnp.concatenate([w[:, :o3], w[:, o4:]], axis=1).T.astype(BF16)
    qt, k, ct, qit, ki, wit = _proj_a(
        x2, norm_mix_g[0][None, :], wck, wt,
        a_q_g[0][:, None], a_k_g[0][None, :], a_iq_g[0][:, None], a_ik_g[0][None, :])
    ot = _dsa(qt, qit, wit, ki, k, ct)
    x2 = _ffn(x2, ot, True, a_w_out[0].astype(BF16), norm_ffn_g[0][None, :],
              *_ffn_weights(f_w_up[0], f_conv_w[0], f_conv_b[0], f_w_down[0]))

    cos, sin = _rotary_tables()
    decay, xi, gc, zeta_rows = _retention_tables()
    q, kk, kz, v, sg = _proj_b(x2, norm_mix_g[1][None, :], b_w_in[0].astype(BF16), cos, sin, zeta_rows)
    y = _retention(q, kk, kz, v, sg, decay, xi, gc, b_out_g[0][None, :])
    x2 = _ffn(x2, y, False, b_w_out[0].astype(BF16), norm_ffn_g[1][None, :],
              *_ffn_weights(f_w_up[1], f_conv_w[1], f_conv_b[1], f_w_down[1]))
    return x2.reshape(BATCH, SEQ, D_MODEL)
```

```python
import functools

import jax
import jax.numpy as jnp
import numpy as np
from jax import lax
from jax.experimental import pallas as pl
from jax.experimental.pallas import tpu as pltpu

D_MODEL = 1024
BATCH = 8
SEQ = 2048
TOKENS = BATCH * SEQ
CHUNK = 64
EPS = 1e-6
A_HEADS = 8
A_LAT = 128
IDX_HEADS = 8
IDX_DIM = 64
TOPK = 256
A_Q_COLS = A_HEADS * A_LAT
B_HEADS = 4
B_DK = 256
B_DV = 512
ROPE_BASE = 10000.0
RET_CHUNK = 256
D_FF = 2816
CONV_W = 3
FF_CHUNK = 256
N_FF_CHUNKS = D_FF // FF_CHUNK

LANES = 128
SUBLANES = 8
ROW_TILE = 512
Q_BLOCK = 128
Q_GROUPS = 2
Q_STEP = Q_GROUPS * Q_BLOCK
KEY_CHUNK = 256
COUNT_ROWS = 32
INDEX_BITS = 12
COARSE_BITS = 16
FINE_BITS = 17
VMEM_LIMIT = 56 * 1024 * 1024
MASK_NEG = -(2.0 ** 100)
LOG2_E = 1.4426950408889634

BF16 = jnp.bfloat16
F32 = jnp.float32


def _params(*sem):
    return pltpu.CompilerParams(dimension_semantics=sem, vmem_limit_bytes=VMEM_LIMIT)


def _const_spec(shape):
    nd = len(shape)
    return pl.BlockSpec(shape, lambda *_: (0,) * nd, pipeline_mode=pl.Buffered(1))


def _rms(x, g):
    return x * lax.rsqrt(jnp.mean(x * x, axis=-1, keepdims=True) + EPS) * g


def _dot(a, b):
    return jnp.dot(a, b, preferred_element_type=F32)


def _dot_nt(a, b):
    return lax.dot_general(a, b, (((1,), (1,)), ((), ())), preferred_element_type=F32)


def _dot_tn(a, b):
    return lax.dot_general(a, b, (((0,), (0,)), ((), ())), preferred_element_type=F32)


T_Q = 0
T_C = T_Q + A_Q_COLS
T_QI = T_C + A_LAT
T_WI = T_QI + IDX_HEADS * IDX_DIM
T_ROWS = T_WI + IDX_HEADS


def _proj_a_kernel(x_ref, g_ref, wck_ref, wt_ref, qg_ref, kg_ref, iqg_ref, ikg_ref,
                   qt_out, k_out, ct_out, qit_out, ki_out, wit_out):
    hb = _rms(x_ref[...], g_ref[...]).astype(BF16)
    ck = _dot(hb, wck_ref[...])
    k_out[...] = _rms(ck[:, :A_LAT], kg_ref[...]).astype(BF16)
    ki_out[...] = _rms(ck[:, A_LAT:A_LAT + IDX_DIM], ikg_ref[...]).astype(BF16)

    t = _dot_nt(wt_ref[...], hb)

    def head_norm(rows, gain):
        return rows * lax.rsqrt(jnp.mean(rows * rows, axis=0, keepdims=True) + EPS) * gain

    qg = qg_ref[...] * (A_LAT ** -0.5 * LOG2_E)
    for h in range(A_HEADS):
        r0 = T_Q + h * A_LAT
        qt_out[h * A_LAT:(h + 1) * A_LAT, :] = head_norm(t[r0:r0 + A_LAT, :], qg).astype(BF16)
    ct_out[...] = t[T_C:T_C + A_LAT, :].astype(BF16)
    for h in range(IDX_HEADS):
        r0 = T_QI + h * IDX_DIM
        qit_out[h * IDX_DIM:(h + 1) * IDX_DIM, :] = head_norm(t[r0:r0 + IDX_DIM, :], iqg_ref[...]).astype(BF16)
    wit_out[...] = t[T_WI:, :] * (IDX_HEADS ** -0.5 * IDX_DIM ** -0.5)


def _proj_a(x2, g, wck, wt, qg, kg, iqg, ikg):
    n = TOKENS // ROW_TILE
    return pl.pallas_call(
        _proj_a_kernel,
        grid=(n,),
        in_specs=[
            pl.BlockSpec((ROW_TILE, D_MODEL), lambda i: (i, 0)),
            _const_spec((1, D_MODEL)),
            _const_spec((D_MODEL, 2 * A_LAT)),
            _const_spec((T_ROWS, D_MODEL)),
            _const_spec((A_LAT, 1)),
            _const_spec((1, A_LAT)),
            _const_spec((IDX_DIM, 1)),
            _const_spec((1, IDX_DIM)),
        ],
        out_specs=[
            pl.BlockSpec((A_Q_COLS, ROW_TILE), lambda i: (0, i)),
            pl.BlockSpec((ROW_TILE, A_LAT), lambda i: (i, 0)),
            pl.BlockSpec((A_LAT, ROW_TILE), lambda i: (0, i)),
            pl.BlockSpec((IDX_HEADS * IDX_DIM, ROW_TILE), lambda i: (0, i)),
            pl.BlockSpec((ROW_TILE, IDX_DIM), lambda i: (i, 0)),
            pl.BlockSpec((IDX_HEADS, ROW_TILE), lambda i: (0, i)),
        ],
        out_shape=[
            jax.ShapeDtypeStruct((A_Q_COLS, TOKENS), BF16),
            jax.ShapeDtypeStruct((TOKENS, A_LAT), BF16),
            jax.ShapeDtypeStruct((A_LAT, TOKENS), BF16),
            jax.ShapeDtypeStruct((IDX_HEADS * IDX_DIM, TOKENS), BF16),
            jax.ShapeDtypeStruct((TOKENS, IDX_DIM), BF16),
            jax.ShapeDtypeStruct((IDX_HEADS, TOKENS), F32),
        ],
        compiler_params=_params("parallel"),
        name="dsa_in_proj",
    )(x2, g, wck, wt, qg, kg, iqg, ikg)


def _ordered_bits_to_float(u):
    k = u ^ jnp.int32(-2 ** 31)
    bits = k ^ ((k >> 31) & jnp.int32(0x7FFFFFFF))
    return pltpu.bitcast(bits, F32)


def _dsa_kernel(qt_ref, qit_ref, wit_ref, ki_ref, k_ref, ct_ref, ot_ref,
                score_scr, score16_scr, thr_scr, cut_scr, s_scr, m_scr, l_scr, acc_scr):
    step = pl.program_id(1)
    n_chunks = step + 1
    cols = A_HEADS * Q_BLOCK
    qpos = step * Q_STEP + lax.broadcasted_iota(jnp.int32, (1, Q_STEP), 1)
    limit = ((qpos >> 6) + 1) << 6

    def block_lanes(g):
        return slice(g * Q_BLOCK, (g + 1) * Q_BLOCK)

    def block_cols(g):
        return slice(g * cols, (g + 1) * cols)

    def key_pos(r0, rows):
        return r0 + lax.broadcasted_iota(jnp.int32, (rows, Q_STEP), 0)

    def for_chunks(fn):
        def pair(jj, carry):
            r0 = pl.multiple_of(jj * (2 * KEY_CHUNK), 2 * KEY_CHUNK)
            fn(r0)
            fn(pl.multiple_of(r0 + KEY_CHUNK, KEY_CHUNK))
            return carry
        lax.fori_loop(0, n_chunks // 2, pair, 0)

        @pl.when(n_chunks % 2 == 1)
        def _():
            fn(pl.multiple_of((n_chunks - 1) * KEY_CHUNK, KEY_CHUNK))

    def count(n, pred_fn):
        acc = None
        for j in range(n):
            r0 = j * KEY_CHUNK
            hit = pred_fn(score_scr[r0:r0 + KEY_CHUNK, :], key_pos(r0, KEY_CHUNK))
            part = jnp.sum(hit.reshape(KEY_CHUNK // COUNT_ROWS, COUNT_ROWS, Q_STEP), axis=0)
            acc = part if acc is None else acc + part
        return jnp.sum(acc, axis=0, keepdims=True)

    thr_scr[...] = jnp.full(thr_scr.shape, -jnp.inf, F32)
    cut_scr[...] = jnp.full(cut_scr.shape, 2 * SEQ, jnp.int32)

    @pl.when(n_chunks * KEY_CHUNK <= TOPK)
    def _():
        score_scr[0:KEY_CHUNK, :] = jnp.zeros((KEY_CHUNK, Q_STEP), F32)

    @pl.when(n_chunks * KEY_CHUNK > TOPK)
    def _():
        def score_chunk(r0):
            kic = ki_ref[pl.ds(r0, KEY_CHUNK), :]
            acc = jnp.zeros((KEY_CHUNK, Q_STEP), F32)
            for h in range(IDX_HEADS):
                rel = _dot(kic, qit_ref[h * IDX_DIM:(h + 1) * IDX_DIM, :])
                acc = acc + jnp.maximum(rel, 0.0) * wit_ref[h:h + 1, :]
            masked = jnp.where(key_pos(r0, KEY_CHUNK) < limit, acc, -jnp.inf)
            score_scr[pl.ds(r0, KEY_CHUNK), :] = masked
            score16_scr[pl.ds(r0, KEY_CHUNK), :] = masked.astype(BF16)
        for_chunks(score_chunk)

    def count_coarse(n, cand):
        one = jnp.ones((), BF16)
        zero = jnp.zeros((), BF16)
        accs = [None, None]
        for j in range(n):
            r0 = j * KEY_CHUNK
            hit = jnp.where(score16_scr[r0:r0 + KEY_CHUNK, :] >= cand, one, zero)
            for i, r in enumerate(range(0, KEY_CHUNK, COUNT_ROWS)):
                part = hit[r:r + COUNT_ROWS, :]
                accs[i % 2] = part if accs[i % 2] is None else accs[i % 2] + part
        return jnp.sum(accs[0].astype(F32) + accs[1].astype(F32), axis=0, keepdims=True)

    def search(n):
        def coarse_body(b, u):
            cand = u | (jnp.int32(1) << (31 - b))
            n_ge = count_coarse(n, _ordered_bits_to_float(cand).astype(BF16))
            return jnp.where(n_ge >= float(TOPK), cand, u)
        u16 = lax.fori_loop(0, COARSE_BITS, coarse_body, jnp.zeros((1, Q_STEP), jnp.int32))

        base = u16 - jnp.int32(1 << (31 - COARSE_BITS))

        def fine_body(b, d):
            cand = d | (jnp.int32(1) << (FINE_BITS - 1 - b))
            cf = _ordered_bits_to_float(base + cand)
            n_ge = count(n, lambda s, _: jnp.where(s >= cf, 1.0, 0.0))
            return jnp.where(n_ge >= float(TOPK), cand, d)
        d = lax.fori_loop(0, FINE_BITS, fine_body, jnp.zeros((1, Q_STEP), jnp.int32))
        thr = _ordered_bits_to_float(base + d)
        thr_scr[...] = jnp.broadcast_to(thr, thr_scr.shape)

        excess = count(n, lambda s, _: jnp.where(s >= thr, 1.0, 0.0)) - float(TOPK)
        max_excess = jnp.max(excess).astype(jnp.int32)

        @pl.when(jnp.logical_and(max_excess > 0, max_excess <= INDEX_BITS))
        def _():
            def drop_body(i, cut):
                top = None
                for j in range(n):
                    r0 = j * KEY_CHUNK
                    kp = key_pos(r0, KEY_CHUNK)
                    tied = jnp.where(score_scr[r0:r0 + KEY_CHUNK, :] == thr, jnp.where(kp < cut, kp, -1), -1)
                    part = jnp.max(tied.reshape(KEY_CHUNK // COUNT_ROWS, COUNT_ROWS, Q_STEP), axis=0)
                    top = part if top is None else jnp.maximum(top, part)
                top = jnp.max(top, axis=0, keepdims=True)
                return jnp.where(excess > i.astype(F32), top, cut)
            cut = lax.fori_loop(0, max_excess, drop_body, jnp.full((1, Q_STEP), 2 * SEQ, jnp.int32))
            cut_scr[...] = jnp.broadcast_to(cut, cut_scr.shape)

        @pl.when(max_excess > INDEX_BITS)
        def _():
            n_gt = count(n, lambda s, _: jnp.where(s > thr, 1.0, 0.0))
            need = float(TOPK) - n_gt

            def cut_body(b, cut):
                cand = cut | (jnp.int32(1) << (INDEX_BITS - 1 - b))
                n_eq = count(n, lambda s, kp: jnp.where(s == thr, jnp.where(kp < cand, 1.0, 0.0), 0.0))
                return jnp.where(n_eq <= need, cand, cut)
            cut = lax.fori_loop(0, INDEX_BITS, cut_body, jnp.zeros((1, Q_STEP), jnp.int32))
            cut_scr[...] = jnp.broadcast_to(cut, cut_scr.shape)

    for n in range(TOPK // KEY_CHUNK + 1, SEQ // KEY_CHUNK + 1):
        pl.when(n_chunks == n)(functools.partial(search, n))

    thr = thr_scr[0:1, :]
    cut = cut_scr[0:1, :]
    eye = (lax.broadcasted_iota(jnp.int32, (Q_BLOCK, Q_BLOCK), 0)
           == lax.broadcasted_iota(jnp.int32, (Q_BLOCK, Q_BLOCK), 1))
    eye = jnp.concatenate([jnp.where(eye, 1.0, 0.0).astype(BF16)] * A_HEADS, axis=1)
    qt_aug = []
    for g in range(Q_GROUPS):
        qt = jnp.concatenate([qt_ref[h * A_LAT:(h + 1) * A_LAT, block_lanes(g)] for h in range(A_HEADS)], axis=1)
        qt_aug.append(jnp.concatenate([qt, eye], axis=0))

    def row_groups(a):
        return a.reshape(KEY_CHUNK // SUBLANES, SUBLANES, a.shape[-1])

    m_scr[...] = jnp.full(m_scr.shape, -jnp.inf, F32)

    def logits_chunk(r0):
        sc = score_scr[pl.ds(r0, KEY_CHUNK), :]
        kp = key_pos(r0, KEY_CHUNK)
        tie_bias = jnp.where(sc == thr, jnp.where(kp < cut, 0.0, MASK_NEG), MASK_NEG)
        bias = jnp.where(kp < limit, jnp.where(sc > thr, 0.0, tie_bias), MASK_NEG).astype(BF16)
        kc = k_ref[pl.ds(r0, KEY_CHUNK), :]
        for g in range(Q_GROUPS):
            st = _dot(jnp.concatenate([kc, bias[:, block_lanes(g)]], axis=1), qt_aug[g])
            s_scr[pl.ds(r0, KEY_CHUNK), block_cols(g)] = st
            m_scr[:, block_cols(g)] = jnp.maximum(m_scr[:, block_cols(g)], jnp.max(row_groups(st), axis=0))
    for_chunks(logits_chunk)
    m = jnp.max(m_scr[...], axis=0, keepdims=True)

    l_scr[...] = jnp.zeros(l_scr.shape, F32)
    acc_scr[...] = jnp.zeros(acc_scr.shape, F32)

    def pv_chunk(r0):
        p = jnp.exp2(s_scr[pl.ds(r0, KEY_CHUNK), :] - m)
        l_scr[...] += jnp.sum(row_groups(p), axis=0)
        acc_scr[...] += _dot(ct_ref[:, pl.ds(r0, KEY_CHUNK)], p.astype(BF16))
    for_chunks(pv_chunk)

    o = acc_scr[...] / jnp.sum(l_scr[...], axis=0, keepdims=True)
    for g in range(Q_GROUPS):
        for h in range(A_HEADS):
            c0 = g * cols + h * Q_BLOCK
            ot_ref[h * A_LAT:(h + 1) * A_LAT, block_lanes(g)] = o[:, c0:c0 + Q_BLOCK].astype(BF16)


def _dsa(qt, qit, wit, ki, k, ct):
    assert Q_STEP == KEY_CHUNK
    nq = SEQ // Q_STEP
    cols = Q_GROUPS * A_HEADS * Q_BLOCK
    return pl.pallas_call(
        _dsa_kernel,
        grid=(BATCH, nq),
        in_specs=[
            pl.BlockSpec((A_Q_COLS, Q_STEP), lambda b, i: (0, b * nq + i)),
            pl.BlockSpec((IDX_HEADS * IDX_DIM, Q_STEP), lambda b, i: (0, b * nq + i)),
            pl.BlockSpec((IDX_HEADS, Q_STEP), lambda b, i: (0, b * nq + i)),
            pl.BlockSpec((SEQ, IDX_DIM), lambda b, i: (b, 0)),
            pl.BlockSpec((SEQ, A_LAT), lambda b, i: (b, 0)),
            pl.BlockSpec((A_LAT, SEQ), lambda b, i: (0, b)),
        ],
        out_specs=pl.BlockSpec((A_Q_COLS, Q_STEP), lambda b, i: (0, b * nq + i)),
        out_shape=jax.ShapeDtypeStruct((A_Q_COLS, TOKENS), BF16),
        scratch_shapes=[
            pltpu.VMEM((SEQ, Q_STEP), F32),
            pltpu.VMEM((SEQ, Q_STEP), BF16),
            pltpu.VMEM((SUBLANES, Q_STEP), F32),
            pltpu.VMEM((SUBLANES, Q_STEP), jnp.int32),
            pltpu.VMEM((SEQ, cols), F32),
            pltpu.VMEM((SUBLANES, cols), F32),
            pltpu.VMEM((SUBLANES, cols), F32),
            pltpu.VMEM((A_LAT, cols), F32),
        ],
        compiler_params=_params("parallel", "arbitrary"),
        name="dsa_core",
    )(qt, qit, wit, ki, k, ct)


SEG = ROW_TILE // SUBLANES
HALO = (CONV_W - 1) * SUBLANES


def _interleave_rows(a):
    return jnp.transpose(a.reshape(SUBLANES, SEG, a.shape[-1]), (1, 0, 2)).reshape(a.shape)


def _deinterleave_rows(a):
    return jnp.transpose(a.reshape(SEG, SUBLANES, a.shape[-1]), (1, 0, 2)).reshape(a.shape)


def _ffn_kernel(m_transposed, x_ref, m_ref, wout_ref, g_ref, wup_ref, cw_ref, cb_ref, wdown_ref, o_ref,
                hb_scr, u_even, u_odd, gate_scr, carry_scr):
    t = pl.program_id(0)
    first_of_sequence = (t % (SEQ // ROW_TILE)) == 0
    mixer = (_dot_tn if m_transposed else _dot)(m_ref[...], wout_ref[...])
    x1 = x_ref[...] + mixer
    o_ref[...] = x1
    hb_scr[...] = _interleave_rows(_rms(x1, g_ref[...])).astype(BF16)
    first_sublane = lax.broadcasted_iota(jnp.int32, (HALO, 2 * FF_CHUNK), 0) % SUBLANES == 0

    def chunk_cols(ref, c):
        if isinstance(c, int):
            a0, b0 = c * FF_CHUNK, D_FF + c * FF_CHUNK
        else:
            a0 = pl.multiple_of(c * FF_CHUNK, FF_CHUNK)
            b0 = pl.multiple_of(D_FF + c * FF_CHUNK, FF_CHUNK)
        return ref[:, pl.ds(a0, FF_CHUNK)], ref[:, pl.ds(b0, FF_CHUNK)]

    def up(c):
        hb = hb_scr[...]
        wa, wb = chunk_cols(wup_ref, c)
        return jnp.concatenate([_dot(hb, wa), _dot(hb, wb)], axis=1)

    def down_one_sublane(a):
        return jnp.concatenate([pltpu.roll(a[r:r + SUBLANES, :], 1, 0)
                                for r in range(0, a.shape[0], SUBLANES)], axis=0)

    def act(c, u):
        tail = u[ROW_TILE - HALO:, :]
        prev = jnp.where(first_of_sequence, 0.0, carry_scr[c])
        carry_scr[c] = tail
        wrap = jnp.where(first_sublane, down_one_sublane(prev), down_one_sublane(tail))
        cw = jnp.concatenate(chunk_cols(cw_ref, c), axis=1)
        cb = jnp.concatenate(chunk_cols(cb_ref, c), axis=1)
        v = u * cw[CONV_W - 1:CONV_W, :] + cb
        for d in range(1, CONV_W):
            shifted = jnp.concatenate([wrap[HALO - d * SUBLANES:, :], u[:ROW_TILE - d * SUBLANES, :]], axis=0)
            v = v + shifted * cw[CONV_W - 1 - d:CONV_W - d, :]
        a = v[:, :FF_CHUNK]
        b = v[:, FF_CHUNK:]
        return (a * jax.nn.sigmoid(a) * b).astype(BF16)

    def stage(c, u_cur, u_next, has_up=True):
        if has_up:
            u_next[...] = up(c + 1)
        g0 = c * FF_CHUNK if isinstance(c, int) else pl.multiple_of(c * FF_CHUNK, FF_CHUNK)
        gate_scr[:, pl.ds(g0, FF_CHUNK)] = act(c, u_cur[...])

    def even_stage(c, **kw):
        stage(c, u_even, u_odd, **kw)

    def odd_stage(c, **kw):
        stage(c, u_odd, u_even, **kw)

    u_even[...] = up(0)
    for c in range(N_FF_CHUNKS):
        (even_stage if c % 2 == 0 else odd_stage)(c, has_up=c + 1 < N_FF_CHUNKS)
    o_ref[...] += _deinterleave_rows(_dot(gate_scr[...], wdown_ref[...]))


def _ffn(x2, m, m_transposed, wout, g, wup, cw, cb, wdown):
    assert N_FF_CHUNKS % 2 == 1 and N_FF_CHUNKS >= 3
    n = TOKENS // ROW_TILE
    km = wout.shape[0]
    m_spec = (pl.BlockSpec((km, ROW_TILE), lambda i: (0, i)) if m_transposed
              else pl.BlockSpec((ROW_TILE, km), lambda i: (i, 0)))
    return pl.pallas_call(
        functools.partial(_ffn_kernel, m_transposed),
        grid=(n,),
        in_specs=[
            pl.BlockSpec((ROW_TILE, D_MODEL), lambda i: (i, 0)),
            m_spec,
            _const_spec((km, D_MODEL)),
            _const_spec((1, D_MODEL)),
            _const_spec((D_MODEL, 2 * D_FF)),
            _const_spec((CONV_W, 2 * D_FF)),
            _const_spec((1, 2 * D_FF)),
            _const_spec((D_FF, D_MODEL)),
        ],
        out_specs=pl.BlockSpec((ROW_TILE, D_MODEL), lambda i: (i, 0)),
        out_shape=jax.ShapeDtypeStruct((TOKENS, D_MODEL), F32),
        scratch_shapes=[
            pltpu.VMEM((ROW_TILE, D_MODEL), BF16),
            pltpu.VMEM((ROW_TILE, 2 * FF_CHUNK), F32),
            pltpu.VMEM((ROW_TILE, 2 * FF_CHUNK), F32),
            pltpu.VMEM((ROW_TILE, D_FF), BF16),
            pltpu.VMEM((N_FF_CHUNKS, HALO, 2 * FF_CHUNK), F32),
        ],
        compiler_params=_params("arbitrary"),
        name="conv_ffn",
    )(x2, m, wout, g, wup, cw, cb, wdown)


def _proj_b_kernel(x_ref, g_ref, w_ref, cos_ref, sin_ref, zeta_ref, q_out, k_out, kz_out, v_out, sg_out):
    hb = _rms(x_ref[...], g_ref[...]).astype(BF16)
    cos = cos_ref[...]
    sin = sin_ref[...]
    half = B_DK // 2

    def rotary(z):
        z1, z2 = z[:, :half], z[:, half:]
        return z1 * cos - z2 * sin, z1 * sin + z2 * cos

    dq = B_HEADS * B_DK
    dv = B_HEADS * B_DV
    for h in range(B_HEADS):
        gate = _dot(hb, w_ref[:, 2 * dq + dv + h * B_DV:2 * dq + dv + (h + 1) * B_DV])
        sg_out[:, h * B_DV:(h + 1) * B_DV] = (gate * jax.nn.sigmoid(gate)).astype(BF16)
    for h in range(B_HEADS):
        zq = _dot(hb, w_ref[:, h * B_DK:(h + 1) * B_DK])
        for part, r in enumerate(rotary(zq)):
            cols = slice(h * B_DK + part * half, h * B_DK + (part + 1) * half)
            q_out[:, cols] = r.astype(BF16)
        zk = _dot(hb, w_ref[:, dq + h * B_DK:dq + (h + 1) * B_DK])
        for part, r in enumerate(rotary(zk)):
            cols = slice(h * B_DK + part * half, h * B_DK + (part + 1) * half)
            kr = r * (B_DK ** -0.5)
            k_out[:, cols] = kr.astype(BF16)
            kz_out[:, cols] = (kr * zeta_ref[:, cols]).astype(BF16)
    for h in range(B_HEADS):
        v_out[:, h * B_DV:(h + 1) * B_DV] = _dot(
            hb, w_ref[:, 2 * dq + h * B_DV:2 * dq + (h + 1) * B_DV]).astype(BF16)


def _proj_b(x2, g, w, cos, sin, zeta_rows):
    n = TOKENS // ROW_TILE
    per_seq = SEQ // ROW_TILE
    dq = B_HEADS * B_DK
    dv = B_HEADS * B_DV
    return pl.pallas_call(
        _proj_b_kernel,
        grid=(n,),
        in_specs=[
            pl.BlockSpec((ROW_TILE, D_MODEL), lambda i: (i, 0)),
            _const_spec((1, D_MODEL)),
            _const_spec((D_MODEL, 2 * dq + 2 * dv)),
            pl.BlockSpec((ROW_TILE, B_DK // 2), lambda i: (i % per_seq, 0)),
            pl.BlockSpec((ROW_TILE, B_DK // 2), lambda i: (i % per_seq, 0)),
            _const_spec((ROW_TILE, dq)),
        ],
        out_specs=[
            pl.BlockSpec((ROW_TILE, dq), lambda i: (i, 0)),
            pl.BlockSpec((ROW_TILE, dq), lambda i: (i, 0)),
            pl.BlockSpec((ROW_TILE, dq), lambda i: (i, 0)),
            pl.BlockSpec((ROW_TILE, dv), lambda i: (i, 0)),
            pl.BlockSpec((ROW_TILE, dv), lambda i: (i, 0)),
        ],
        out_shape=[
            jax.ShapeDtypeStruct((TOKENS, dq), BF16),
            jax.ShapeDtypeStruct((TOKENS, dq), BF16),
            jax.ShapeDtypeStruct((TOKENS, dq), BF16),
            jax.ShapeDtypeStruct((TOKENS, dv), BF16),
            jax.ShapeDtypeStruct((TOKENS, dv), BF16),
        ],
        compiler_params=_params("parallel"),
        name="ret_in_proj",
    )(x2, g, w, cos, sin, zeta_rows)


RET_HEADS_PER_STEP = 2


def _retention_kernel(q_ref, k_ref, kz_ref, v_ref, sg_ref, decay_ref, xi_ref, gc_ref, og_ref,
                      y_ref, state_scr):
    state_scr[...] = jnp.zeros(state_scr.shape, F32)

    for n in range(SEQ // RET_CHUNK):
        r0 = n * RET_CHUNK
        for hh in range(RET_HEADS_PER_STEP):
            kcols = slice(hh * B_DK, (hh + 1) * B_DK)
            vcols = slice(hh * B_DV, (hh + 1) * B_DV)
            qc = q_ref[pl.ds(r0, RET_CHUNK), kcols]
            vc = v_ref[pl.ds(r0, RET_CHUNK), vcols]
            state = state_scr[hh]
            s = (_dot_nt(qc, k_ref[pl.ds(r0, RET_CHUNK), kcols]) * decay_ref[hh]).astype(BF16)
            ret = _dot(s, vc) + _dot(qc, state.astype(BF16)) * xi_ref[hh]
            state_scr[hh] = state * gc_ref[hh] + _dot_tn(kz_ref[pl.ds(r0, RET_CHUNK), kcols], vc)
            y = _rms(ret, og_ref[:, vcols]) * sg_ref[pl.ds(r0, RET_CHUNK), vcols].astype(F32)
            y_ref[pl.ds(r0, RET_CHUNK), vcols] = y.astype(BF16)


def _retention(q, k, kz, v, sg, decay, xi, gc, og):
    hs = RET_HEADS_PER_STEP
    return pl.pallas_call(
        _retention_kernel,
        grid=(BATCH, B_HEADS // hs),
        in_specs=[
            pl.BlockSpec((SEQ, hs * B_DK), lambda b, h: (b, h)),
            pl.BlockSpec((SEQ, hs * B_DK), lambda b, h: (b, h)),
            pl.BlockSpec((SEQ, hs * B_DK), lambda b, h: (b, h)),
            pl.BlockSpec((SEQ, hs * B_DV), lambda b, h: (b, h)),
            pl.BlockSpec((SEQ, hs * B_DV), lambda b, h: (b, h)),
            pl.BlockSpec((hs, RET_CHUNK, RET_CHUNK), lambda b, h: (h, 0, 0)),
            pl.BlockSpec((hs, RET_CHUNK, B_DV), lambda b, h: (h, 0, 0)),
            pl.BlockSpec((hs, 1, B_DV), lambda b, h: (h, 0, 0)),
            pl.BlockSpec((1, hs * B_DV), lambda b, h: (0, h)),
        ],
        out_specs=pl.BlockSpec((SEQ, hs * B_DV), lambda b, h: (b, h)),
        out_shape=jax.ShapeDtypeStruct((TOKENS, B_HEADS * B_DV), BF16),
        scratch_shapes=[pltpu.VMEM((hs, B_DK, B_DV), F32)],
        compiler_params=_params("parallel", "parallel"),
        name="retention",
    )(q, k, kz, v, sg, decay, xi, gc, og)


def _retention_tables():
    log_gamma = np.log1p(-(2.0 ** (-5.0 - np.arange(B_HEADS, dtype=np.float32)))).astype(np.float32)
    pos = np.arange(RET_CHUNK, dtype=np.float32)
    diff = pos[:, None] - pos[None, :]
    decay = np.where(diff[None] >= 0,
                     np.exp(np.maximum(diff, 0.0)[None] * log_gamma[:, None, None]), 0.0)
    xi = np.exp((pos + 1.0)[None, :] * log_gamma[:, None])
    zeta = np.exp((RET_CHUNK - 1.0 - pos)[None, :] * log_gamma[:, None])
    gc = np.exp(RET_CHUNK * log_gamma)
    xi_b = np.broadcast_to(xi[:, :, None], (B_HEADS, RET_CHUNK, B_DV))
    gc_b = np.broadcast_to(gc[:, None, None], (B_HEADS, 1, B_DV))
    zeta_rows = np.tile(np.repeat(zeta.T, B_DK, axis=1), (ROW_TILE // RET_CHUNK, 1))
    return tuple(jnp.asarray(a, F32) for a in (decay, xi_b, gc_b, zeta_rows))


def _rotary_tables():
    inv = (1.0 / (ROPE_BASE ** (np.arange(0, B_DK, 2, dtype=np.float32) / B_DK))).astype(np.float32)
    ang = np.arange(SEQ, dtype=np.float32)[:, None] * inv[None, :]
    return jnp.asarray(np.cos(ang), F32), jnp.asarray(np.sin(ang), F32)


def _ffn_weights(w_up, conv_w, conv_b, w_down):
    return w_up.astype(BF16), conv_w, conv_b[None, :], w_down.astype(BF16)


def _ldprobe_kernel(variant, x_ref, o_ref, scr):
    scr[...] = jnp.broadcast_to(x_ref[0:1, :], scr.shape).astype(scr.dtype)
    rows, lanes = scr.shape

    def body(i, acc):
        a0 = None
        for j in range(rows // 256):
            blk = scr[j * 256:(j + 1) * 256, :]
            if variant == "cmp":
                blk = jnp.where(blk >= acc[0:1, :].astype(blk.dtype), 1.0, 0.0)
            if variant == "bf16add":
                for r in range(0, 256, 32):
                    part = blk[r:r + 32, :]
                    a0 = part if a0 is None else a0 + part
            else:
                part = jnp.sum(blk.astype(F32).reshape(8, 32, lanes), axis=0)
                a0 = part if a0 is None else a0 + part
        return acc + a0.astype(F32)
    acc = lax.fori_loop(0, 64, body, jnp.zeros((32, lanes), F32))
    o_ref[...] = acc


def _ldprobe(x, variant, rows, lanes, dtype):
    return pl.pallas_call(
        functools.partial(_ldprobe_kernel, variant),
        in_specs=[pl.BlockSpec((8, lanes), lambda: (0, 0))],
        out_specs=pl.BlockSpec((32, lanes), lambda: (0, 0)),
        out_shape=jax.ShapeDtypeStruct((32, lanes), F32),
        scratch_shapes=[pltpu.VMEM((rows, lanes), dtype)],
        name=f"ldprobe_{variant}_{rows}x{lanes}_{jnp.dtype(dtype).name}",
    )(x)


def kernel(x, norm_mix_g, norm_ffn_g, a_w_in, a_q_g, a_k_g, a_iq_g, a_ik_g, a_w_out,
           b_w_in, b_out_g, b_w_out, f_w_up, f_conv_w, f_conv_b, f_w_down):
    xs = x[0, :8, :]
    pr = (_ldprobe(xs[:, :256], "sum", 2048, 256, F32).sum()
          + _ldprobe(xs[:, :256], "cmp", 2048, 256, F32).sum()
          + _ldprobe(xs[:, :128], "sum", 4096, 128, F32).sum()
          + _ldprobe(xs[:, :512], "sum", 1024, 512, F32).sum()
          + _ldprobe(xs[:, :256], "bf16add", 2048, 256, BF16).sum()
          + _ldprobe(xs[:, :256], "sum", 2048, 256, BF16).sum())
    x = x + 0.0 * pr
    x2 = x.reshape(TOKENS, D_MODEL)

    w = a_w_in[0]
    o1 = A_Q_COLS
    o2 = o1 + A_LAT
    o3 = o2 + IDX_HEADS * IDX_DIM
    o4 = o3 + IDX_DIM
    wck = jnp.concatenate(
        [w[:, o1:o2], w[:, o3:o4], jnp.zeros((D_MODEL, A_LAT - IDX_DIM), F32)], axis=1).astype(BF16)
    wt = jnp.concatenate([w[:, :o3], w[:, o4:]], axis=1).T.astype(BF16)
    qt, k, ct, qit, ki, wit = _proj_a(
        x2, norm_mix_g[0][None, :], wck, wt,
        a_q_g[0][:, None], a_k_g[0][None, :], a_iq_g[0][:, None], a_ik_g[0][None, :])
    ot = _dsa(qt, qit, wit, ki, k, ct)
    x2 = _ffn(x2, ot, True, a_w_out[0].astype(BF16), norm_ffn_g[0][None, :],
              *_ffn_weights(f_w_up[0], f_conv_w[0], f_conv_b[0], f_w_down[0]))

    cos, sin = _rotary_tables()
    decay, xi, gc, zeta_rows = _retention_tables()
    q, kk, kz, v, sg = _proj_b(x2, norm_mix_g[1][None, :], b_w_in[0].astype(BF16), cos, sin, zeta_rows)
    y = _retention(q, kk, kz, v, sg, decay, xi, gc, b_out_g[0][None, :])
    x2 = _ffn(x2, y, False, b_w_out[0].astype(BF16), norm_ffn_g[1][None, :],
              *_ffn_weights(f_w_up[1], f_conv_w[1], f_conv_b[1], f_w_down[1]))
    return x2.reshape(BATCH, SEQ, D_MODEL)
```

```python
import functools

import jax
import jax.numpy as jnp
import numpy as np
from jax import lax
from jax.experimental import pallas as pl
from jax.experimental.pallas import tpu as pltpu

D_MODEL = 1024
BATCH = 8
SEQ = 2048
TOKENS = BATCH * SEQ
CHUNK = 64
EPS = 1e-6
A_HEADS = 8
A_LAT = 128
IDX_HEADS = 8
IDX_DIM = 64
TOPK = 256
A_Q_COLS = A_HEADS * A_LAT
B_HEADS = 4
B_DK = 256
B_DV = 512
ROPE_BASE = 10000.0
RET_CHUNK = 256
D_FF = 2816
CONV_W = 3
FF_CHUNK = 256
N_FF_CHUNKS = D_FF // FF_CHUNK

LANES = 128
SUBLANES = 8
ROW_TILE = 512
Q_BLOCK = 128
Q_GROUPS = 2
Q_STEP = Q_GROUPS * Q_BLOCK
KEY_CHUNK = 256
COUNT_ROWS = 32
INDEX_BITS = 12
COARSE_BITS = 16
FINE_BITS = 17
CAST_BLOCK_ELEMS = 3 * 512 * 1024
VMEM_LIMIT = 56 * 1024 * 1024
MASK_NEG = -(2.0 ** 100)
LOG2_E = 1.4426950408889634

BF16 = jnp.bfloat16
F32 = jnp.float32


def _params(*sem):
    return pltpu.CompilerParams(dimension_semantics=sem, vmem_limit_bytes=VMEM_LIMIT)


def _const_spec(shape):
    nd = len(shape)
    return pl.BlockSpec(shape, lambda *_: (0,) * nd, pipeline_mode=pl.Buffered(1))


def _rms(x, g):
    return x * lax.rsqrt(jnp.mean(x * x, axis=-1, keepdims=True) + EPS) * g


def _dot(a, b):
    return jnp.dot(a, b, preferred_element_type=F32)


def _dot_nt(a, b):
    return lax.dot_general(a, b, (((1,), (1,)), ((), ())), preferred_element_type=F32)


def _dot_tn(a, b):
    return lax.dot_general(a, b, (((0,), (0,)), ((), ())), preferred_element_type=F32)


T_Q = 0
T_C = T_Q + A_Q_COLS
T_QI = T_C + A_LAT
T_WI = T_QI + IDX_HEADS * IDX_DIM
T_ROWS = T_WI + IDX_HEADS


def _proj_a_kernel(x_ref, g_ref, wck_ref, wt_ref, qg_ref, kg_ref, iqg_ref, ikg_ref,
                   qt_out, k_out, ct_out, qit_out, ki_out, wit_out):
    hb = _rms(x_ref[...], g_ref[...]).astype(BF16)
    ck = _dot(hb, wck_ref[...])
    k_out[...] = _rms(ck[:, :A_LAT], kg_ref[...]).astype(BF16)
    ki_out[...] = _rms(ck[:, A_LAT:A_LAT + IDX_DIM], ikg_ref[...]).astype(BF16)

    t = _dot_nt(wt_ref[...], hb)

    def head_norm(rows, gain):
        return rows * lax.rsqrt(jnp.mean(rows * rows, axis=0, keepdims=True) + EPS) * gain

    qg = qg_ref[...] * (A_LAT ** -0.5 * LOG2_E)
    for h in range(A_HEADS):
        r0 = T_Q + h * A_LAT
        qt_out[h * A_LAT:(h + 1) * A_LAT, :] = head_norm(t[r0:r0 + A_LAT, :], qg).astype(BF16)
    ct_out[...] = t[T_C:T_C + A_LAT, :].astype(BF16)
    for h in range(IDX_HEADS):
        r0 = T_QI + h * IDX_DIM
        qit_out[h * IDX_DIM:(h + 1) * IDX_DIM, :] = head_norm(t[r0:r0 + IDX_DIM, :], iqg_ref[...]).astype(BF16)
    wit_out[...] = t[T_WI:, :] * (IDX_HEADS ** -0.5 * IDX_DIM ** -0.5)


def _proj_a(x2, g, wck, wt, qg, kg, iqg, ikg):
    n = TOKENS // ROW_TILE
    return pl.pallas_call(
        _proj_a_kernel,
        grid=(n,),
        in_specs=[
            pl.BlockSpec((ROW_TILE, D_MODEL), lambda i: (i, 0)),
            _const_spec((1, D_MODEL)),
            _const_spec((D_MODEL, 2 * A_LAT)),
            _const_spec((T_ROWS, D_MODEL)),
            _const_spec((A_LAT, 1)),
            _const_spec((1, A_LAT)),
            _const_spec((IDX_DIM, 1)),
            _const_spec((1, IDX_DIM)),
        ],
        out_specs=[
            pl.BlockSpec((A_Q_COLS, ROW_TILE), lambda i: (0, i)),
            pl.BlockSpec((ROW_TILE, A_LAT), lambda i: (i, 0)),
            pl.BlockSpec((A_LAT, ROW_TILE), lambda i: (0, i)),
            pl.BlockSpec((IDX_HEADS * IDX_DIM, ROW_TILE), lambda i: (0, i)),
            pl.BlockSpec((ROW_TILE, IDX_DIM), lambda i: (i, 0)),
            pl.BlockSpec((IDX_HEADS, ROW_TILE), lambda i: (0, i)),
        ],
        out_shape=[
            jax.ShapeDtypeStruct((A_Q_COLS, TOKENS), BF16),
            jax.ShapeDtypeStruct((TOKENS, A_LAT), BF16),
            jax.ShapeDtypeStruct((A_LAT, TOKENS), BF16),
            jax.ShapeDtypeStruct((IDX_HEADS * IDX_DIM, TOKENS), BF16),
            jax.ShapeDtypeStruct((TOKENS, IDX_DIM), BF16),
            jax.ShapeDtypeStruct((IDX_HEADS, TOKENS), F32),
        ],
        compiler_params=_params("parallel"),
        name="dsa_in_proj",
    )(x2, g, wck, wt, qg, kg, iqg, ikg)


def _ordered_bits_to_float(u):
    k = u ^ jnp.int32(-2 ** 31)
    bits = k ^ ((k >> 31) & jnp.int32(0x7FFFFFFF))
    return pltpu.bitcast(bits, F32)


def _dsa_kernel(qt_ref, qit_ref, wit_ref, ki_ref, k_ref, ct_ref, ot_ref,
                score_scr, score16_scr, thr_scr, cut_scr, s_scr, m_scr, l_scr, acc_scr):
    step = pl.program_id(1)
    n_chunks = step + 1
    cols = A_HEADS * Q_BLOCK
    qpos = step * Q_STEP + lax.broadcasted_iota(jnp.int32, (1, Q_STEP), 1)
    limit = ((qpos >> 6) + 1) << 6

    def block_lanes(g):
        return slice(g * Q_BLOCK, (g + 1) * Q_BLOCK)

    def block_cols(g):
        return slice(g * cols, (g + 1) * cols)

    def key_pos(r0, rows):
        return r0 + lax.broadcasted_iota(jnp.int32, (rows, Q_STEP), 0)

    def for_chunks(fn):
        def pair(jj, carry):
            r0 = pl.multiple_of(jj * (2 * KEY_CHUNK), 2 * KEY_CHUNK)
            fn(r0)
            fn(pl.multiple_of(r0 + KEY_CHUNK, KEY_CHUNK))
            return carry
        lax.fori_loop(0, n_chunks // 2, pair, 0)

        @pl.when(n_chunks % 2 == 1)
        def _():
            fn(pl.multiple_of((n_chunks - 1) * KEY_CHUNK, KEY_CHUNK))

    def count(n, pred_fn):
        acc = None
        for j in range(n):
            r0 = j * KEY_CHUNK
            hit = pred_fn(score_scr[r0:r0 + KEY_CHUNK, :], key_pos(r0, KEY_CHUNK))
            part = jnp.sum(hit.reshape(KEY_CHUNK // COUNT_ROWS, COUNT_ROWS, Q_STEP), axis=0)
            acc = part if acc is None else acc + part
        return jnp.sum(acc, axis=0, keepdims=True)

    thr_scr[...] = jnp.full(thr_scr.shape, -jnp.inf, F32)
    cut_scr[...] = jnp.full(cut_scr.shape, 2 * SEQ, jnp.int32)

    @pl.when(n_chunks * KEY_CHUNK <= TOPK)
    def _():
        score_scr[0:KEY_CHUNK, :] = jnp.zeros((KEY_CHUNK, Q_STEP), F32)

    @pl.when(n_chunks * KEY_CHUNK > TOPK)
    def _():
        def score_chunk(r0):
            kic = ki_ref[pl.ds(r0, KEY_CHUNK), :]
            acc = jnp.zeros((KEY_CHUNK, Q_STEP), F32)
            for h in range(IDX_HEADS):
                rel = _dot(kic, qit_ref[h * IDX_DIM:(h + 1) * IDX_DIM, :])
                acc = acc + jnp.maximum(rel, 0.0) * wit_ref[h:h + 1, :]
            masked = jnp.where(key_pos(r0, KEY_CHUNK) < limit, acc, -jnp.inf)
            score_scr[pl.ds(r0, KEY_CHUNK), :] = masked
            score16_scr[pl.ds(r0, KEY_CHUNK), :] = masked.astype(BF16)
        for_chunks(score_chunk)

    def count_coarse(n, cand):
        one = jnp.ones((), BF16)
        zero = jnp.zeros((), BF16)
        accs = [None, None]
        for j in range(n):
            r0 = j * KEY_CHUNK
            hit = jnp.where(score16_scr[r0:r0 + KEY_CHUNK, :] >= cand, one, zero)
            for i, r in enumerate(range(0, KEY_CHUNK, COUNT_ROWS)):
                part = hit[r:r + COUNT_ROWS, :]
                accs[i % 2] = part if accs[i % 2] is None else accs[i % 2] + part
        return jnp.sum(accs[0].astype(F32) + accs[1].astype(F32), axis=0, keepdims=True)

    def search(n):
        def coarse_body(b, u):
            cand = u | (jnp.int32(1) << (31 - b))
            n_ge = count_coarse(n, _ordered_bits_to_float(cand).astype(BF16))
            return jnp.where(n_ge >= float(TOPK), cand, u)
        u16 = lax.fori_loop(0, COARSE_BITS, coarse_body, jnp.zeros((1, Q_STEP), jnp.int32))

        base = u16 - jnp.int32(1 << (31 - COARSE_BITS))

        def fine_body(b, d):
            cand = d | (jnp.int32(1) << (FINE_BITS - 1 - b))
            cf = _ordered_bits_to_float(base + cand)
            n_ge = count(n, lambda s, _: jnp.where(s >= cf, 1.0, 0.0))
            return jnp.where(n_ge >= float(TOPK), cand, d)
        d = lax.fori_loop(0, FINE_BITS, fine_body, jnp.zeros((1, Q_STEP), jnp.int32))
        thr = _ordered_bits_to_float(base + d)
        thr_scr[...] = jnp.broadcast_to(thr, thr_scr.shape)

        excess = count(n, lambda s, _: jnp.where(s >= thr, 1.0, 0.0)) - float(TOPK)
        max_excess = jnp.max(excess).astype(jnp.int32)

        @pl.when(jnp.logical_and(max_excess > 0, max_excess <= INDEX_BITS))
        def _():
            def drop_body(i, cut):
                top = None
                for j in range(n):
                    r0 = j * KEY_CHUNK
                    kp = key_pos(r0, KEY_CHUNK)
                    tied = jnp.where(score_scr[r0:r0 + KEY_CHUNK, :] == thr, jnp.where(kp < cut, kp, -1), -1)
                    part = jnp.max(tied.reshape(KEY_CHUNK // COUNT_ROWS, COUNT_ROWS, Q_STEP), axis=0)
                    top = part if top is None else jnp.maximum(top, part)
                top = jnp.max(top, axis=0, keepdims=True)
                return jnp.where(excess > i.astype(F32), top, cut)
            cut = lax.fori_loop(0, max_excess, drop_body, jnp.full((1, Q_STEP), 2 * SEQ, jnp.int32))
            cut_scr[...] = jnp.broadcast_to(cut, cut_scr.shape)

        @pl.when(max_excess > INDEX_BITS)
        def _():
            n_gt = count(n, lambda s, _: jnp.where(s > thr, 1.0, 0.0))
            need = float(TOPK) - n_gt

            def cut_body(b, cut):
                cand = cut | (jnp.int32(1) << (INDEX_BITS - 1 - b))
                n_eq = count(n, lambda s, kp: jnp.where(s == thr, jnp.where(kp < cand, 1.0, 0.0), 0.0))
                return jnp.where(n_eq <= need, cand, cut)
            cut = lax.fori_loop(0, INDEX_BITS, cut_body, jnp.zeros((1, Q_STEP), jnp.int32))
            cut_scr[...] = jnp.broadcast_to(cut, cut_scr.shape)

    for n in range(TOPK // KEY_CHUNK + 1, SEQ // KEY_CHUNK + 1):
        pl.when(n_chunks == n)(functools.partial(search, n))

    thr = thr_scr[0:1, :]
    cut = cut_scr[0:1, :]
    eye = (lax.broadcasted_iota(jnp.int32, (Q_BLOCK, Q_BLOCK), 0)
           == lax.broadcasted_iota(jnp.int32, (Q_BLOCK, Q_BLOCK), 1))
    eye = jnp.concatenate([jnp.where(eye, 1.0, 0.0).astype(BF16)] * A_HEADS, axis=1)
    qt_aug = []
    for g in range(Q_GROUPS):
        qt = jnp.concatenate([qt_ref[h * A_LAT:(h + 1) * A_LAT, block_lanes(g)] for h in range(A_HEADS)], axis=1)
        qt_aug.append(jnp.concatenate([qt, eye], axis=0))

    def row_groups(a):
        return a.reshape(KEY_CHUNK // SUBLANES, SUBLANES, a.shape[-1])

    m_scr[...] = jnp.full(m_scr.shape, -jnp.inf, F32)

    def logits_chunk(r0):
        sc = score_scr[pl.ds(r0, KEY_CHUNK), :]
        kp = key_pos(r0, KEY_CHUNK)
        tie_bias = jnp.where(sc == thr, jnp.where(kp < cut, 0.0, MASK_NEG), MASK_NEG)
        bias = jnp.where(kp < limit, jnp.where(sc > thr, 0.0, tie_bias), MASK_NEG).astype(BF16)
        kc = k_ref[pl.ds(r0, KEY_CHUNK), :]
        for g in range(Q_GROUPS):
            st = _dot(jnp.concatenate([kc, bias[:, block_lanes(g)]], axis=1), qt_aug[g])
            s_scr[pl.ds(r0, KEY_CHUNK), block_cols(g)] = st
            m_scr[:, block_cols(g)] = jnp.maximum(m_scr[:, block_cols(g)], jnp.max(row_groups(st), axis=0))
    for_chunks(logits_chunk)
    m = jnp.max(m_scr[...], axis=0, keepdims=True)

    l_scr[...] = jnp.zeros(l_scr.shape, F32)
    acc_scr[...] = jnp.zeros(acc_scr.shape, F32)

    def pv_chunk(r0):
        p = jnp.exp2(s_scr[pl.ds(r0, KEY_CHUNK), :] - m)
        l_scr[...] += jnp.sum(row_groups(p), axis=0)
        acc_scr[...] += _dot(ct_ref[:, pl.ds(r0, KEY_CHUNK)], p.astype(BF16))
    for_chunks(pv_chunk)

    o = acc_scr[...] / jnp.sum(l_scr[...], axis=0, keepdims=True)
    for g in range(Q_GROUPS):
        for h in range(A_HEADS):
            c0 = g * cols + h * Q_BLOCK
            ot_ref[h * A_LAT:(h + 1) * A_LAT, block_lanes(g)] = o[:, c0:c0 + Q_BLOCK].astype(BF16)


def _dsa(qt, qit, wit, ki, k, ct):
    assert Q_STEP == KEY_CHUNK
    nq = SEQ // Q_STEP
    cols = Q_GROUPS * A_HEADS * Q_BLOCK
    return pl.pallas_call(
        _dsa_kernel,
        grid=(BATCH, nq),
        in_specs=[
            pl.BlockSpec((A_Q_COLS, Q_STEP), lambda b, i: (0, b * nq + i)),
            pl.BlockSpec((IDX_HEADS * IDX_DIM, Q_STEP), lambda b, i: (0, b * nq + i)),
            pl.BlockSpec((IDX_HEADS, Q_STEP), lambda b, i: (0, b * nq + i)),
            pl.BlockSpec((SEQ, IDX_DIM), lambda b, i: (b, 0)),
            pl.BlockSpec((SEQ, A_LAT), lambda b, i: (b, 0)),
            pl.BlockSpec((A_LAT, SEQ), lambda b, i: (0, b)),
        ],
        out_specs=pl.BlockSpec((A_Q_COLS, Q_STEP), lambda b, i: (0, b * nq + i)),
        out_shape=jax.ShapeDtypeStruct((A_Q_COLS, TOKENS), BF16),
        scratch_shapes=[
            pltpu.VMEM((SEQ, Q_STEP), F32),
            pltpu.VMEM((SEQ, Q_STEP), BF16),
            pltpu.VMEM((SUBLANES, Q_STEP), F32),
            pltpu.VMEM((SUBLANES, Q_STEP), jnp.int32),
            pltpu.VMEM((SEQ, cols), F32),
            pltpu.VMEM((SUBLANES, cols), F32),
            pltpu.VMEM((SUBLANES, cols), F32),
            pltpu.VMEM((A_LAT, cols), F32),
        ],
        compiler_params=_params("parallel", "arbitrary"),
        name="dsa_core",
    )(qt, qit, wit, ki, k, ct)


SEG = ROW_TILE // SUBLANES
HALO = (CONV_W - 1) * SUBLANES


def _interleave_rows(a):
    return jnp.transpose(a.reshape(SUBLANES, SEG, a.shape[-1]), (1, 0, 2)).reshape(a.shape)


def _deinterleave_rows(a):
    return jnp.transpose(a.reshape(SEG, SUBLANES, a.shape[-1]), (1, 0, 2)).reshape(a.shape)


def _ffn_kernel(m_transposed, x_ref, m_ref, wout_ref, g_ref, wup_ref, cw_ref, cb_ref, wdown_ref, o_ref,
                hb_scr, u_even, u_odd, gate_scr, carry_scr):
    t = pl.program_id(0)
    first_of_sequence = (t % (SEQ // ROW_TILE)) == 0
    mixer = (_dot_tn if m_transposed else _dot)(m_ref[...], wout_ref[...])
    x1 = x_ref[...] + mixer
    o_ref[...] = x1
    hb_scr[...] = _interleave_rows(_rms(x1, g_ref[...])).astype(BF16)
    first_sublane = lax.broadcasted_iota(jnp.int32, (HALO, 2 * FF_CHUNK), 0) % SUBLANES == 0

    def chunk_cols(ref, c):
        if isinstance(c, int):
            a0, b0 = c * FF_CHUNK, D_FF + c * FF_CHUNK
        else:
            a0 = pl.multiple_of(c * FF_CHUNK, FF_CHUNK)
            b0 = pl.multiple_of(D_FF + c * FF_CHUNK, FF_CHUNK)
        return ref[:, pl.ds(a0, FF_CHUNK)], ref[:, pl.ds(b0, FF_CHUNK)]

    def up(c):
        hb = hb_scr[...]
        wa, wb = chunk_cols(wup_ref, c)
        return jnp.concatenate([_dot(hb, wa), _dot(hb, wb)], axis=1)

    def down_one_sublane(a):
        return jnp.concatenate([pltpu.roll(a[r:r + SUBLANES, :], 1, 0)
                                for r in range(0, a.shape[0], SUBLANES)], axis=0)

    def act(c, u):
        tail = u[ROW_TILE - HALO:, :]
        prev = jnp.where(first_of_sequence, 0.0, carry_scr[c])
        carry_scr[c] = tail
        wrap = jnp.where(first_sublane, down_one_sublane(prev), down_one_sublane(tail))
        cw = jnp.concatenate(chunk_cols(cw_ref, c), axis=1)
        cb = jnp.concatenate(chunk_cols(cb_ref, c), axis=1)
        v = u * cw[CONV_W - 1:CONV_W, :] + cb
        for d in range(1, CONV_W):
            shifted = jnp.concatenate([wrap[HALO - d * SUBLANES:, :], u[:ROW_TILE - d * SUBLANES, :]], axis=0)
            v = v + shifted * cw[CONV_W - 1 - d:CONV_W - d, :]
        a = v[:, :FF_CHUNK]
        b = v[:, FF_CHUNK:]
        return (a * jax.nn.sigmoid(a) * b).astype(BF16)

    def stage(c, u_cur, u_next, has_up=True):
        if has_up:
            u_next[...] = up(c + 1)
        g0 = c * FF_CHUNK if isinstance(c, int) else pl.multiple_of(c * FF_CHUNK, FF_CHUNK)
        gate_scr[:, pl.ds(g0, FF_CHUNK)] = act(c, u_cur[...])

    def even_stage(c, **kw):
        stage(c, u_even, u_odd, **kw)

    def odd_stage(c, **kw):
        stage(c, u_odd, u_even, **kw)

    u_even[...] = up(0)
    for c in range(N_FF_CHUNKS):
        (even_stage if c % 2 == 0 else odd_stage)(c, has_up=c + 1 < N_FF_CHUNKS)
    o_ref[...] += _deinterleave_rows(_dot(gate_scr[...], wdown_ref[...]))


def _layer_spec(shape, layer):
    nd = len(shape)
    return pl.BlockSpec((None,) + tuple(shape), lambda *_: (layer,) + (0,) * nd, pipeline_mode=pl.Buffered(1))


def _ffn(x2, m, m_transposed, wout, g, layer, wup, cw, cb, wdown):
    assert N_FF_CHUNKS % 2 == 1 and N_FF_CHUNKS >= 3
    n = TOKENS // ROW_TILE
    km = wout.shape[0]
    m_spec = (pl.BlockSpec((km, ROW_TILE), lambda i: (0, i)) if m_transposed
              else pl.BlockSpec((ROW_TILE, km), lambda i: (i, 0)))
    return pl.pallas_call(
        functools.partial(_ffn_kernel, m_transposed),
        grid=(n,),
        in_specs=[
            pl.BlockSpec((ROW_TILE, D_MODEL), lambda i: (i, 0)),
            m_spec,
            _const_spec((km, D_MODEL)),
            _const_spec((1, D_MODEL)),
            _layer_spec((D_MODEL, 2 * D_FF), layer),
            _const_spec((CONV_W, 2 * D_FF)),
            _const_spec((1, 2 * D_FF)),
            _layer_spec((D_FF, D_MODEL), layer),
        ],
        out_specs=pl.BlockSpec((ROW_TILE, D_MODEL), lambda i: (i, 0)),
        out_shape=jax.ShapeDtypeStruct((TOKENS, D_MODEL), F32),
        scratch_shapes=[
            pltpu.VMEM((ROW_TILE, D_MODEL), BF16),
            pltpu.VMEM((ROW_TILE, 2 * FF_CHUNK), F32),
            pltpu.VMEM((ROW_TILE, 2 * FF_CHUNK), F32),
            pltpu.VMEM((ROW_TILE, D_FF), BF16),
            pltpu.VMEM((N_FF_CHUNKS, HALO, 2 * FF_CHUNK), F32),
        ],
        compiler_params=_params("arbitrary"),
        name="conv_ffn",
    )(x2, m, wout, g, wup, cw, cb, wdown)


def _proj_b_kernel(x_ref, g_ref, w_ref, cos_ref, sin_ref, zeta_ref, q_out, k_out, kz_out, v_out, sg_out):
    hb = _rms(x_ref[...], g_ref[...]).astype(BF16)
    cos = cos_ref[...]
    sin = sin_ref[...]
    half = B_DK // 2

    def rotary(z):
        z1, z2 = z[:, :half], z[:, half:]
        return z1 * cos - z2 * sin, z1 * sin + z2 * cos

    dq = B_HEADS * B_DK
    dv = B_HEADS * B_DV
    for h in range(B_HEADS):
        gate = _dot(hb, w_ref[:, 2 * dq + dv + h * B_DV:2 * dq + dv + (h + 1) * B_DV])
        sg_out[:, h * B_DV:(h + 1) * B_DV] = (gate * jax.nn.sigmoid(gate)).astype(BF16)
    for h in range(B_HEADS):
        zq = _dot(hb, w_ref[:, h * B_DK:(h + 1) * B_DK])
        for part, r in enumerate(rotary(zq)):
            cols = slice(h * B_DK + part * half, h * B_DK + (part + 1) * half)
            q_out[:, cols] = r.astype(BF16)
        zk = _dot(hb, w_ref[:, dq + h * B_DK:dq + (h + 1) * B_DK])
        for part, r in enumerate(rotary(zk)):
            cols = slice(h * B_DK + part * half, h * B_DK + (part + 1) * half)
            kr = r * (B_DK ** -0.5)
            k_out[:, cols] = kr.astype(BF16)
            kz_out[:, cols] = (kr * zeta_ref[:, cols]).astype(BF16)
    for h in range(B_HEADS):
        v_out[:, h * B_DV:(h + 1) * B_DV] = _dot(
            hb, w_ref[:, 2 * dq + h * B_DV:2 * dq + (h + 1) * B_DV]).astype(BF16)


def _proj_b(x2, g, w, cos, sin, zeta_rows):
    n = TOKENS // ROW_TILE
    per_seq = SEQ // ROW_TILE
    dq = B_HEADS * B_DK
    dv = B_HEADS * B_DV
    return pl.pallas_call(
        _proj_b_kernel,
        grid=(n,),
        in_specs=[
            pl.BlockSpec((ROW_TILE, D_MODEL), lambda i: (i, 0)),
            _const_spec((1, D_MODEL)),
            _const_spec((D_MODEL, 2 * dq + 2 * dv)),
            pl.BlockSpec((ROW_TILE, B_DK // 2), lambda i: (i % per_seq, 0)),
            pl.BlockSpec((ROW_TILE, B_DK // 2), lambda i: (i % per_seq, 0)),
            _const_spec((ROW_TILE, dq)),
        ],
        out_specs=[
            pl.BlockSpec((ROW_TILE, dq), lambda i: (i, 0)),
            pl.BlockSpec((ROW_TILE, dq), lambda i: (i, 0)),
            pl.BlockSpec((ROW_TILE, dq), lambda i: (i, 0)),
            pl.BlockSpec((ROW_TILE, dv), lambda i: (i, 0)),
            pl.BlockSpec((ROW_TILE, dv), lambda i: (i, 0)),
        ],
        out_shape=[
            jax.ShapeDtypeStruct((TOKENS, dq), BF16),
            jax.ShapeDtypeStruct((TOKENS, dq), BF16),
            jax.ShapeDtypeStruct((TOKENS, dq), BF16),
            jax.ShapeDtypeStruct((TOKENS, dv), BF16),
            jax.ShapeDtypeStruct((TOKENS, dv), BF16),
        ],
        compiler_params=_params("parallel"),
        name="ret_in_proj",
    )(x2, g, w, cos, sin, zeta_rows)


RET_HEADS_PER_STEP = 2


def _retention_kernel(q_ref, k_ref, kz_ref, v_ref, sg_ref, decay_ref, xi_ref, gc_ref, og_ref,
                      y_ref, state_scr):
    state_scr[...] = jnp.zeros(state_scr.shape, F32)

    for n in range(SEQ // RET_CHUNK):
        r0 = n * RET_CHUNK
        for hh in range(RET_HEADS_PER_STEP):
            kcols = slice(hh * B_DK, (hh + 1) * B_DK)
            vcols = slice(hh * B_DV, (hh + 1) * B_DV)
            qc = q_ref[pl.ds(r0, RET_CHUNK), kcols]
            vc = v_ref[pl.ds(r0, RET_CHUNK), vcols]
            state = state_scr[hh]
            s = (_dot_nt(qc, k_ref[pl.ds(r0, RET_CHUNK), kcols]) * decay_ref[hh]).astype(BF16)
            ret = _dot(s, vc) + _dot(qc, state.astype(BF16)) * xi_ref[hh]
            state_scr[hh] = state * gc_ref[hh] + _dot_tn(kz_ref[pl.ds(r0, RET_CHUNK), kcols], vc)
            y = _rms(ret, og_ref[:, vcols]) * sg_ref[pl.ds(r0, RET_CHUNK), vcols].astype(F32)
            y_ref[pl.ds(r0, RET_CHUNK), vcols] = y.astype(BF16)


def _retention(q, k, kz, v, sg, decay, xi, gc, og):
    hs = RET_HEADS_PER_STEP
    return pl.pallas_call(
        _retention_kernel,
        grid=(BATCH, B_HEADS // hs),
        in_specs=[
            pl.BlockSpec((SEQ, hs * B_DK), lambda b, h: (b, h)),
            pl.BlockSpec((SEQ, hs * B_DK), lambda b, h: (b, h)),
            pl.BlockSpec((SEQ, hs * B_DK), lambda b, h: (b, h)),
            pl.BlockSpec((SEQ, hs * B_DV), lambda b, h: (b, h)),
            pl.BlockSpec((SEQ, hs * B_DV), lambda b, h: (b, h)),
            pl.BlockSpec((hs, RET_CHUNK, RET_CHUNK), lambda b, h: (h, 0, 0)),
            pl.BlockSpec((hs, RET_CHUNK, B_DV), lambda b, h: (h, 0, 0)),
            pl.BlockSpec((hs, 1, B_DV), lambda b, h: (h, 0, 0)),
            pl.BlockSpec((1, hs * B_DV), lambda b, h: (0, h)),
        ],
        out_specs=pl.BlockSpec((SEQ, hs * B_DV), lambda b, h: (b, h)),
        out_shape=jax.ShapeDtypeStruct((TOKENS, B_HEADS * B_DV), BF16),
        scratch_shapes=[pltpu.VMEM((hs, B_DK, B_DV), F32)],
        compiler_params=_params("parallel", "parallel"),
        name="retention",
    )(q, k, kz, v, sg, decay, xi, gc, og)


def _retention_tables():
    log_gamma = np.log1p(-(2.0 ** (-5.0 - np.arange(B_HEADS, dtype=np.float32)))).astype(np.float32)
    pos = np.arange(RET_CHUNK, dtype=np.float32)
    diff = pos[:, None] - pos[None, :]
    decay = np.where(diff[None] >= 0,
                     np.exp(np.maximum(diff, 0.0)[None] * log_gamma[:, None, None]), 0.0)
    xi = np.exp((pos + 1.0)[None, :] * log_gamma[:, None])
    zeta = np.exp((RET_CHUNK - 1.0 - pos)[None, :] * log_gamma[:, None])
    gc = np.exp(RET_CHUNK * log_gamma)
    xi_b = np.broadcast_to(xi[:, :, None], (B_HEADS, RET_CHUNK, B_DV))
    gc_b = np.broadcast_to(gc[:, None, None], (B_HEADS, 1, B_DV))
    zeta_rows = np.tile(np.repeat(zeta.T, B_DK, axis=1), (ROW_TILE // RET_CHUNK, 1))
    return tuple(jnp.asarray(a, F32) for a in (decay, xi_b, gc_b, zeta_rows))


def _rotary_tables():
    inv = (1.0 / (ROPE_BASE ** (np.arange(0, B_DK, 2, dtype=np.float32) / B_DK))).astype(np.float32)
    ang = np.arange(SEQ, dtype=np.float32)[:, None] * inv[None, :]
    return jnp.asarray(np.cos(ang), F32), jnp.asarray(np.sin(ang), F32)


def _cast_kernel(w_ref, o_ref):
    o_ref[...] = w_ref[...].astype(BF16)


def _to_bf16(w):
    rows = int(np.prod(w.shape[:-1]))
    cols = w.shape[-1]
    block_rows = max(r for r in range(SUBLANES, rows + 1, SUBLANES)
                     if rows % r == 0 and (r * cols <= CAST_BLOCK_ELEMS or r == SUBLANES))
    out = pl.pallas_call(
        _cast_kernel,
        grid=(rows // block_rows,),
        in_specs=[pl.BlockSpec((block_rows, cols), lambda i: (i, 0))],
        out_specs=pl.BlockSpec((block_rows, cols), lambda i: (i, 0)),
        out_shape=jax.ShapeDtypeStruct((rows, cols), BF16),
        compiler_params=_params("parallel"),
        name="weights_to_bf16",
    )(w.reshape(rows, cols))
    return out.reshape(w.shape)


def kernel(x, norm_mix_g, norm_ffn_g, a_w_in, a_q_g, a_k_g, a_iq_g, a_ik_g, a_w_out,
           b_w_in, b_out_g, b_w_out, f_w_up, f_conv_w, f_conv_b, f_w_down):
    x2 = x.reshape(TOKENS, D_MODEL)

    w = a_w_in[0]
    o1 = A_Q_COLS
    o2 = o1 + A_LAT
    o3 = o2 + IDX_HEADS * IDX_DIM
    o4 = o3 + IDX_DIM
    wck = jnp.concatenate(
        [w[:, o1:o2], w[:, o3:o4], jnp.zeros((D_MODEL, A_LAT - IDX_DIM), F32)], axis=1).astype(BF16)
    wt = jnp.concatenate([w[:, :o3], w[:, o4:]], axis=1).T.astype(BF16)
    qt, k, ct, qit, ki, wit = _proj_a(
        x2, norm_mix_g[0][None, :], wck, wt,
        a_q_g[0][:, None], a_k_g[0][None, :], a_iq_g[0][:, None], a_ik_g[0][None, :])
    ot = _dsa(qt, qit, wit, ki, k, ct)
    w_up = _to_bf16(f_w_up)
    w_down = _to_bf16(f_w_down)
    x2 = _ffn(x2, ot, True, _to_bf16(a_w_out[0]), norm_ffn_g[0][None, :],
              0, w_up, f_conv_w[0], f_conv_b[0][None, :], w_down)

    cos, sin = _rotary_tables()
    decay, xi, gc, zeta_rows = _retention_tables()
    q, kk, kz, v, sg = _proj_b(x2, norm_mix_g[1][None, :], _to_bf16(b_w_in[0]), cos, sin, zeta_rows)
    y = _retention(q, kk, kz, v, sg, decay, xi, gc, b_out_g[0][None, :])
    x2 = _ffn(x2, y, False, _to_bf16(b_w_out[0]), norm_ffn_g[1][None, :],
              1, w_up, f_conv_w[1], f_conv_b[1][None, :], w_down)
    return x2.reshape(BATCH, SEQ, D_MODEL)
```

```python
import functools

import jax
import jax.numpy as jnp
import numpy as np
from jax import lax
from jax.experimental import pallas as pl
from jax.experimental.pallas import tpu as pltpu

D_MODEL = 1024
BATCH = 8
SEQ = 2048
TOKENS = BATCH * SEQ
CHUNK = 64
EPS = 1e-6
A_HEADS = 8
A_LAT = 128
IDX_HEADS = 8
IDX_DIM = 64
TOPK = 256
A_Q_COLS = A_HEADS * A_LAT
B_HEADS = 4
B_DK = 256
B_DV = 512
ROPE_BASE = 10000.0
RET_CHUNK = 256
D_FF = 2816
CONV_W = 3
FF_CHUNK = 256
N_FF_CHUNKS = D_FF // FF_CHUNK

LANES = 128
SUBLANES = 8
ROW_TILE = 512
Q_BLOCK = 128
Q_GROUPS = 2
Q_STEP = Q_GROUPS * Q_BLOCK
KEY_CHUNK = 256
CHUNKS_PER_LOOP_STEP = 4
COUNT_ROWS = 32
INDEX_BITS = 12
COARSE_BITS = 16
FINE_BITS = 17
BF16_TILE_ROWS = 16
VMEM_LIMIT = 56 * 1024 * 1024
MASK_NEG = -(2.0 ** 100)
LOG2_E = 1.4426950408889634

BF16 = jnp.bfloat16
F32 = jnp.float32


def _params(*sem):
    return pltpu.CompilerParams(dimension_semantics=sem, vmem_limit_bytes=VMEM_LIMIT)


def _const_spec(shape):
    nd = len(shape)
    return pl.BlockSpec(shape, lambda *_: (0,) * nd, pipeline_mode=pl.Buffered(1))


def _rms(x, g):
    return x * lax.rsqrt(jnp.mean(x * x, axis=-1, keepdims=True) + EPS) * g


def _dot(a, b):
    return jnp.dot(a, b, preferred_element_type=F32)


def _dot_nt(a, b):
    return lax.dot_general(a, b, (((1,), (1,)), ((), ())), preferred_element_type=F32)


def _dot_tn(a, b):
    return lax.dot_general(a, b, (((0,), (0,)), ((), ())), preferred_element_type=F32)


T_Q = 0
T_C = T_Q + A_Q_COLS
T_QI = T_C + A_LAT
T_WI = T_QI + IDX_HEADS * IDX_DIM
T_ROWS = T_WI + IDX_HEADS


def _proj_a_kernel(x_ref, g_ref, wck_ref, wt_ref, qg_ref, kg_ref, iqg_ref, ikg_ref,
                   qt_out, k_out, ct_out, qit_out, ki_out, wit_out):
    def head_norm(rows, gain):
        return rows * lax.rsqrt(jnp.mean(rows * rows, axis=0, keepdims=True) + EPS) * gain

    qg = qg_ref[...] * (A_LAT ** -0.5 * LOG2_E)
    hb = _rms(x_ref[...], g_ref[...]).astype(BF16)
    ck = _dot(hb, wck_ref[...])
    k_out[...] = _rms(ck[:, :A_LAT], kg_ref[...]).astype(BF16)
    ki_out[...] = _rms(ck[:, A_LAT:A_LAT + IDX_DIM], ikg_ref[...]).astype(BF16)

    t = _dot_nt(wt_ref[...], hb)
    for h in range(A_HEADS):
        r0 = T_Q + h * A_LAT
        qt_out[h * A_LAT:(h + 1) * A_LAT, :] = head_norm(t[r0:r0 + A_LAT, :], qg).astype(BF16)
    ct_out[...] = t[T_C:T_C + A_LAT, :].astype(BF16)
    for h in range(IDX_HEADS):
        r0 = T_QI + h * IDX_DIM
        qit_out[h * IDX_DIM:(h + 1) * IDX_DIM, :] = head_norm(t[r0:r0 + IDX_DIM, :], iqg_ref[...]).astype(BF16)
    wit_out[...] = t[T_WI:, :] * (IDX_HEADS ** -0.5 * IDX_DIM ** -0.5)


def _proj_a(x2, g, wck, wt, qg, kg, iqg, ikg):
    n = TOKENS // ROW_TILE
    return pl.pallas_call(
        _proj_a_kernel,
        grid=(n,),
        in_specs=[
            pl.BlockSpec((ROW_TILE, D_MODEL), lambda i: (i, 0)),
            _const_spec((1, D_MODEL)),
            _const_spec((D_MODEL, 2 * A_LAT)),
            _const_spec((T_ROWS, D_MODEL)),
            _const_spec((A_LAT, 1)),
            _const_spec((1, A_LAT)),
            _const_spec((IDX_DIM, 1)),
            _const_spec((1, IDX_DIM)),
        ],
        out_specs=[
            pl.BlockSpec((A_Q_COLS, ROW_TILE), lambda i: (0, i)),
            pl.BlockSpec((ROW_TILE, A_LAT), lambda i: (i, 0)),
            pl.BlockSpec((A_LAT, ROW_TILE), lambda i: (0, i)),
            pl.BlockSpec((IDX_HEADS * IDX_DIM, ROW_TILE), lambda i: (0, i)),
            pl.BlockSpec((ROW_TILE, IDX_DIM), lambda i: (i, 0)),
            pl.BlockSpec((IDX_HEADS, ROW_TILE), lambda i: (0, i)),
        ],
        out_shape=[
            jax.ShapeDtypeStruct((A_Q_COLS, TOKENS), BF16),
            jax.ShapeDtypeStruct((TOKENS, A_LAT), BF16),
            jax.ShapeDtypeStruct((A_LAT, TOKENS), BF16),
            jax.ShapeDtypeStruct((IDX_HEADS * IDX_DIM, TOKENS), BF16),
            jax.ShapeDtypeStruct((TOKENS, IDX_DIM), BF16),
            jax.ShapeDtypeStruct((IDX_HEADS, TOKENS), F32),
        ],
        compiler_params=_params("parallel"),
        name="dsa_in_proj",
    )(x2, g, wck, wt, qg, kg, iqg, ikg)


def _ordered_bits_to_float(u):
    k = u ^ jnp.int32(-2 ** 31)
    bits = k ^ ((k >> 31) & jnp.int32(0x7FFFFFFF))
    return pltpu.bitcast(bits, F32)


def _dsa_kernel(n_weights, qt_ref, qit_ref, wit_ref, ki_ref, k_ref, ct_ref, *refs):
    w_refs = refs[:n_weights]
    ot_ref = refs[n_weights]
    wo_refs = refs[n_weights + 1:2 * n_weights + 1]
    score_scr, score16_scr, thr_scr, cut_scr, s_scr, m_scr, l_scr, acc_scr = refs[2 * n_weights + 1:]
    for w_ref, wo_ref in zip(w_refs, wo_refs):
        wo_ref[...] = w_ref[...].astype(BF16)
    step = pl.program_id(1)
    n_chunks = step + 1
    cols = A_HEADS * Q_BLOCK
    qpos = step * Q_STEP + lax.broadcasted_iota(jnp.int32, (1, Q_STEP), 1)
    limit = ((qpos >> 6) + 1) << 6

    def block_lanes(g):
        return slice(g * Q_BLOCK, (g + 1) * Q_BLOCK)

    def block_cols(g):
        return slice(g * cols, (g + 1) * cols)

    def key_pos(r0, rows):
        return r0 + lax.broadcasted_iota(jnp.int32, (rows, Q_STEP), 0)

    def for_chunks(fn):
        def run(first_chunk, count):
            for t in range(count):
                fn(pl.multiple_of((first_chunk + t) * KEY_CHUNK, KEY_CHUNK))

        def group(jj, carry):
            run(jj * CHUNKS_PER_LOOP_STEP, CHUNKS_PER_LOOP_STEP)
            return carry
        lax.fori_loop(0, n_chunks // CHUNKS_PER_LOOP_STEP, group, 0)
        done = (n_chunks // CHUNKS_PER_LOOP_STEP) * CHUNKS_PER_LOOP_STEP
        size = CHUNKS_PER_LOOP_STEP // 2
        while size >= 1:
            rest = n_chunks - done
            pl.when(rest >= size)(functools.partial(run, done, size))
            done = done + jnp.where(rest >= size, size, 0)
            size //= 2

    def count(n, pred_fn):
        acc = None
        for j in range(n):
            r0 = j * KEY_CHUNK
            hit = pred_fn(score_scr[r0:r0 + KEY_CHUNK, :], key_pos(r0, KEY_CHUNK))
            part = jnp.sum(hit.reshape(KEY_CHUNK // COUNT_ROWS, COUNT_ROWS, Q_STEP), axis=0)
            acc = part if acc is None else acc + part
        return jnp.sum(acc, axis=0, keepdims=True)

    thr_scr[...] = jnp.full(thr_scr.shape, -jnp.inf, F32)
    cut_scr[...] = jnp.full(cut_scr.shape, 2 * SEQ, jnp.int32)

    @pl.when(n_chunks * KEY_CHUNK <= TOPK)
    def _():
        score_scr[0:KEY_CHUNK, :] = jnp.zeros((KEY_CHUNK, Q_STEP), F32)

    @pl.when(n_chunks * KEY_CHUNK > TOPK)
    def _():
        def score_chunk(r0):
            kic = ki_ref[pl.ds(r0, KEY_CHUNK), :]
            acc = jnp.zeros((KEY_CHUNK, Q_STEP), F32)
            for h in range(IDX_HEADS):
                rel = _dot(kic, qit_ref[h * IDX_DIM:(h + 1) * IDX_DIM, :])
                acc = acc + jnp.maximum(rel, 0.0) * wit_ref[h:h + 1, :]
            masked = jnp.where(key_pos(r0, KEY_CHUNK) < limit, acc, -jnp.inf)
            score_scr[pl.ds(r0, KEY_CHUNK), :] = masked
            score16_scr[pl.ds(r0, KEY_CHUNK), :] = masked.astype(BF16)
        for_chunks(score_chunk)

    def count_coarse(n, cand):
        one = jnp.ones((), BF16)
        zero = jnp.zeros((), BF16)
        accs = [None, None]
        for j in range(n):
            r0 = j * KEY_CHUNK
            hit = jnp.where(score16_scr[r0:r0 + KEY_CHUNK, :] >= cand, one, zero)
            for i, r in enumerate(range(0, KEY_CHUNK, COUNT_ROWS)):
                part = hit[r:r + COUNT_ROWS, :]
                accs[i % 2] = part if accs[i % 2] is None else accs[i % 2] + part
        return jnp.sum(accs[0].astype(F32) + accs[1].astype(F32), axis=0, keepdims=True)

    def search(n):
        def coarse_body(b, u):
            cand = u | (jnp.int32(1) << (31 - b))
            n_ge = count_coarse(n, _ordered_bits_to_float(cand).astype(BF16))
            return jnp.where(n_ge >= float(TOPK), cand, u)
        u16 = lax.fori_loop(0, COARSE_BITS, coarse_body, jnp.zeros((1, Q_STEP), jnp.int32))

        base = u16 - jnp.int32(1 << (31 - COARSE_BITS))

        def fine_body(b, d):
            cand = d | (jnp.int32(1) << (FINE_BITS - 1 - b))
            cf = _ordered_bits_to_float(base + cand)
            n_ge = count(n, lambda s, _: jnp.where(s >= cf, 1.0, 0.0))
            return jnp.where(n_ge >= float(TOPK), cand, d)
        d = lax.fori_loop(0, FINE_BITS, fine_body, jnp.zeros((1, Q_STEP), jnp.int32))
        thr = _ordered_bits_to_float(base + d)
        thr_scr[...] = jnp.broadcast_to(thr, thr_scr.shape)

        excess = count(n, lambda s, _: jnp.where(s >= thr, 1.0, 0.0)) - float(TOPK)
        max_excess = jnp.max(excess).astype(jnp.int32)

        @pl.when(jnp.logical_and(max_excess > 0, max_excess <= INDEX_BITS))
        def _():
            def drop_body(i, cut):
                top = None
                for j in range(n):
                    r0 = j * KEY_CHUNK
                    kp = key_pos(r0, KEY_CHUNK)
                    tied = jnp.where(score_scr[r0:r0 + KEY_CHUNK, :] == thr, jnp.where(kp < cut, kp, -1), -1)
                    part = jnp.max(tied.reshape(KEY_CHUNK // COUNT_ROWS, COUNT_ROWS, Q_STEP), axis=0)
                    top = part if top is None else jnp.maximum(top, part)
                top = jnp.max(top, axis=0, keepdims=True)
                return jnp.where(excess > i.astype(F32), top, cut)
            cut = lax.fori_loop(0, max_excess, drop_body, jnp.full((1, Q_STEP), 2 * SEQ, jnp.int32))
            cut_scr[...] = jnp.broadcast_to(cut, cut_scr.shape)

        @pl.when(max_excess > INDEX_BITS)
        def _():
            n_gt = count(n, lambda s, _: jnp.where(s > thr, 1.0, 0.0))
            need = float(TOPK) - n_gt

            def cut_body(b, cut):
                cand = cut | (jnp.int32(1) << (INDEX_BITS - 1 - b))
                n_eq = count(n, lambda s, kp: jnp.where(s == thr, jnp.where(kp < cand, 1.0, 0.0), 0.0))
                return jnp.where(n_eq <= need, cand, cut)
            cut = lax.fori_loop(0, INDEX_BITS, cut_body, jnp.zeros((1, Q_STEP), jnp.int32))
            cut_scr[...] = jnp.broadcast_to(cut, cut_scr.shape)

    for n in range(TOPK // KEY_CHUNK + 1, SEQ // KEY_CHUNK + 1):
        pl.when(n_chunks == n)(functools.partial(search, n))

    thr = thr_scr[0:1, :]
    cut = cut_scr[0:1, :]
    eye = (lax.broadcasted_iota(jnp.int32, (Q_BLOCK, Q_BLOCK), 0)
           == lax.broadcasted_iota(jnp.int32, (Q_BLOCK, Q_BLOCK), 1))
    eye = jnp.concatenate([jnp.where(eye, 1.0, 0.0).astype(BF16)] * A_HEADS, axis=1)
    qt_aug = []
    for g in range(Q_GROUPS):
        qt = jnp.concatenate([qt_ref[h * A_LAT:(h + 1) * A_LAT, block_lanes(g)] for h in range(A_HEADS)], axis=1)
        qt_aug.append(jnp.concatenate([qt, eye], axis=0))

    def row_groups(a):
        return a.reshape(KEY_CHUNK // SUBLANES, SUBLANES, a.shape[-1])

    m_scr[...] = jnp.full(m_scr.shape, -jnp.inf, F32)

    def logits_chunk(r0):
        sc = score_scr[pl.ds(r0, KEY_CHUNK), :]
        kp = key_pos(r0, KEY_CHUNK)
        tie_bias = jnp.where(sc == thr, jnp.where(kp < cut, 0.0, MASK_NEG), MASK_NEG)
        bias = jnp.where(kp < limit, jnp.where(sc > thr, 0.0, tie_bias), MASK_NEG).astype(BF16)
        kc = k_ref[pl.ds(r0, KEY_CHUNK), :]
        for g in range(Q_GROUPS):
            st = _dot(jnp.concatenate([kc, bias[:, block_lanes(g)]], axis=1), qt_aug[g])
            s_scr[pl.ds(r0, KEY_CHUNK), block_cols(g)] = st
            m_scr[:, block_cols(g)] = jnp.maximum(m_scr[:, block_cols(g)], jnp.max(row_groups(st), axis=0))
    for_chunks(logits_chunk)
    m = jnp.max(m_scr[...], axis=0, keepdims=True)

    l_scr[...] = jnp.zeros(l_scr.shape, F32)
    acc_scr[...] = jnp.zeros(acc_scr.shape, F32)

    def pv_chunk(r0):
        p = jnp.exp2(s_scr[pl.ds(r0, KEY_CHUNK), :] - m)
        l_scr[...] += jnp.sum(row_groups(p), axis=0)
        acc_scr[...] += _dot(ct_ref[:, pl.ds(r0, KEY_CHUNK)], p.astype(BF16))
    for_chunks(pv_chunk)

    o = acc_scr[...] / jnp.sum(l_scr[...], axis=0, keepdims=True)
    for g in range(Q_GROUPS):
        for h in range(A_HEADS):
            c0 = g * cols + h * Q_BLOCK
            ot_ref[h * A_LAT:(h + 1) * A_LAT, block_lanes(g)] = o[:, c0:c0 + Q_BLOCK].astype(BF16)


def _dsa(qt, qit, wit, ki, k, ct, weights):
    assert Q_STEP == KEY_CHUNK
    nq = SEQ // Q_STEP
    cols = Q_GROUPS * A_HEADS * Q_BLOCK
    steps = BATCH * nq
    views = [w.reshape(-1, w.shape[-1]) for w in weights]

    def slab_spec(v):
        n_slabs = max(d for d in range(1, steps + 1)
                      if steps % d == 0 and v.shape[0] % (d * BF16_TILE_ROWS) == 0)
        per_slab = steps // n_slabs
        return pl.BlockSpec((v.shape[0] // n_slabs, v.shape[1]), lambda b, i: ((b * nq + i) // per_slab, 0))
    outs = pl.pallas_call(
        functools.partial(_dsa_kernel, len(views)),
        grid=(BATCH, nq),
        in_specs=[
            pl.BlockSpec((A_Q_COLS, Q_STEP), lambda b, i: (0, b * nq + i)),
            pl.BlockSpec((IDX_HEADS * IDX_DIM, Q_STEP), lambda b, i: (0, b * nq + i)),
            pl.BlockSpec((IDX_HEADS, Q_STEP), lambda b, i: (0, b * nq + i)),
            pl.BlockSpec((SEQ, IDX_DIM), lambda b, i: (b, 0)),
            pl.BlockSpec((SEQ, A_LAT), lambda b, i: (b, 0)),
            pl.BlockSpec((A_LAT, SEQ), lambda b, i: (0, b)),
        ] + [slab_spec(v) for v in views],
        out_specs=[pl.BlockSpec((A_Q_COLS, Q_STEP), lambda b, i: (0, b * nq + i))] + [slab_spec(v) for v in views],
        out_shape=[jax.ShapeDtypeStruct((A_Q_COLS, TOKENS), BF16)]
        + [jax.ShapeDtypeStruct(v.shape, BF16) for v in views],
        scratch_shapes=[
            pltpu.VMEM((SEQ, Q_STEP), F32),
            pltpu.VMEM((SEQ, Q_STEP), BF16),
            pltpu.VMEM((SUBLANES, Q_STEP), F32),
            pltpu.VMEM((SUBLANES, Q_STEP), jnp.int32),
            pltpu.VMEM((SEQ, cols), F32),
            pltpu.VMEM((SUBLANES, cols), F32),
            pltpu.VMEM((SUBLANES, cols), F32),
            pltpu.VMEM((A_LAT, cols), F32),
        ],
        compiler_params=_params("parallel", "arbitrary"),
        name="dsa_core",
    )(qt, qit, wit, ki, k, ct, *views)
    return outs[0], [o.reshape(w.shape) for o, w in zip(outs[1:], weights)]


SEG = ROW_TILE // SUBLANES
HALO = (CONV_W - 1) * SUBLANES


def _interleave_rows(a):
    return jnp.transpose(a.reshape(SUBLANES, SEG, a.shape[-1]), (1, 0, 2)).reshape(a.shape)


def _deinterleave_rows(a):
    return jnp.transpose(a.reshape(SEG, SUBLANES, a.shape[-1]), (1, 0, 2)).reshape(a.shape)


def _ffn_kernel(m_transposed, x_ref, m_ref, wout_ref, g_ref, wup_ref, cw_ref, cb_ref, wdown_ref, o_ref,
                hb_scr, u_even, u_odd, gate_scr, carry_scr):
    t = pl.program_id(0)
    first_of_sequence = (t % (SEQ // ROW_TILE)) == 0
    mixer = (_dot_tn if m_transposed else _dot)(m_ref[...], wout_ref[...])
    x1 = x_ref[...] + mixer
    o_ref[...] = x1
    hb_scr[...] = _interleave_rows(_rms(x1, g_ref[...])).astype(BF16)
    first_sublane = lax.broadcasted_iota(jnp.int32, (HALO, 2 * FF_CHUNK), 0) % SUBLANES == 0

    def chunk_cols(ref, c):
        if isinstance(c, int):
            a0, b0 = c * FF_CHUNK, D_FF + c * FF_CHUNK
        else:
            a0 = pl.multiple_of(c * FF_CHUNK, FF_CHUNK)
            b0 = pl.multiple_of(D_FF + c * FF_CHUNK, FF_CHUNK)
        return ref[:, pl.ds(a0, FF_CHUNK)], ref[:, pl.ds(b0, FF_CHUNK)]

    def up(c):
        hb = hb_scr[...]
        wa, wb = chunk_cols(wup_ref, c)
        return jnp.concatenate([_dot(hb, wa), _dot(hb, wb)], axis=1)

    def down_one_sublane(a):
        return jnp.concatenate([pltpu.roll(a[r:r + SUBLANES, :], 1, 0)
                                for r in range(0, a.shape[0], SUBLANES)], axis=0)

    def act(c, u):
        tail = u[ROW_TILE - HALO:, :]
        prev = jnp.where(first_of_sequence, 0.0, carry_scr[c])
        carry_scr[c] = tail
        wrap = jnp.where(first_sublane, down_one_sublane(prev), down_one_sublane(tail))
        cw = jnp.concatenate(chunk_cols(cw_ref, c), axis=1)
        cb = jnp.concatenate(chunk_cols(cb_ref, c), axis=1)
        v = u * cw[CONV_W - 1:CONV_W, :] + cb
        for d in range(1, CONV_W):
            shifted = jnp.concatenate([wrap[HALO - d * SUBLANES:, :], u[:ROW_TILE - d * SUBLANES, :]], axis=0)
            v = v + shifted * cw[CONV_W - 1 - d:CONV_W - d, :]
        a = v[:, :FF_CHUNK]
        b = v[:, FF_CHUNK:]
        return (a * jax.nn.sigmoid(a) * b).astype(BF16)

    def stage(c, u_cur, u_next, has_up=True):
        if has_up:
            u_next[...] = up(c + 1)
        g0 = c * FF_CHUNK if isinstance(c, int) else pl.multiple_of(c * FF_CHUNK, FF_CHUNK)
        gate_scr[:, pl.ds(g0, FF_CHUNK)] = act(c, u_cur[...])

    def even_stage(c, **kw):
        stage(c, u_even, u_odd, **kw)

    def odd_stage(c, **kw):
        stage(c, u_odd, u_even, **kw)

    u_even[...] = up(0)
    for c in range(N_FF_CHUNKS):
        (even_stage if c % 2 == 0 else odd_stage)(c, has_up=c + 1 < N_FF_CHUNKS)
    o_ref[...] += _deinterleave_rows(_dot(gate_scr[...], wdown_ref[...]))


def _layer_spec(shape, layer):
    nd = len(shape)
    return pl.BlockSpec((None,) + tuple(shape), lambda *_: (layer,) + (0,) * nd, pipeline_mode=pl.Buffered(1))


def _ffn(x2, m, m_transposed, wout, g, layer, wup, cw, cb, wdown):
    assert N_FF_CHUNKS % 2 == 1 and N_FF_CHUNKS >= 3
    n = TOKENS // ROW_TILE
    km = wout.shape[0]
    m_spec = (pl.BlockSpec((km, ROW_TILE), lambda i: (0, i)) if m_transposed
              else pl.BlockSpec((ROW_TILE, km), lambda i: (i, 0)))
    return pl.pallas_call(
        functools.partial(_ffn_kernel, m_transposed),
        grid=(n,),
        in_specs=[
            pl.BlockSpec((ROW_TILE, D_MODEL), lambda i: (i, 0)),
            m_spec,
            _const_spec((km, D_MODEL)),
            _const_spec((1, D_MODEL)),
            _layer_spec((D_MODEL, 2 * D_FF), layer),
            _const_spec((CONV_W, 2 * D_FF)),
            _const_spec((1, 2 * D_FF)),
            _layer_spec((D_FF, D_MODEL), layer),
        ],
        out_specs=pl.BlockSpec((ROW_TILE, D_MODEL), lambda i: (i, 0)),
        out_shape=jax.ShapeDtypeStruct((TOKENS, D_MODEL), F32),
        scratch_shapes=[
            pltpu.VMEM((ROW_TILE, D_MODEL), BF16),
            pltpu.VMEM((ROW_TILE, 2 * FF_CHUNK), F32),
            pltpu.VMEM((ROW_TILE, 2 * FF_CHUNK), F32),
            pltpu.VMEM((ROW_TILE, D_FF), BF16),
            pltpu.VMEM((N_FF_CHUNKS, HALO, 2 * FF_CHUNK), F32),
        ],
        compiler_params=_params("arbitrary"),
        name="conv_ffn",
    )(x2, m, wout, g, wup, cw, cb, wdown)


def _proj_b_kernel(x_ref, g_ref, w_ref, cos_ref, sin_ref, zeta_ref, q_out, k_out, kz_out, v_out, sg_out):
    row_halves = (slice(0, ROW_TILE // 2), slice(ROW_TILE // 2, ROW_TILE))
    hb_halves = [_rms(x_ref[rows, :], g_ref[...]).astype(BF16) for rows in row_halves]
    hb = jnp.concatenate(hb_halves, axis=0)
    cos = cos_ref[...]
    sin = sin_ref[...]
    half = B_DK // 2

    def rotary(z):
        z1, z2 = z[:, :half], z[:, half:]
        return z1 * cos - z2 * sin, z1 * sin + z2 * cos

    dq = B_HEADS * B_DK
    dv = B_HEADS * B_DV
    for h in range(B_HEADS):
        for rows, hb_half in zip(row_halves, hb_halves):
            gate = _dot(hb_half, w_ref[:, 2 * dq + dv + h * B_DV:2 * dq + dv + (h + 1) * B_DV])
            sg_out[rows, h * B_DV:(h + 1) * B_DV] = (gate * jax.nn.sigmoid(gate)).astype(BF16)
    for h in range(B_HEADS):
        zq = _dot(hb, w_ref[:, h * B_DK:(h + 1) * B_DK])
        for part, r in enumerate(rotary(zq)):
            cols = slice(h * B_DK + part * half, h * B_DK + (part + 1) * half)
            q_out[:, cols] = r.astype(BF16)
        zk = _dot(hb, w_ref[:, dq + h * B_DK:dq + (h + 1) * B_DK])
        for part, r in enumerate(rotary(zk)):
            cols = slice(h * B_DK + part * half, h * B_DK + (part + 1) * half)
            kr = r * (B_DK ** -0.5)
            k_out[:, cols] = kr.astype(BF16)
            kz_out[:, cols] = (kr * zeta_ref[:, cols]).astype(BF16)
    for h in range(B_HEADS):
        v_out[:, h * B_DV:(h + 1) * B_DV] = _dot(
            hb, w_ref[:, 2 * dq + h * B_DV:2 * dq + (h + 1) * B_DV]).astype(BF16)


def _proj_b(x2, g, w, cos, sin, zeta_rows):
    n = TOKENS // ROW_TILE
    per_seq = SEQ // ROW_TILE
    dq = B_HEADS * B_DK
    dv = B_HEADS * B_DV
    return pl.pallas_call(
        _proj_b_kernel,
        grid=(n,),
        in_specs=[
            pl.BlockSpec((ROW_TILE, D_MODEL), lambda i: (i, 0)),
            _const_spec((1, D_MODEL)),
            _const_spec((D_MODEL, 2 * dq + 2 * dv)),
            pl.BlockSpec((ROW_TILE, B_DK // 2), lambda i: (i % per_seq, 0)),
            pl.BlockSpec((ROW_TILE, B_DK // 2), lambda i: (i % per_seq, 0)),
            _const_spec((ROW_TILE, dq)),
        ],
        out_specs=[
            pl.BlockSpec((ROW_TILE, dq), lambda i: (i, 0)),
            pl.BlockSpec((ROW_TILE, dq), lambda i: (i, 0)),
            pl.BlockSpec((ROW_TILE, dq), lambda i: (i, 0)),
            pl.BlockSpec((ROW_TILE, dv), lambda i: (i, 0)),
            pl.BlockSpec((ROW_TILE, dv), lambda i: (i, 0)),
        ],
        out_shape=[
            jax.ShapeDtypeStruct((TOKENS, dq), BF16),
            jax.ShapeDtypeStruct((TOKENS, dq), BF16),
            jax.ShapeDtypeStruct((TOKENS, dq), BF16),
            jax.ShapeDtypeStruct((TOKENS, dv), BF16),
            jax.ShapeDtypeStruct((TOKENS, dv), BF16),
        ],
        compiler_params=_params("parallel"),
        name="ret_in_proj",
    )(x2, g, w, cos, sin, zeta_rows)


RET_HEADS_PER_STEP = 2


def _retention_kernel(q_ref, k_ref, kz_ref, v_ref, sg_ref, decay_ref, xi_ref, gc_ref, og_ref,
                      y_ref, state_scr):
    state_scr[...] = jnp.zeros(state_scr.shape, F32)

    for n in range(SEQ // RET_CHUNK):
        r0 = n * RET_CHUNK
        for hh in range(RET_HEADS_PER_STEP):
            kcols = slice(hh * B_DK, (hh + 1) * B_DK)
            vcols = slice(hh * B_DV, (hh + 1) * B_DV)
            qc = q_ref[pl.ds(r0, RET_CHUNK), kcols]
            vc = v_ref[pl.ds(r0, RET_CHUNK), vcols]
            state = state_scr[hh]
            s = (_dot_nt(qc, k_ref[pl.ds(r0, RET_CHUNK), kcols]) * decay_ref[hh]).astype(BF16)
            ret = _dot(s, vc) + _dot(qc, state.astype(BF16)) * xi_ref[hh]
            state_scr[hh] = state * gc_ref[hh] + _dot_tn(kz_ref[pl.ds(r0, RET_CHUNK), kcols], vc)
            y = _rms(ret, og_ref[:, vcols]) * sg_ref[pl.ds(r0, RET_CHUNK), vcols].astype(F32)
            y_ref[pl.ds(r0, RET_CHUNK), vcols] = y.astype(BF16)


def _retention(q, k, kz, v, sg, decay, xi, gc, og):
    hs = RET_HEADS_PER_STEP
    return pl.pallas_call(
        _retention_kernel,
        grid=(BATCH, B_HEADS // hs),
        in_specs=[
            pl.BlockSpec((SEQ, hs * B_DK), lambda b, h: (b, h)),
            pl.BlockSpec((SEQ, hs * B_DK), lambda b, h: (b, h)),
            pl.BlockSpec((SEQ, hs * B_DK), lambda b, h: (b, h)),
            pl.BlockSpec((SEQ, hs * B_DV), lambda b, h: (b, h)),
            pl.BlockSpec((SEQ, hs * B_DV), lambda b, h: (b, h)),
            pl.BlockSpec((hs, RET_CHUNK, RET_CHUNK), lambda b, h: (h, 0, 0)),
            pl.BlockSpec((hs, RET_CHUNK, B_DV), lambda b, h: (h, 0, 0)),
            pl.BlockSpec((hs, 1, B_DV), lambda b, h: (h, 0, 0)),
            pl.BlockSpec((1, hs * B_DV), lambda b, h: (0, h)),
        ],
        out_specs=pl.BlockSpec((SEQ, hs * B_DV), lambda b, h: (b, h)),
        out_shape=jax.ShapeDtypeStruct((TOKENS, B_HEADS * B_DV), BF16),
        scratch_shapes=[pltpu.VMEM((hs, B_DK, B_DV), F32)],
        compiler_params=_params("parallel", "parallel"),
        name="retention",
    )(q, k, kz, v, sg, decay, xi, gc, og)


def _retention_tables():
    log_gamma = np.log1p(-(2.0 ** (-5.0 - np.arange(B_HEADS, dtype=np.float32)))).astype(np.float32)
    pos = np.arange(RET_CHUNK, dtype=np.float32)
    diff = pos[:, None] - pos[None, :]
    decay = np.where(diff[None] >= 0,
                     np.exp(np.maximum(diff, 0.0)[None] * log_gamma[:, None, None]), 0.0)
    xi = np.exp((pos + 1.0)[None, :] * log_gamma[:, None])
    zeta = np.exp((RET_CHUNK - 1.0 - pos)[None, :] * log_gamma[:, None])
    gc = np.exp(RET_CHUNK * log_gamma)
    xi_b = np.broadcast_to(xi[:, :, None], (B_HEADS, RET_CHUNK, B_DV))
    gc_b = np.broadcast_to(gc[:, None, None], (B_HEADS, 1, B_DV))
    zeta_rows = np.tile(np.repeat(zeta.T, B_DK, axis=1), (ROW_TILE // RET_CHUNK, 1))
    return tuple(jnp.asarray(a, F32) for a in (decay, xi_b, gc_b, zeta_rows))


def _rotary_tables():
    inv = (1.0 / (ROPE_BASE ** (np.arange(0, B_DK, 2, dtype=np.float32) / B_DK))).astype(np.float32)
    ang = np.arange(SEQ, dtype=np.float32)[:, None] * inv[None, :]
    return jnp.asarray(np.cos(ang), F32), jnp.asarray(np.sin(ang), F32)


def kernel(x, norm_mix_g, norm_ffn_g, a_w_in, a_q_g, a_k_g, a_iq_g, a_ik_g, a_w_out,
           b_w_in, b_out_g, b_w_out, f_w_up, f_conv_w, f_conv_b, f_w_down):
    x2 = x.reshape(TOKENS, D_MODEL)

    w = a_w_in[0]
    o1 = A_Q_COLS
    o2 = o1 + A_LAT
    o3 = o2 + IDX_HEADS * IDX_DIM
    o4 = o3 + IDX_DIM
    wck = jnp.concatenate(
        [w[:, o1:o2], w[:, o3:o4], jnp.zeros((D_MODEL, A_LAT - IDX_DIM), F32)], axis=1).astype(BF16)
    wt = jnp.concatenate([w[:, :o3], w[:, o4:]], axis=1).T.astype(BF16)
    qt, k, ct, qit, ki, wit = _proj_a(
        x2, norm_mix_g[0][None, :], wck, wt,
        a_q_g[0][:, None], a_k_g[0][None, :], a_iq_g[0][:, None], a_ik_g[0][None, :])
    ot, (w_up, w_down, wa_out, wb_in, wb_out) = _dsa(
        qt, qit, wit, ki, k, ct, [f_w_up, f_w_down, a_w_out[0], b_w_in[0], b_w_out[0]])
    x2 = _ffn(x2, ot, True, wa_out, norm_ffn_g[0][None, :],
              0, w_up, f_conv_w[0], f_conv_b[0][None, :], w_down)

    cos, sin = _rotary_tables()
    decay, xi, gc, zeta_rows = _retention_tables()
    q, kk, kz, v, sg = _proj_b(x2, norm_mix_g[1][None, :], wb_in, cos, sin, zeta_rows)
    y = _retention(q, kk, kz, v, sg, decay, xi, gc, b_out_g[0][None, :])
    x2 = _ffn(x2, y, False, wb_out, norm_ffn_g[1][None, :],
              1, w_up, f_conv_w[1], f_conv_b[1][None, :], w_down)
    return x2.reshape(BATCH, SEQ, D_MODEL)
```

```python
import functools

import jax
import jax.numpy as jnp
import numpy as np
from jax import lax
from jax.experimental import pallas as pl
from jax.experimental.pallas import tpu as pltpu

D_MODEL = 1024
BATCH = 8
SEQ = 2048
TOKENS = BATCH * SEQ
CHUNK = 64
CHUNK_SHIFT = CHUNK.bit_length() - 1
EPS = 1e-6
A_HEADS = 8
A_LAT = 128
IDX_HEADS = 8
IDX_DIM = 64
TOPK = 256
A_Q_COLS = A_HEADS * A_LAT
B_HEADS = 4
B_DK = 256
B_DV = 512
ROPE_BASE = 10000.0
RET_CHUNK = 256
D_FF = 2816
CONV_W = 3
FF_CHUNK = 256
N_FF_CHUNKS = D_FF // FF_CHUNK

LANES = 128
SUBLANES = 8
ROW_TILE = 512
Q_BLOCK = 128
Q_GROUPS = 2
Q_STEP = Q_GROUPS * Q_BLOCK
KEY_CHUNK = 256
CHUNKS_PER_LOOP_STEP = 4
COUNT_ROWS = 32
INDEX_BITS = 12
COARSE_BITS = 16
FINE_BITS = 17
BF16_TILE_ROWS = 16
VMEM_LIMIT = 56 * 1024 * 1024
MASK_NEG = -(2.0 ** 100)
LOG2_E = 1.4426950408889634

BF16 = jnp.bfloat16
F32 = jnp.float32


def _params(*sem):
    return pltpu.CompilerParams(dimension_semantics=sem, vmem_limit_bytes=VMEM_LIMIT)


def _const_spec(shape):
    nd = len(shape)
    return pl.BlockSpec(shape, lambda *_: (0,) * nd, pipeline_mode=pl.Buffered(1))


def _rms(x, g):
    return x * lax.rsqrt(jnp.mean(x * x, axis=-1, keepdims=True) + EPS) * g


def _dot(a, b):
    return jnp.dot(a, b, preferred_element_type=F32)


def _dot_nt(a, b):
    return lax.dot_general(a, b, (((1,), (1,)), ((), ())), preferred_element_type=F32)


def _dot_tn(a, b):
    return lax.dot_general(a, b, (((0,), (0,)), ((), ())), preferred_element_type=F32)


T_Q = 0
T_C = T_Q + A_Q_COLS
T_QI = T_C + A_LAT
T_WI = T_QI + IDX_HEADS * IDX_DIM
T_ROWS = T_WI + IDX_HEADS


def _proj_a_kernel(x_ref, g_ref, wck_ref, wt_ref, qg_ref, kg_ref, iqg_ref, ikg_ref,
                   qt_out, k_out, ct_out, qit_out, ki_out, wit_out):
    def head_norm(rows, gain):
        return rows * lax.rsqrt(jnp.mean(rows * rows, axis=0, keepdims=True) + EPS) * gain

    qg = qg_ref[...] * (A_LAT ** -0.5 * LOG2_E)
    hb = _rms(x_ref[...], g_ref[...]).astype(BF16)
    ck = _dot(hb, wck_ref[...])
    k_out[...] = _rms(ck[:, :A_LAT], kg_ref[...]).astype(BF16)
    ki_out[...] = _rms(ck[:, A_LAT:A_LAT + IDX_DIM], ikg_ref[...]).astype(BF16)

    t = _dot_nt(wt_ref[...], hb)
    for h in range(A_HEADS):
        r0 = T_Q + h * A_LAT
        qt_out[h * A_LAT:(h + 1) * A_LAT, :] = head_norm(t[r0:r0 + A_LAT, :], qg).astype(BF16)
    ct_out[...] = t[T_C:T_C + A_LAT, :].astype(BF16)
    for h in range(IDX_HEADS):
        r0 = T_QI + h * IDX_DIM
        qit_out[h * IDX_DIM:(h + 1) * IDX_DIM, :] = head_norm(t[r0:r0 + IDX_DIM, :], iqg_ref[...]).astype(BF16)
    wit_out[...] = t[T_WI:, :] * (IDX_HEADS ** -0.5 * IDX_DIM ** -0.5)


def _proj_a(x2, g, wck, wt, qg, kg, iqg, ikg):
    n = TOKENS // ROW_TILE
    return pl.pallas_call(
        _proj_a_kernel,
        grid=(n,),
        in_specs=[
            pl.BlockSpec((ROW_TILE, D_MODEL), lambda i: (i, 0)),
            _const_spec((1, D_MODEL)),
            _const_spec((D_MODEL, 2 * A_LAT)),
            _const_spec((T_ROWS, D_MODEL)),
            _const_spec((A_LAT, 1)),
            _const_spec((1, A_LAT)),
            _const_spec((IDX_DIM, 1)),
            _const_spec((1, IDX_DIM)),
        ],
        out_specs=[
            pl.BlockSpec((A_Q_COLS, ROW_TILE), lambda i: (0, i)),
            pl.BlockSpec((ROW_TILE, A_LAT), lambda i: (i, 0)),
            pl.BlockSpec((A_LAT, ROW_TILE), lambda i: (0, i)),
            pl.BlockSpec((IDX_HEADS * IDX_DIM, ROW_TILE), lambda i: (0, i)),
            pl.BlockSpec((ROW_TILE, IDX_DIM), lambda i: (i, 0)),
            pl.BlockSpec((IDX_HEADS, ROW_TILE), lambda i: (0, i)),
        ],
        out_shape=[
            jax.ShapeDtypeStruct((A_Q_COLS, TOKENS), BF16),
            jax.ShapeDtypeStruct((TOKENS, A_LAT), BF16),
            jax.ShapeDtypeStruct((A_LAT, TOKENS), BF16),
            jax.ShapeDtypeStruct((IDX_HEADS * IDX_DIM, TOKENS), BF16),
            jax.ShapeDtypeStruct((TOKENS, IDX_DIM), BF16),
            jax.ShapeDtypeStruct((IDX_HEADS, TOKENS), F32),
        ],
        compiler_params=_params("parallel"),
        name="dsa_in_proj",
    )(x2, g, wck, wt, qg, kg, iqg, ikg)


def _ordered_bits_to_float(u):
    k = u ^ jnp.int32(-2 ** 31)
    bits = k ^ ((k >> 31) & jnp.int32(0x7FFFFFFF))
    return pltpu.bitcast(bits, F32)


def _dsa_kernel(n_weights, qt_ref, qit_ref, wit_ref, ki_ref, k_ref, ct_ref, *refs):
    w_refs = refs[:n_weights]
    ot_ref = refs[n_weights]
    wo_refs = refs[n_weights + 1:2 * n_weights + 1]
    score_scr, score16_scr, thr_scr, cut_scr, s_scr, m_scr, l_scr, acc_scr = refs[2 * n_weights + 1:]
    for w_ref, wo_ref in zip(w_refs, wo_refs):
        wo_ref[...] = w_ref[...].astype(BF16)
    step = pl.program_id(1)
    n_chunks = step + 1
    cols = A_HEADS * Q_BLOCK
    qpos = step * Q_STEP + lax.broadcasted_iota(jnp.int32, (1, Q_STEP), 1)
    limit = ((qpos >> CHUNK_SHIFT) + 1) << CHUNK_SHIFT

    def block_lanes(g):
        return slice(g * Q_BLOCK, (g + 1) * Q_BLOCK)

    def block_cols(g):
        return slice(g * cols, (g + 1) * cols)

    def key_pos(r0, rows):
        return r0 + lax.broadcasted_iota(jnp.int32, (rows, Q_STEP), 0)

    def for_chunks(fn):
        def run(first_chunk, count):
            for t in range(count):
                fn(pl.multiple_of((first_chunk + t) * KEY_CHUNK, KEY_CHUNK))

        def group(jj, carry):
            run(jj * CHUNKS_PER_LOOP_STEP, CHUNKS_PER_LOOP_STEP)
            return carry
        lax.fori_loop(0, n_chunks // CHUNKS_PER_LOOP_STEP, group, 0)
        done = (n_chunks // CHUNKS_PER_LOOP_STEP) * CHUNKS_PER_LOOP_STEP
        size = CHUNKS_PER_LOOP_STEP // 2
        while size >= 1:
            rest = n_chunks - done
            pl.when(rest >= size)(functools.partial(run, done, size))
            done = done + jnp.where(rest >= size, size, 0)
            size //= 2

    def count(n, pred_fn):
        acc = None
        for j in range(n):
            r0 = j * KEY_CHUNK
            hit = pred_fn(score_scr[r0:r0 + KEY_CHUNK, :], key_pos(r0, KEY_CHUNK))
            part = jnp.sum(hit.reshape(KEY_CHUNK // COUNT_ROWS, COUNT_ROWS, Q_STEP), axis=0)
            acc = part if acc is None else acc + part
        return jnp.sum(acc, axis=0, keepdims=True)

    thr_scr[...] = jnp.full(thr_scr.shape, -jnp.inf, F32)
    cut_scr[...] = jnp.full(cut_scr.shape, 2 * SEQ, jnp.int32)

    @pl.when(n_chunks * KEY_CHUNK <= TOPK)
    def _():
        score_scr[0:KEY_CHUNK, :] = jnp.zeros((KEY_CHUNK, Q_STEP), F32)

    @pl.when(n_chunks * KEY_CHUNK > TOPK)
    def _():
        def score_chunk(r0):
            kic = ki_ref[pl.ds(r0, KEY_CHUNK), :]
            acc = jnp.zeros((KEY_CHUNK, Q_STEP), F32)
            for h in range(IDX_HEADS):
                rel = _dot(kic, qit_ref[h * IDX_DIM:(h + 1) * IDX_DIM, :])
                acc = acc + jnp.maximum(rel, 0.0) * wit_ref[h:h + 1, :]
            masked = jnp.where(key_pos(r0, KEY_CHUNK) < limit, acc, -jnp.inf)
            score_scr[pl.ds(r0, KEY_CHUNK), :] = masked
            score16_scr[pl.ds(r0, KEY_CHUNK), :] = masked.astype(BF16)
        for_chunks(score_chunk)

    def count_coarse(n, cand):
        one = jnp.ones((), BF16)
        zero = jnp.zeros((), BF16)
        accs = [None, None]
        for j in range(n):
            r0 = j * KEY_CHUNK
            hit = jnp.where(score16_scr[r0:r0 + KEY_CHUNK, :] >= cand, one, zero)
            for i, r in enumerate(range(0, KEY_CHUNK, COUNT_ROWS)):
                part = hit[r:r + COUNT_ROWS, :]
                accs[i % 2] = part if accs[i % 2] is None else accs[i % 2] + part
        return jnp.sum(accs[0].astype(F32) + accs[1].astype(F32), axis=0, keepdims=True)

    def search(n):
        def coarse_body(b, u):
            cand = u | (jnp.int32(1) << (31 - b))
            n_ge = count_coarse(n, _ordered_bits_to_float(cand).astype(BF16))
            return jnp.where(n_ge >= float(TOPK), cand, u)
        u16 = lax.fori_loop(0, COARSE_BITS, coarse_body, jnp.zeros((1, Q_STEP), jnp.int32))

        base = u16 - jnp.int32(1 << (31 - COARSE_BITS))

        def fine_body(b, d):
            cand = d | (jnp.int32(1) << (FINE_BITS - 1 - b))
            cf = _ordered_bits_to_float(base + cand)
            n_ge = count(n, lambda s, _: jnp.where(s >= cf, 1.0, 0.0))
            return jnp.where(n_ge >= float(TOPK), cand, d)
        d = lax.fori_loop(0, FINE_BITS, fine_body, jnp.zeros((1, Q_STEP), jnp.int32))
        thr = _ordered_bits_to_float(base + d)
        thr_scr[...] = jnp.broadcast_to(thr, thr_scr.shape)

        excess = count(n, lambda s, _: jnp.where(s >= thr, 1.0, 0.0)) - float(TOPK)
        max_excess = jnp.max(excess).astype(jnp.int32)

        @pl.when(jnp.logical_and(max_excess > 0, max_excess <= INDEX_BITS))
        def _():
            def drop_body(i, cut):
                top = None
                for j in range(n):
                    r0 = j * KEY_CHUNK
                    kp = key_pos(r0, KEY_CHUNK)
                    tied = jnp.where(score_scr[r0:r0 + KEY_CHUNK, :] == thr, jnp.where(kp < cut, kp, -1), -1)
                    part = jnp.max(tied.reshape(KEY_CHUNK // COUNT_ROWS, COUNT_ROWS, Q_STEP), axis=0)
                    top = part if top is None else jnp.maximum(top, part)
                top = jnp.max(top, axis=0, keepdims=True)
                return jnp.where(excess > i.astype(F32), top, cut)
            cut = lax.fori_loop(0, max_excess, drop_body, jnp.full((1, Q_STEP), 2 * SEQ, jnp.int32))
            cut_scr[...] = jnp.broadcast_to(cut, cut_scr.shape)

        @pl.when(max_excess > INDEX_BITS)
        def _():
            n_gt = count(n, lambda s, _: jnp.where(s > thr, 1.0, 0.0))
            need = float(TOPK) - n_gt

            def cut_body(b, cut):
                cand = cut | (jnp.int32(1) << (INDEX_BITS - 1 - b))
                n_eq = count(n, lambda s, kp: jnp.where(s == thr, jnp.where(kp < cand, 1.0, 0.0), 0.0))
                return jnp.where(n_eq <= need, cand, cut)
            cut = lax.fori_loop(0, INDEX_BITS, cut_body, jnp.zeros((1, Q_STEP), jnp.int32))
            cut_scr[...] = jnp.broadcast_to(cut, cut_scr.shape)

    for n in range(TOPK // KEY_CHUNK + 1, SEQ // KEY_CHUNK + 1):
        pl.when(n_chunks == n)(functools.partial(search, n))

    thr = thr_scr[0:1, :]
    cut = cut_scr[0:1, :]
    eye = (lax.broadcasted_iota(jnp.int32, (Q_BLOCK, Q_BLOCK), 0)
           == lax.broadcasted_iota(jnp.int32, (Q_BLOCK, Q_BLOCK), 1))
    eye = jnp.concatenate([jnp.where(eye, 1.0, 0.0).astype(BF16)] * A_HEADS, axis=1)
    qt_aug = []
    for g in range(Q_GROUPS):
        qt = jnp.concatenate([qt_ref[h * A_LAT:(h + 1) * A_LAT, block_lanes(g)] for h in range(A_HEADS)], axis=1)
        qt_aug.append(jnp.concatenate([qt, eye], axis=0))

    def row_groups(a):
        return a.reshape(KEY_CHUNK // SUBLANES, SUBLANES, a.shape[-1])

    m_scr[...] = jnp.full(m_scr.shape, -jnp.inf, F32)

    def logits_chunk(r0):
        sc = score_scr[pl.ds(r0, KEY_CHUNK), :]
        kp = key_pos(r0, KEY_CHUNK)
        tie_bias = jnp.where(sc == thr, jnp.where(kp < cut, 0.0, MASK_NEG), MASK_NEG)
        bias = jnp.where(kp < limit, jnp.where(sc > thr, 0.0, tie_bias), MASK_NEG).astype(BF16)
        kc = k_ref[pl.ds(r0, KEY_CHUNK), :]
        for g in range(Q_GROUPS):
            st = _dot(jnp.concatenate([kc, bias[:, block_lanes(g)]], axis=1), qt_aug[g])
            s_scr[pl.ds(r0, KEY_CHUNK), block_cols(g)] = st
            m_scr[:, block_cols(g)] = jnp.maximum(m_scr[:, block_cols(g)], jnp.max(row_groups(st), axis=0))
    for_chunks(logits_chunk)
    m = jnp.max(m_scr[...], axis=0, keepdims=True)

    l_scr[...] = jnp.zeros(l_scr.shape, F32)
    acc_scr[...] = jnp.zeros(acc_scr.shape, F32)

    def pv_chunk(r0):
        p = jnp.exp2(s_scr[pl.ds(r0, KEY_CHUNK), :] - m)
        l_scr[...] += jnp.sum(row_groups(p), axis=0)
        acc_scr[...] += _dot(ct_ref[:, pl.ds(r0, KEY_CHUNK)], p.astype(BF16))
    for_chunks(pv_chunk)

    o = acc_scr[...] / jnp.sum(l_scr[...], axis=0, keepdims=True)
    for g in range(Q_GROUPS):
        for h in range(A_HEADS):
            c0 = g * cols + h * Q_BLOCK
            ot_ref[h * A_LAT:(h + 1) * A_LAT, block_lanes(g)] = o[:, c0:c0 + Q_BLOCK].astype(BF16)


def _dsa(qt, qit, wit, ki, k, ct, weights):
    assert Q_STEP == KEY_CHUNK
    nq = SEQ // Q_STEP
    cols = Q_GROUPS * A_HEADS * Q_BLOCK
    steps = BATCH * nq
    views = [w.reshape(-1, w.shape[-1]) for w in weights]

    def slab_spec(v):
        n_slabs = max(d for d in range(1, steps + 1)
                      if steps % d == 0 and v.shape[0] % (d * BF16_TILE_ROWS) == 0)
        per_slab = steps // n_slabs
        return pl.BlockSpec((v.shape[0] // n_slabs, v.shape[1]), lambda b, i: ((b * nq + i) // per_slab, 0))
    outs = pl.pallas_call(
        functools.partial(_dsa_kernel, len(views)),
        grid=(BATCH, nq),
        in_specs=[
            pl.BlockSpec((A_Q_COLS, Q_STEP), lambda b, i: (0, b * nq + i)),
            pl.BlockSpec((IDX_HEADS * IDX_DIM, Q_STEP), lambda b, i: (0, b * nq + i)),
            pl.BlockSpec((IDX_HEADS, Q_STEP), lambda b, i: (0, b * nq + i)),
            pl.BlockSpec((SEQ, IDX_DIM), lambda b, i: (b, 0)),
            pl.BlockSpec((SEQ, A_LAT), lambda b, i: (b, 0)),
            pl.BlockSpec((A_LAT, SEQ), lambda b, i: (0, b)),
        ] + [slab_spec(v) for v in views],
        out_specs=[pl.BlockSpec((A_Q_COLS, Q_STEP), lambda b, i: (0, b * nq + i))] + [slab_spec(v) for v in views],
        out_shape=[jax.ShapeDtypeStruct((A_Q_COLS, TOKENS), BF16)]
        + [jax.ShapeDtypeStruct(v.shape, BF16) for v in views],
        scratch_shapes=[
            pltpu.VMEM((SEQ, Q_STEP), F32),
            pltpu.VMEM((SEQ, Q_STEP), BF16),
            pltpu.VMEM((SUBLANES, Q_STEP), F32),
            pltpu.VMEM((SUBLANES, Q_STEP), jnp.int32),
            pltpu.VMEM((SEQ, cols), F32),
            pltpu.VMEM((SUBLANES, cols), F32),
            pltpu.VMEM((SUBLANES, cols), F32),
            pltpu.VMEM((A_LAT, cols), F32),
        ],
        compiler_params=_params("parallel", "arbitrary"),
        name="dsa_core",
    )(qt, qit, wit, ki, k, ct, *views)
    return outs[0], [o.reshape(w.shape) for o, w in zip(outs[1:], weights)]


SEG = ROW_TILE // SUBLANES
HALO = (CONV_W - 1) * SUBLANES


def _interleave_rows(a):
    return jnp.transpose(a.reshape(SUBLANES, SEG, a.shape[-1]), (1, 0, 2)).reshape(a.shape)


def _deinterleave_rows(a):
    return jnp.transpose(a.reshape(SEG, SUBLANES, a.shape[-1]), (1, 0, 2)).reshape(a.shape)


def _ffn_kernel(m_transposed, x_ref, m_ref, wout_ref, g_ref, wup_ref, cw_ref, cb_ref, wdown_ref, o_ref,
                hb_scr, u_even, u_odd, gate_scr, carry_scr):
    t = pl.program_id(0)
    first_of_sequence = (t % (SEQ // ROW_TILE)) == 0
    mixer = (_dot_tn if m_transposed else _dot)(m_ref[...], wout_ref[...])
    x1 = x_ref[...] + mixer
    o_ref[...] = x1
    hb_scr[...] = _interleave_rows(_rms(x1, g_ref[...])).astype(BF16)
    first_sublane = lax.broadcasted_iota(jnp.int32, (HALO, 2 * FF_CHUNK), 0) % SUBLANES == 0

    def chunk_cols(ref, c):
        if isinstance(c, int):
            a0, b0 = c * FF_CHUNK, D_FF + c * FF_CHUNK
        else:
            a0 = pl.multiple_of(c * FF_CHUNK, FF_CHUNK)
            b0 = pl.multiple_of(D_FF + c * FF_CHUNK, FF_CHUNK)
        return ref[:, pl.ds(a0, FF_CHUNK)], ref[:, pl.ds(b0, FF_CHUNK)]

    def up(c):
        hb = hb_scr[...]
        wa, wb = chunk_cols(wup_ref, c)
        return jnp.concatenate([_dot(hb, wa), _dot(hb, wb)], axis=1)

    def down_one_sublane(a):
        return jnp.concatenate([pltpu.roll(a[r:r + SUBLANES, :], 1, 0)
                                for r in range(0, a.shape[0], SUBLANES)], axis=0)

    def act(c, u):
        tail = u[ROW_TILE - HALO:, :]
        prev = jnp.where(first_of_sequence, 0.0, carry_scr[c])
        carry_scr[c] = tail
        wrap = jnp.where(first_sublane, down_one_sublane(prev), down_one_sublane(tail))
        cw = jnp.concatenate(chunk_cols(cw_ref, c), axis=1)
        cb = jnp.concatenate(chunk_cols(cb_ref, c), axis=1)
        v = u * cw[CONV_W - 1:CONV_W, :] + cb
        for d in range(1, CONV_W):
            shifted = jnp.concatenate([wrap[HALO - d * SUBLANES:, :], u[:ROW_TILE - d * SUBLANES, :]], axis=0)
            v = v + shifted * cw[CONV_W - 1 - d:CONV_W - d, :]
        a = v[:, :FF_CHUNK]
        b = v[:, FF_CHUNK:]
        return (a * jax.nn.sigmoid(a) * b).astype(BF16)

    def stage(c, u_cur, u_next, has_up=True):
        if has_up:
            u_next[...] = up(c + 1)
        g0 = c * FF_CHUNK if isinstance(c, int) else pl.multiple_of(c * FF_CHUNK, FF_CHUNK)
        gate_scr[:, pl.ds(g0, FF_CHUNK)] = act(c, u_cur[...])

    def even_stage(c, **kw):
        stage(c, u_even, u_odd, **kw)

    def odd_stage(c, **kw):
        stage(c, u_odd, u_even, **kw)

    u_even[...] = up(0)
    for c in range(N_FF_CHUNKS):
        (even_stage if c % 2 == 0 else odd_stage)(c, has_up=c + 1 < N_FF_CHUNKS)
    o_ref[...] += _deinterleave_rows(_dot(gate_scr[...], wdown_ref[...]))


def _layer_spec(shape, layer):
    nd = len(shape)
    return pl.BlockSpec((None,) + tuple(shape), lambda *_: (layer,) + (0,) * nd, pipeline_mode=pl.Buffered(1))


def _ffn(x2, m, m_transposed, wout, g, layer, wup, cw, cb, wdown):
    n = TOKENS // ROW_TILE
    km = wout.shape[0]
    m_spec = (pl.BlockSpec((km, ROW_TILE), lambda i: (0, i)) if m_transposed
              else pl.BlockSpec((ROW_TILE, km), lambda i: (i, 0)))
    return pl.pallas_call(
        functools.partial(_ffn_kernel, m_transposed),
        grid=(n,),
        in_specs=[
            pl.BlockSpec((ROW_TILE, D_MODEL), lambda i: (i, 0)),
            m_spec,
            _const_spec((km, D_MODEL)),
            _const_spec((1, D_MODEL)),
            _layer_spec((D_MODEL, 2 * D_FF), layer),
            _const_spec((CONV_W, 2 * D_FF)),
            _const_spec((1, 2 * D_FF)),
            _layer_spec((D_FF, D_MODEL), layer),
        ],
        out_specs=pl.BlockSpec((ROW_TILE, D_MODEL), lambda i: (i, 0)),
        out_shape=jax.ShapeDtypeStruct((TOKENS, D_MODEL), F32),
        scratch_shapes=[
            pltpu.VMEM((ROW_TILE, D_MODEL), BF16),
            pltpu.VMEM((ROW_TILE, 2 * FF_CHUNK), F32),
            pltpu.VMEM((ROW_TILE, 2 * FF_CHUNK), F32),
            pltpu.VMEM((ROW_TILE, D_FF), BF16),
            pltpu.VMEM((N_FF_CHUNKS, HALO, 2 * FF_CHUNK), F32),
        ],
        compiler_params=_params("arbitrary"),
        name="conv_ffn",
    )(x2, m, wout, g, wup, cw, cb, wdown)


def _proj_b_kernel(x_ref, g_ref, w_ref, cos_ref, sin_ref, zeta_ref, q_out, k_out, kz_out, v_out, sg_out):
    row_halves = (slice(0, ROW_TILE // 2), slice(ROW_TILE // 2, ROW_TILE))
    hb_halves = [_rms(x_ref[rows, :], g_ref[...]).astype(BF16) for rows in row_halves]
    hb = jnp.concatenate(hb_halves, axis=0)
    cos = cos_ref[...]
    sin = sin_ref[...]
    half = B_DK // 2

    def rotary(z):
        z1, z2 = z[:, :half], z[:, half:]
        return z1 * cos - z2 * sin, z1 * sin + z2 * cos

    dq = B_HEADS * B_DK
    dv = B_HEADS * B_DV
    for h in range(B_HEADS):
        for rows, hb_half in zip(row_halves, hb_halves):
            gate = _dot(hb_half, w_ref[:, 2 * dq + dv + h * B_DV:2 * dq + dv + (h + 1) * B_DV])
            sg_out[rows, h * B_DV:(h + 1) * B_DV] = (gate * jax.nn.sigmoid(gate)).astype(BF16)
    for h in range(B_HEADS):
        zq = _dot(hb, w_ref[:, h * B_DK:(h + 1) * B_DK])
        for part, r in enumerate(rotary(zq)):
            cols = slice(h * B_DK + part * half, h * B_DK + (part + 1) * half)
            q_out[:, cols] = r.astype(BF16)
        zk = _dot(hb, w_ref[:, dq + h * B_DK:dq + (h + 1) * B_DK])
        for part, r in enumerate(rotary(zk)):
            cols = slice(h * B_DK + part * half, h * B_DK + (part + 1) * half)
            kr = r * (B_DK ** -0.5)
            k_out[:, cols] = kr.astype(BF16)
            kz_out[:, cols] = (kr * zeta_ref[:, cols]).astype(BF16)
    for h in range(B_HEADS):
        v_out[:, h * B_DV:(h + 1) * B_DV] = _dot(
            hb, w_ref[:, 2 * dq + h * B_DV:2 * dq + (h + 1) * B_DV]).astype(BF16)


def _proj_b(x2, g, w, cos, sin, zeta_rows):
    n = TOKENS // ROW_TILE
    per_seq = SEQ // ROW_TILE
    dq = B_HEADS * B_DK
    dv = B_HEADS * B_DV
    return pl.pallas_call(
        _proj_b_kernel,
        grid=(n,),
        in_specs=[
            pl.BlockSpec((ROW_TILE, D_MODEL), lambda i: (i, 0)),
            _const_spec((1, D_MODEL)),
            _const_spec((D_MODEL, 2 * dq + 2 * dv)),
            pl.BlockSpec((ROW_TILE, B_DK // 2), lambda i: (i % per_seq, 0)),
            pl.BlockSpec((ROW_TILE, B_DK // 2), lambda i: (i % per_seq, 0)),
            _const_spec((ROW_TILE, dq)),
        ],
        out_specs=[
            pl.BlockSpec((ROW_TILE, dq), lambda i: (i, 0)),
            pl.BlockSpec((ROW_TILE, dq), lambda i: (i, 0)),
            pl.BlockSpec((ROW_TILE, dq), lambda i: (i, 0)),
            pl.BlockSpec((ROW_TILE, dv), lambda i: (i, 0)),
            pl.BlockSpec((ROW_TILE, dv), lambda i: (i, 0)),
        ],
        out_shape=[
            jax.ShapeDtypeStruct((TOKENS, dq), BF16),
            jax.ShapeDtypeStruct((TOKENS, dq), BF16),
            jax.ShapeDtypeStruct((TOKENS, dq), BF16),
            jax.ShapeDtypeStruct((TOKENS, dv), BF16),
            jax.ShapeDtypeStruct((TOKENS, dv), BF16),
        ],
        compiler_params=_params("parallel"),
        name="ret_in_proj",
    )(x2, g, w, cos, sin, zeta_rows)


RET_HEADS_PER_STEP = 2
RET_VALUE_COLS = 256


def _retention_kernel(q_ref, k_ref, kz_ref, v_ref, sg_ref, decay_ref, xi_ref, gc_ref, og_ref,
                      y_ref, state_scr):
    state_scr[...] = jnp.zeros(state_scr.shape, F32)

    for n in range(SEQ // RET_CHUNK):
        r0 = n * RET_CHUNK
        for hh in range(RET_HEADS_PER_STEP):
            kcols = slice(hh * B_DK, (hh + 1) * B_DK)
            vcols = slice(hh * B_DV, (hh + 1) * B_DV)
            qc = q_ref[pl.ds(r0, RET_CHUNK), kcols]
            kzc = kz_ref[pl.ds(r0, RET_CHUNK), kcols]
            s = (_dot_nt(qc, k_ref[pl.ds(r0, RET_CHUNK), kcols]) * decay_ref[hh]).astype(BF16)
            rets = []
            for c0 in range(0, B_DV, RET_VALUE_COLS):
                cols = slice(hh * B_DV + c0, hh * B_DV + c0 + RET_VALUE_COLS)
                hcols = slice(c0, c0 + RET_VALUE_COLS)
                vc = v_ref[pl.ds(r0, RET_CHUNK), cols]
                state = state_scr[hh, :, hcols]
                rets.append(_dot(s, vc) + _dot(qc, state.astype(BF16)) * xi_ref[hh, :, hcols])
                state_scr[hh, :, hcols] = state * gc_ref[hh, :, hcols] + _dot_tn(kzc, vc)
            sumsq = functools.reduce(jnp.add, [jnp.sum(r * r, axis=-1, keepdims=True) for r in rets])
            scale = lax.rsqrt(sumsq * (1.0 / B_DV) + EPS)
            for i, r in enumerate(rets):
                cols = slice(hh * B_DV + i * RET_VALUE_COLS, hh * B_DV + (i + 1) * RET_VALUE_COLS)
                y = r * scale * og_ref[:, cols] * sg_ref[pl.ds(r0, RET_CHUNK), cols].astype(F32)
                y_ref[pl.ds(r0, RET_CHUNK), cols] = y.astype(BF16)


def _retention(q, k, kz, v, sg, decay, xi, gc, og):
    hs = RET_HEADS_PER_STEP
    return pl.pallas_call(
        _retention_kernel,
        grid=(BATCH, B_HEADS // hs),
        in_specs=[
            pl.BlockSpec((SEQ, hs * B_DK), lambda b, h: (b, h)),
            pl.BlockSpec((SEQ, hs * B_DK), lambda b, h: (b, h)),
            pl.BlockSpec((SEQ, hs * B_DK), lambda b, h: (b, h)),
            pl.BlockSpec((SEQ, hs * B_DV), lambda b, h: (b, h)),
            pl.BlockSpec((SEQ, hs * B_DV), lambda b, h: (b, h)),
            pl.BlockSpec((hs, RET_CHUNK, RET_CHUNK), lambda b, h: (h, 0, 0)),
            pl.BlockSpec((hs, RET_CHUNK, B_DV), lambda b, h: (h, 0, 0)),
            pl.BlockSpec((hs, 1, B_DV), lambda b, h: (h, 0, 0)),
            pl.BlockSpec((1, hs * B_DV), lambda b, h: (0, h)),
        ],
        out_specs=pl.BlockSpec((SEQ, hs * B_DV), lambda b, h: (b, h)),
        out_shape=jax.ShapeDtypeStruct((TOKENS, B_HEADS * B_DV), BF16),
        scratch_shapes=[pltpu.VMEM((hs, B_DK, B_DV), F32)],
        compiler_params=_params("parallel", "parallel"),
        name="retention",
    )(q, k, kz, v, sg, decay, xi, gc, og)


def _retention_tables():
    log_gamma = np.log1p(-(2.0 ** (-5.0 - np.arange(B_HEADS, dtype=np.float32)))).astype(np.float32)
    pos = np.arange(RET_CHUNK, dtype=np.float32)
    diff = pos[:, None] - pos[None, :]
    decay = np.where(diff[None] >= 0,
                     np.exp(np.maximum(diff, 0.0)[None] * log_gamma[:, None, None]), 0.0)
    xi = np.exp((pos + 1.0)[None, :] * log_gamma[:, None])
    zeta = np.exp((RET_CHUNK - 1.0 - pos)[None, :] * log_gamma[:, None])
    gc = np.exp(RET_CHUNK * log_gamma)
    xi_b = np.broadcast_to(xi[:, :, None], (B_HEADS, RET_CHUNK, B_DV))
    gc_b = np.broadcast_to(gc[:, None, None], (B_HEADS, 1, B_DV))
    zeta_rows = np.tile(np.repeat(zeta.T, B_DK, axis=1), (ROW_TILE // RET_CHUNK, 1))
    return tuple(jnp.asarray(a, F32) for a in (decay, xi_b, gc_b, zeta_rows))


def _rotary_tables():
    inv = (1.0 / (ROPE_BASE ** (np.arange(0, B_DK, 2, dtype=np.float32) / B_DK))).astype(np.float32)
    ang = np.arange(SEQ, dtype=np.float32)[:, None] * inv[None, :]
    return jnp.asarray(np.cos(ang), F32), jnp.asarray(np.sin(ang), F32)


def kernel(x, norm_mix_g, norm_ffn_g, a_w_in, a_q_g, a_k_g, a_iq_g, a_ik_g, a_w_out,
           b_w_in, b_out_g, b_w_out, f_w_up, f_conv_w, f_conv_b, f_w_down):
    x2 = x.reshape(TOKENS, D_MODEL)

    w = a_w_in[0]
    o1 = A_Q_COLS
    o2 = o1 + A_LAT
    o3 = o2 + IDX_HEADS * IDX_DIM
    o4 = o3 + IDX_DIM
    wck = jnp.concatenate(
        [w[:, o1:o2], w[:, o3:o4], jnp.zeros((D_MODEL, A_LAT - IDX_DIM), F32)], axis=1).astype(BF16)
    wt = jnp.concatenate([w[:, :o3], w[:, o4:]], axis=1).T.astype(BF16)
    qt, k, ct, qit, ki, wit = _proj_a(
        x2, norm_mix_g[0][None, :], wck, wt,
        a_q_g[0][:, None], a_k_g[0][None, :], a_iq_g[0][:, None], a_ik_g[0][None, :])
    ot, (w_up, w_down, wa_out, wb_in, wb_out) = _dsa(
        qt, qit, wit, ki, k, ct, [f_w_up, f_w_down, a_w_out[0], b_w_in[0], b_w_out[0]])
    x2 = _ffn(x2, ot, True, wa_out, norm_ffn_g[0][None, :],
              0, w_up, f_conv_w[0], f_conv_b[0][None, :], w_down)

    cos, sin = _rotary_tables()
    decay, xi, gc, zeta_rows = _retention_tables()
    q, kk, kz, v, sg = _proj_b(x2, norm_mix_g[1][None, :], wb_in, cos, sin, zeta_rows)
    y = _retention(q, kk, kz, v, sg, decay, xi, gc, b_out_g[0][None, :])
    x2 = _ffn(x2, y, False, wb_out, norm_ffn_g[1][None, :],
              1, w_up, f_conv_w[1], f_conv_b[1][None, :], w_down)
    return x2.reshape(BATCH, SEQ, D_MODEL)
```

```python
import functools

import jax
import jax.numpy as jnp
import numpy as np
from jax import lax
from jax.experimental import pallas as pl
from jax.experimental.pallas import tpu as pltpu

D_MODEL = 1024
BATCH = 8
SEQ = 2048
TOKENS = BATCH * SEQ
CHUNK = 64
CHUNK_SHIFT = CHUNK.bit_length() - 1
EPS = 1e-6
A_HEADS = 8
A_LAT = 128
IDX_HEADS = 8
IDX_DIM = 64
TOPK = 256
A_Q_COLS = A_HEADS * A_LAT
B_HEADS = 4
B_DK = 256
B_DV = 512
ROPE_BASE = 10000.0
RET_CHUNK = 256
D_FF = 2816
CONV_W = 3
FF_CHUNK = 256
N_FF_CHUNKS = D_FF // FF_CHUNK

LANES = 128
SUBLANES = 8
ROW_TILE = 512
Q_BLOCK = 128
Q_GROUPS = 2
Q_STEP = Q_GROUPS * Q_BLOCK
KEY_CHUNK = 256
CHUNKS_PER_LOOP_STEP = 4
COUNT_ROWS = 32
INDEX_BITS = 12
COARSE_BITS = 16
FINE_BITS = 17
BF16_TILE_ROWS = 16
VMEM_LIMIT = 56 * 1024 * 1024
MASK_NEG = -(2.0 ** 100)
LOG2_E = 1.4426950408889634

BF16 = jnp.bfloat16
F32 = jnp.float32


def _params(*sem):
    return pltpu.CompilerParams(dimension_semantics=sem, vmem_limit_bytes=VMEM_LIMIT)


def _const_spec(shape):
    nd = len(shape)
    return pl.BlockSpec(shape, lambda *_: (0,) * nd, pipeline_mode=pl.Buffered(1))


def _rms(x, g):
    return x * lax.rsqrt(jnp.mean(x * x, axis=-1, keepdims=True) + EPS) * g


def _dot(a, b):
    return jnp.dot(a, b, preferred_element_type=F32)


def _dot_nt(a, b):
    return lax.dot_general(a, b, (((1,), (1,)), ((), ())), preferred_element_type=F32)


def _dot_tn(a, b):
    return lax.dot_general(a, b, (((0,), (0,)), ((), ())), preferred_element_type=F32)


T_Q = 0
T_C = T_Q + A_Q_COLS
T_QI = T_C + A_LAT
T_WI = T_QI + IDX_HEADS * IDX_DIM
T_ROWS = T_WI + IDX_HEADS


def _proj_a_kernel(x_ref, g_ref, wck_ref, wt_ref, qg_ref, kg_ref, iqg_ref, ikg_ref,
                   qt_out, k_out, ct_out, qit_out, ki_out, wit_out):
    def head_norm(rows, gain):
        return rows * lax.rsqrt(jnp.mean(rows * rows, axis=0, keepdims=True) + EPS) * gain

    qg = qg_ref[...] * (A_LAT ** -0.5 * LOG2_E)
    hb = _rms(x_ref[...], g_ref[...]).astype(BF16)
    ck = _dot(hb, wck_ref[...])
    k_out[...] = _rms(ck[:, :A_LAT], kg_ref[...]).astype(BF16)
    ki_out[...] = _rms(ck[:, A_LAT:A_LAT + IDX_DIM], ikg_ref[...]).astype(BF16)

    t = _dot_nt(wt_ref[...], hb)
    for h in range(A_HEADS):
        r0 = T_Q + h * A_LAT
        qt_out[h * A_LAT:(h + 1) * A_LAT, :] = head_norm(t[r0:r0 + A_LAT, :], qg).astype(BF16)
    ct_out[...] = t[T_C:T_C + A_LAT, :].astype(BF16)
    for h in range(IDX_HEADS):
        r0 = T_QI + h * IDX_DIM
        qit_out[h * IDX_DIM:(h + 1) * IDX_DIM, :] = head_norm(t[r0:r0 + IDX_DIM, :], iqg_ref[...]).astype(BF16)
    wit_out[...] = t[T_WI:, :] * (IDX_HEADS ** -0.5 * IDX_DIM ** -0.5)


def _proj_a(x2, g, wck, wt, qg, kg, iqg, ikg):
    n = TOKENS // ROW_TILE
    return pl.pallas_call(
        _proj_a_kernel,
        grid=(n,),
        in_specs=[
            pl.BlockSpec((ROW_TILE, D_MODEL), lambda i: (i, 0)),
            _const_spec((1, D_MODEL)),
            _const_spec((D_MODEL, 2 * A_LAT)),
            _const_spec((T_ROWS, D_MODEL)),
            _const_spec((A_LAT, 1)),
            _const_spec((1, A_LAT)),
            _const_spec((IDX_DIM, 1)),
            _const_spec((1, IDX_DIM)),
        ],
        out_specs=[
            pl.BlockSpec((A_Q_COLS, ROW_TILE), lambda i: (0, i)),
            pl.BlockSpec((ROW_TILE, A_LAT), lambda i: (i, 0)),
            pl.BlockSpec((A_LAT, ROW_TILE), lambda i: (0, i)),
            pl.BlockSpec((IDX_HEADS * IDX_DIM, ROW_TILE), lambda i: (0, i)),
            pl.BlockSpec((ROW_TILE, IDX_DIM), lambda i: (i, 0)),
            pl.BlockSpec((IDX_HEADS, ROW_TILE), lambda i: (0, i)),
        ],
        out_shape=[
            jax.ShapeDtypeStruct((A_Q_COLS, TOKENS), BF16),
            jax.ShapeDtypeStruct((TOKENS, A_LAT), BF16),
            jax.ShapeDtypeStruct((A_LAT, TOKENS), BF16),
            jax.ShapeDtypeStruct((IDX_HEADS * IDX_DIM, TOKENS), BF16),
            jax.ShapeDtypeStruct((TOKENS, IDX_DIM), BF16),
            jax.ShapeDtypeStruct((IDX_HEADS, TOKENS), F32),
        ],
        compiler_params=_params("parallel"),
        name="dsa_in_proj",
    )(x2, g, wck, wt, qg, kg, iqg, ikg)


def _ordered_bits_to_float(u):
    k = u ^ jnp.int32(-2 ** 31)
    bits = k ^ ((k >> 31) & jnp.int32(0x7FFFFFFF))
    return pltpu.bitcast(bits, F32)


def _dsa_kernel(n_weights, qt_ref, qit_ref, wit_ref, ki_ref, k_ref, ct_ref, *refs):
    w_refs = refs[:n_weights]
    ot_ref = refs[n_weights]
    wo_refs = refs[n_weights + 1:2 * n_weights + 1]
    score_scr, score16_scr, thr_scr, cut_scr, s_scr, m_scr, l_scr, acc_scr = refs[2 * n_weights + 1:]
    for w_ref, wo_ref in zip(w_refs, wo_refs):
        wo_ref[...] = w_ref[...].astype(BF16)
    step = pl.program_id(1)
    n_chunks = step + 1
    cols = A_HEADS * Q_BLOCK
    qpos = step * Q_STEP + lax.broadcasted_iota(jnp.int32, (1, Q_STEP), 1)
    limit = ((qpos >> CHUNK_SHIFT) + 1) << CHUNK_SHIFT

    def block_lanes(g):
        return slice(g * Q_BLOCK, (g + 1) * Q_BLOCK)

    def block_cols(g):
        return slice(g * cols, (g + 1) * cols)

    def key_pos(r0, rows):
        return r0 + lax.broadcasted_iota(jnp.int32, (rows, Q_STEP), 0)

    def for_chunks(fn):
        def run(first_chunk, count):
            for t in range(count):
                fn(pl.multiple_of((first_chunk + t) * KEY_CHUNK, KEY_CHUNK))

        def group(jj, carry):
            run(jj * CHUNKS_PER_LOOP_STEP, CHUNKS_PER_LOOP_STEP)
            return carry
        lax.fori_loop(0, n_chunks // CHUNKS_PER_LOOP_STEP, group, 0)
        done = (n_chunks // CHUNKS_PER_LOOP_STEP) * CHUNKS_PER_LOOP_STEP
        size = CHUNKS_PER_LOOP_STEP // 2
        while size >= 1:
            rest = n_chunks - done
            pl.when(rest >= size)(functools.partial(run, done, size))
            done = done + jnp.where(rest >= size, size, 0)
            size //= 2

    def count(n, pred_fn):
        acc = None
        for j in range(n):
            r0 = j * KEY_CHUNK
            hit = pred_fn(score_scr[r0:r0 + KEY_CHUNK, :], key_pos(r0, KEY_CHUNK))
            part = jnp.sum(hit.reshape(KEY_CHUNK // COUNT_ROWS, COUNT_ROWS, Q_STEP), axis=0)
            acc = part if acc is None else acc + part
        return jnp.sum(acc, axis=0, keepdims=True)

    thr_scr[...] = jnp.full(thr_scr.shape, -jnp.inf, F32)
    cut_scr[...] = jnp.full(cut_scr.shape, 2 * SEQ, jnp.int32)

    @pl.when(n_chunks * KEY_CHUNK <= TOPK)
    def _():
        score_scr[0:KEY_CHUNK, :] = jnp.zeros((KEY_CHUNK, Q_STEP), F32)

    @pl.when(n_chunks * KEY_CHUNK > TOPK)
    def _():
        def score_chunk(r0):
            kic = ki_ref[pl.ds(r0, KEY_CHUNK), :]
            acc = jnp.zeros((KEY_CHUNK, Q_STEP), F32)
            for h in range(IDX_HEADS):
                rel = _dot(kic, qit_ref[h * IDX_DIM:(h + 1) * IDX_DIM, :])
                acc = acc + jnp.maximum(rel, 0.0) * wit_ref[h:h + 1, :]
            masked = jnp.where(key_pos(r0, KEY_CHUNK) < limit, acc, -jnp.inf)
            score_scr[pl.ds(r0, KEY_CHUNK), :] = masked
            score16_scr[pl.ds(r0, KEY_CHUNK), :] = masked.astype(BF16)
        for_chunks(score_chunk)

    def count_coarse(n, cand):
        one = jnp.ones((), BF16)
        zero = jnp.zeros((), BF16)
        accs = [None, None]
        for j in range(n):
            r0 = j * KEY_CHUNK
            hit = jnp.where(score16_scr[r0:r0 + KEY_CHUNK, :] >= cand, one, zero)
            for i, r in enumerate(range(0, KEY_CHUNK, COUNT_ROWS)):
                part = hit[r:r + COUNT_ROWS, :]
                accs[i % 2] = part if accs[i % 2] is None else accs[i % 2] + part
        return jnp.sum(accs[0].astype(F32) + accs[1].astype(F32), axis=0, keepdims=True)

    def search(n):
        def coarse_body(b, u):
            cand = u | (jnp.int32(1) << (31 - b))
            n_ge = count_coarse(n, _ordered_bits_to_float(cand).astype(BF16))
            return jnp.where(n_ge >= float(TOPK), cand, u)
        u16 = lax.fori_loop(0, COARSE_BITS, coarse_body, jnp.zeros((1, Q_STEP), jnp.int32))

        base = u16 - jnp.int32(1 << (31 - COARSE_BITS))

        def fine_body(b, d):
            cand = d | (jnp.int32(1) << (FINE_BITS - 1 - b))
            cf = _ordered_bits_to_float(base + cand)
            n_ge = count(n, lambda s, _: jnp.where(s >= cf, 1.0, 0.0))
            return jnp.where(n_ge >= float(TOPK), cand, d)
        d = lax.fori_loop(0, FINE_BITS, fine_body, jnp.zeros((1, Q_STEP), jnp.int32))
        thr = _ordered_bits_to_float(base + d)
        thr_scr[...] = jnp.broadcast_to(thr, thr_scr.shape)

        excess = count(n, lambda s, _: jnp.where(s >= thr, 1.0, 0.0)) - float(TOPK)
        max_excess = jnp.max(excess).astype(jnp.int32)

        @pl.when(jnp.logical_and(max_excess > 0, max_excess <= INDEX_BITS))
        def _():
            def drop_body(i, cut):
                top = None
                for j in range(n):
                    r0 = j * KEY_CHUNK
                    kp = key_pos(r0, KEY_CHUNK)
                    tied = jnp.where(score_scr[r0:r0 + KEY_CHUNK, :] == thr, jnp.where(kp < cut, kp, -1), -1)
                    part = jnp.max(tied.reshape(KEY_CHUNK // COUNT_ROWS, COUNT_ROWS, Q_STEP), axis=0)
                    top = part if top is None else jnp.maximum(top, part)
                top = jnp.max(top, axis=0, keepdims=True)
                return jnp.where(excess > i.astype(F32), top, cut)
            cut = lax.fori_loop(0, max_excess, drop_body, jnp.full((1, Q_STEP), 2 * SEQ, jnp.int32))
            cut_scr[...] = jnp.broadcast_to(cut, cut_scr.shape)

        @pl.when(max_excess > INDEX_BITS)
        def _():
            n_gt = count(n, lambda s, _: jnp.where(s > thr, 1.0, 0.0))
            need = float(TOPK) - n_gt

            def cut_body(b, cut):
                cand = cut | (jnp.int32(1) << (INDEX_BITS - 1 - b))
                n_eq = count(n, lambda s, kp: jnp.where(s == thr, jnp.where(kp < cand, 1.0, 0.0), 0.0))
                return jnp.where(n_eq <= need, cand, cut)
            cut = lax.fori_loop(0, INDEX_BITS, cut_body, jnp.zeros((1, Q_STEP), jnp.int32))
            cut_scr[...] = jnp.broadcast_to(cut, cut_scr.shape)

    for n in range(TOPK // KEY_CHUNK + 1, SEQ // KEY_CHUNK + 1):
        pl.when(n_chunks == n)(functools.partial(search, n))

    thr = thr_scr[0:1, :]
    cut = cut_scr[0:1, :]
    eye = (lax.broadcasted_iota(jnp.int32, (Q_BLOCK, Q_BLOCK), 0)
           == lax.broadcasted_iota(jnp.int32, (Q_BLOCK, Q_BLOCK), 1))
    eye = jnp.concatenate([jnp.where(eye, 1.0, 0.0).astype(BF16)] * A_HEADS, axis=1)
    qt_aug = []
    for g in range(Q_GROUPS):
        qt = jnp.concatenate([qt_ref[h * A_LAT:(h + 1) * A_LAT, block_lanes(g)] for h in range(A_HEADS)], axis=1)
        qt_aug.append(jnp.concatenate([qt, eye], axis=0))

    def row_groups(a):
        return a.reshape(KEY_CHUNK // SUBLANES, SUBLANES, a.shape[-1])

    m_scr[...] = jnp.full(m_scr.shape, -jnp.inf, F32)

    def logits_chunk(r0):
        sc = score_scr[pl.ds(r0, KEY_CHUNK), :]
        kp = key_pos(r0, KEY_CHUNK)
        tie_bias = jnp.where(sc == thr, jnp.where(kp < cut, 0.0, MASK_NEG), MASK_NEG)
        bias = jnp.where(kp < limit, jnp.where(sc > thr, 0.0, tie_bias), MASK_NEG).astype(BF16)
        kc = k_ref[pl.ds(r0, KEY_CHUNK), :]
        for g in range(Q_GROUPS):
            st = _dot(jnp.concatenate([kc, bias[:, block_lanes(g)]], axis=1), qt_aug[g])
            s_scr[pl.ds(r0, KEY_CHUNK), block_cols(g)] = st
            m_scr[:, block_cols(g)] = jnp.maximum(m_scr[:, block_cols(g)], jnp.max(row_groups(st), axis=0))
    for_chunks(logits_chunk)
    m = jnp.max(m_scr[...], axis=0, keepdims=True)

    l_scr[...] = jnp.zeros(l_scr.shape, F32)
    acc_scr[...] = jnp.zeros(acc_scr.shape, F32)

    def pv_chunk(r0):
        p = jnp.exp2(s_scr[pl.ds(r0, KEY_CHUNK), :] - m)
        l_scr[...] += jnp.sum(row_groups(p), axis=0)
        acc_scr[...] += _dot(ct_ref[:, pl.ds(r0, KEY_CHUNK)], p.astype(BF16))
    for_chunks(pv_chunk)

    o = acc_scr[...] / jnp.sum(l_scr[...], axis=0, keepdims=True)
    for g in range(Q_GROUPS):
        for h in range(A_HEADS):
            c0 = g * cols + h * Q_BLOCK
            ot_ref[h * A_LAT:(h + 1) * A_LAT, block_lanes(g)] = o[:, c0:c0 + Q_BLOCK].astype(BF16)


def _dsa(qt, qit, wit, ki, k, ct, weights):
    assert Q_STEP == KEY_CHUNK
    nq = SEQ // Q_STEP
    cols = Q_GROUPS * A_HEADS * Q_BLOCK
    steps = BATCH * nq
    views = [w.reshape(-1, w.shape[-1]) for w in weights]

    def slab_spec(v):
        n_slabs = max(d for d in range(1, steps + 1)
                      if steps % d == 0 and v.shape[0] % (d * BF16_TILE_ROWS) == 0)
        per_slab = steps // n_slabs
        return pl.BlockSpec((v.shape[0] // n_slabs, v.shape[1]), lambda b, i: ((b * nq + i) // per_slab, 0))
    outs = pl.pallas_call(
        functools.partial(_dsa_kernel, len(views)),
        grid=(BATCH, nq),
        in_specs=[
            pl.BlockSpec((A_Q_COLS, Q_STEP), lambda b, i: (0, b * nq + i)),
            pl.BlockSpec((IDX_HEADS * IDX_DIM, Q_STEP), lambda b, i: (0, b * nq + i)),
            pl.BlockSpec((IDX_HEADS, Q_STEP), lambda b, i: (0, b * nq + i)),
            pl.BlockSpec((SEQ, IDX_DIM), lambda b, i: (b, 0)),
            pl.BlockSpec((SEQ, A_LAT), lambda b, i: (b, 0)),
            pl.BlockSpec((A_LAT, SEQ), lambda b, i: (0, b)),
        ] + [slab_spec(v) for v in views],
        out_specs=[pl.BlockSpec((A_Q_COLS, Q_STEP), lambda b, i: (0, b * nq + i))] + [slab_spec(v) for v in views],
        out_shape=[jax.ShapeDtypeStruct((A_Q_COLS, TOKENS), BF16)]
        + [jax.ShapeDtypeStruct(v.shape, BF16) for v in views],
        scratch_shapes=[
            pltpu.VMEM((SEQ, Q_STEP), F32),
            pltpu.VMEM((SEQ, Q_STEP), BF16),
            pltpu.VMEM((SUBLANES, Q_STEP), F32),
            pltpu.VMEM((SUBLANES, Q_STEP), jnp.int32),
            pltpu.VMEM((SEQ, cols), F32),
            pltpu.VMEM((SUBLANES, cols), F32),
            pltpu.VMEM((SUBLANES, cols), F32),
            pltpu.VMEM((A_LAT, cols), F32),
        ],
        compiler_params=_params("parallel", "arbitrary"),
        name="dsa_core",
    )(qt, qit, wit, ki, k, ct, *views)
    return outs[0], [o.reshape(w.shape) for o, w in zip(outs[1:], weights)]


SEG = ROW_TILE // SUBLANES
HALO = (CONV_W - 1) * SUBLANES


def _interleave_rows(a):
    return jnp.transpose(a.reshape(SUBLANES, SEG, a.shape[-1]), (1, 0, 2)).reshape(a.shape)


def _deinterleave_rows(a):
    return jnp.transpose(a.reshape(SEG, SUBLANES, a.shape[-1]), (1, 0, 2)).reshape(a.shape)


def _ffn_kernel(m_transposed, x_ref, m_ref, wout_ref, g_ref, wup_ref, cw_ref, cb_ref, wdown_ref, o_ref,
                hb_scr, u_even, u_odd, gate_scr, carry_scr):
    t = pl.program_id(0)
    first_of_sequence = (t % (SEQ // ROW_TILE)) == 0
    mixer = (_dot_tn if m_transposed else _dot)(m_ref[...], wout_ref[...])
    x1 = x_ref[...] + mixer
    o_ref[...] = x1
    hb_scr[...] = _interleave_rows(_rms(x1, g_ref[...])).astype(BF16)
    first_sublane = lax.broadcasted_iota(jnp.int32, (HALO, 2 * FF_CHUNK), 0) % SUBLANES == 0

    def chunk_cols(ref, c):
        if isinstance(c, int):
            a0, b0 = c * FF_CHUNK, D_FF + c * FF_CHUNK
        else:
            a0 = pl.multiple_of(c * FF_CHUNK, FF_CHUNK)
            b0 = pl.multiple_of(D_FF + c * FF_CHUNK, FF_CHUNK)
        return ref[:, pl.ds(a0, FF_CHUNK)], ref[:, pl.ds(b0, FF_CHUNK)]

    def up(c):
        hb = hb_scr[...]
        wa, wb = chunk_cols(wup_ref, c)
        return jnp.concatenate([_dot(hb, wa), _dot(hb, wb)], axis=1)

    def down_one_sublane(a):
        return jnp.concatenate([pltpu.roll(a[r:r + SUBLANES, :], 1, 0)
                                for r in range(0, a.shape[0], SUBLANES)], axis=0)

    def act(c, u):
        tail = u[ROW_TILE - HALO:, :]
        prev = jnp.where(first_of_sequence, 0.0, carry_scr[c])
        carry_scr[c] = tail
        wrap = jnp.where(first_sublane, down_one_sublane(prev), down_one_sublane(tail))
        cw = jnp.concatenate(chunk_cols(cw_ref, c), axis=1)
        cb = jnp.concatenate(chunk_cols(cb_ref, c), axis=1)
        v = u * cw[CONV_W - 1:CONV_W, :] + cb
        for d in range(1, CONV_W):
            shifted = jnp.concatenate([wrap[HALO - d * SUBLANES:, :], u[:ROW_TILE - d * SUBLANES, :]], axis=0)
            v = v + shifted * cw[CONV_W - 1 - d:CONV_W - d, :]
        a = v[:, :FF_CHUNK]
        b = v[:, FF_CHUNK:]
        return (a * jax.nn.sigmoid(a) * b).astype(BF16)

    def stage(c, u_cur, u_next, has_up=True):
        if has_up:
            u_next[...] = up(c + 1)
        g0 = c * FF_CHUNK if isinstance(c, int) else pl.multiple_of(c * FF_CHUNK, FF_CHUNK)
        gate_scr[:, pl.ds(g0, FF_CHUNK)] = act(c, u_cur[...])

    def even_stage(c, **kw):
        stage(c, u_even, u_odd, **kw)

    def odd_stage(c, **kw):
        stage(c, u_odd, u_even, **kw)

    u_even[...] = up(0)
    for c in range(N_FF_CHUNKS):
        (even_stage if c % 2 == 0 else odd_stage)(c, has_up=c + 1 < N_FF_CHUNKS)
    o_ref[...] += _deinterleave_rows(_dot(gate_scr[...], wdown_ref[...]))


def _layer_spec(shape, layer):
    nd = len(shape)
    return pl.BlockSpec((None,) + tuple(shape), lambda *_: (layer,) + (0,) * nd, pipeline_mode=pl.Buffered(1))


def _ffn(x2, m, m_transposed, wout, g, layer, wup, cw, cb, wdown):
    n = TOKENS // ROW_TILE
    km = wout.shape[0]
    m_spec = (pl.BlockSpec((km, ROW_TILE), lambda i: (0, i)) if m_transposed
              else pl.BlockSpec((ROW_TILE, km), lambda i: (i, 0)))
    return pl.pallas_call(
        functools.partial(_ffn_kernel, m_transposed),
        grid=(n,),
        in_specs=[
            pl.BlockSpec((ROW_TILE, D_MODEL), lambda i: (i, 0)),
            m_spec,
            _const_spec((km, D_MODEL)),
            _const_spec((1, D_MODEL)),
            _layer_spec((D_MODEL, 2 * D_FF), layer),
            _const_spec((CONV_W, 2 * D_FF)),
            _const_spec((1, 2 * D_FF)),
            _layer_spec((D_FF, D_MODEL), layer),
        ],
        out_specs=pl.BlockSpec((ROW_TILE, D_MODEL), lambda i: (i, 0)),
        out_shape=jax.ShapeDtypeStruct((TOKENS, D_MODEL), F32),
        scratch_shapes=[
            pltpu.VMEM((ROW_TILE, D_MODEL), BF16),
            pltpu.VMEM((ROW_TILE, 2 * FF_CHUNK), F32),
            pltpu.VMEM((ROW_TILE, 2 * FF_CHUNK), F32),
            pltpu.VMEM((ROW_TILE, D_FF), BF16),
            pltpu.VMEM((N_FF_CHUNKS, HALO, 2 * FF_CHUNK), F32),
        ],
        compiler_params=_params("arbitrary"),
        name="conv_ffn",
    )(x2, m, wout, g, wup, cw, cb, wdown)


RET_VALUE_COLS = 256


def _retention_mixer_kernel(x_ref, g_ref, w_ref, cos_ref, sin_ref, zeta_ref, decay_ref, xi_ref, gc_ref, og_ref,
                            y_ref, state_scr):
    @pl.when(pl.program_id(1) == 0)
    def _():
        state_scr[...] = jnp.zeros(state_scr.shape, F32)

    row_halves = (slice(0, ROW_TILE // 2), slice(ROW_TILE // 2, ROW_TILE))
    hb_halves = [_rms(x_ref[rows, :], g_ref[...]).astype(BF16) for rows in row_halves]
    hb = jnp.concatenate(hb_halves, axis=0)
    cos = cos_ref[...]
    sin = sin_ref[...]
    half = B_DK // 2
    dq = B_HEADS * B_DK
    dv = B_HEADS * B_DV

    def rotary(z):
        z1, z2 = z[:, :half], z[:, half:]
        return jnp.concatenate([z1 * cos - z2 * sin, z1 * sin + z2 * cos], axis=1)

    for h in range(B_HEADS):
        kcols = slice(h * B_DK, (h + 1) * B_DK)
        gate = jnp.concatenate(
            [_dot(hb_half, w_ref[:, 2 * dq + dv + h * B_DV:2 * dq + dv + (h + 1) * B_DV]) for hb_half in hb_halves],
            axis=0)
        sg = (gate * jax.nn.sigmoid(gate)).astype(BF16)
        q = rotary(_dot(hb, w_ref[:, h * B_DK:(h + 1) * B_DK])).astype(BF16)
        kr = rotary(_dot(hb, w_ref[:, dq + h * B_DK:dq + (h + 1) * B_DK])) * (B_DK ** -0.5)
        k = kr.astype(BF16)
        kz = (kr * zeta_ref[:, kcols]).astype(BF16)
        v = _dot(hb, w_ref[:, 2 * dq + h * B_DV:2 * dq + (h + 1) * B_DV]).astype(BF16)
        for n in range(ROW_TILE // RET_CHUNK):
            rows = slice(n * RET_CHUNK, (n + 1) * RET_CHUNK)
            qc = q[rows, :]
            kzc = kz[rows, :]
            s = (_dot_nt(qc, k[rows, :]) * decay_ref[h]).astype(BF16)
            rets = []
            for c0 in range(0, B_DV, RET_VALUE_COLS):
                hcols = slice(c0, c0 + RET_VALUE_COLS)
                vc = v[rows, hcols]
                state = state_scr[h, :, hcols]
                rets.append(_dot(s, vc) + _dot(qc, state.astype(BF16)) * xi_ref[h, :, hcols])
                state_scr[h, :, hcols] = state * gc_ref[h, :, hcols] + _dot_tn(kzc, vc)
            sumsq = functools.reduce(jnp.add, [jnp.sum(r * r, axis=-1, keepdims=True) for r in rets])
            scale = lax.rsqrt(sumsq * (1.0 / B_DV) + EPS)
            for i, r in enumerate(rets):
                hcols = slice(i * RET_VALUE_COLS, (i + 1) * RET_VALUE_COLS)
                cols = slice(h * B_DV + i * RET_VALUE_COLS, h * B_DV + (i + 1) * RET_VALUE_COLS)
                y = r * scale * og_ref[:, cols] * sg[rows, hcols].astype(F32)
                y_ref[rows, cols] = y.astype(BF16)


def _retention_mixer(x2, g, w, cos, sin, zeta_rows, decay, xi, gc, og):
    assert ROW_TILE % RET_CHUNK == 0
    per_seq = SEQ // ROW_TILE
    dq = B_HEADS * B_DK
    dv = B_HEADS * B_DV
    return pl.pallas_call(
        _retention_mixer_kernel,
        grid=(BATCH, per_seq),
        in_specs=[
            pl.BlockSpec((ROW_TILE, D_MODEL), lambda b, t: (b * per_seq + t, 0)),
            _const_spec((1, D_MODEL)),
            _const_spec((D_MODEL, 2 * dq + 2 * dv)),
            pl.BlockSpec((ROW_TILE, B_DK // 2), lambda b, t: (t, 0)),
            pl.BlockSpec((ROW_TILE, B_DK // 2), lambda b, t: (t, 0)),
            _const_spec((ROW_TILE, dq)),
            _const_spec((B_HEADS, RET_CHUNK, RET_CHUNK)),
            _const_spec((B_HEADS, RET_CHUNK, B_DV)),
            _const_spec((B_HEADS, 1, B_DV)),
            _const_spec((1, dv)),
        ],
        out_specs=pl.BlockSpec((ROW_TILE, dv), lambda b, t: (b * per_seq + t, 0)),
        out_shape=jax.ShapeDtypeStruct((TOKENS, dv), BF16),
        scratch_shapes=[pltpu.VMEM((B_HEADS, B_DK, B_DV), F32)],
        compiler_params=_params("parallel", "arbitrary"),
        name="retention_mixer",
    )(x2, g, w, cos, sin, zeta_rows, decay, xi, gc, og)


def _retention_tables():
    log_gamma = np.log1p(-(2.0 ** (-5.0 - np.arange(B_HEADS, dtype=np.float32)))).astype(np.float32)
    pos = np.arange(RET_CHUNK, dtype=np.float32)
    diff = pos[:, None] - pos[None, :]
    decay = np.where(diff[None] >= 0,
                     np.exp(np.maximum(diff, 0.0)[None] * log_gamma[:, None, None]), 0.0)
    xi = np.exp((pos + 1.0)[None, :] * log_gamma[:, None])
    zeta = np.exp((RET_CHUNK - 1.0 - pos)[None, :] * log_gamma[:, None])
    gc = np.exp(RET_CHUNK * log_gamma)
    xi_b = np.broadcast_to(xi[:, :, None], (B_HEADS, RET_CHUNK, B_DV))
    gc_b = np.broadcast_to(gc[:, None, None], (B_HEADS, 1, B_DV))
    zeta_rows = np.tile(np.repeat(zeta.T, B_DK, axis=1), (ROW_TILE // RET_CHUNK, 1))
    return tuple(jnp.asarray(a, F32) for a in (decay, xi_b, gc_b, zeta_rows))


def _rotary_tables():
    inv = (1.0 / (ROPE_BASE ** (np.arange(0, B_DK, 2, dtype=np.float32) / B_DK))).astype(np.float32)
    ang = np.arange(SEQ, dtype=np.float32)[:, None] * inv[None, :]
    return jnp.asarray(np.cos(ang), F32), jnp.asarray(np.sin(ang), F32)


def kernel(x, norm_mix_g, norm_ffn_g, a_w_in, a_q_g, a_k_g, a_iq_g, a_ik_g, a_w_out,
           b_w_in, b_out_g, b_w_out, f_w_up, f_conv_w, f_conv_b, f_w_down):
    x2 = x.reshape(TOKENS, D_MODEL)

    w = a_w_in[0]
    o1 = A_Q_COLS
    o2 = o1 + A_LAT
    o3 = o2 + IDX_HEADS * IDX_DIM
    o4 = o3 + IDX_DIM
    wck = jnp.concatenate(
        [w[:, o1:o2], w[:, o3:o4], jnp.zeros((D_MODEL, A_LAT - IDX_DIM), F32)], axis=1).astype(BF16)
    wt = jnp.concatenate([w[:, :o3], w[:, o4:]], axis=1).T.astype(BF16)
    qt, k, ct, qit, ki, wit = _proj_a(
        x2, norm_mix_g[0][None, :], wck, wt,
        a_q_g[0][:, None], a_k_g[0][None, :], a_iq_g[0][:, None], a_ik_g[0][None, :])
    ot, (w_up, w_down, wa_out, wb_in, wb_out) = _dsa(
        qt, qit, wit, ki, k, ct, [f_w_up, f_w_down, a_w_out[0], b_w_in[0], b_w_out[0]])
    x2 = _ffn(x2, ot, True, wa_out, norm_ffn_g[0][None, :],
              0, w_up, f_conv_w[0], f_conv_b[0][None, :], w_down)

    cos, sin = _rotary_tables()
    decay, xi, gc, zeta_rows = _retention_tables()
    y = _retention_mixer(x2, norm_mix_g[1][None, :], wb_in, cos, sin, zeta_rows, decay, xi, gc,
                         b_out_g[0][None, :])
    x2 = _ffn(x2, y, False, wb_out, norm_ffn_g[1][None, :],
              1, w_up, f_conv_w[1], f_conv_b[1][None, :], w_down)
    return x2.reshape(BATCH, SEQ, D_MODEL)
```

```python
import functools

import jax
import jax.numpy as jnp
import numpy as np
from jax import lax
from jax.experimental import pallas as pl
from jax.experimental.pallas import tpu as pltpu

D_MODEL = 1024
BATCH = 8
SEQ = 2048
TOKENS = BATCH * SEQ
CHUNK = 64
CHUNK_SHIFT = CHUNK.bit_length() - 1
EPS = 1e-6
A_HEADS = 8
A_LAT = 128
IDX_HEADS = 8
IDX_DIM = 64
TOPK = 256
A_Q_COLS = A_HEADS * A_LAT
B_HEADS = 4
B_DK = 256
B_DV = 512
ROPE_BASE = 10000.0
RET_CHUNK = 256
D_FF = 2816
CONV_W = 3
FF_CHUNK = 256
N_FF_CHUNKS = D_FF // FF_CHUNK

LANES = 128
SUBLANES = 8
ROW_TILE = 512
Q_BLOCK = 128
Q_GROUPS = 2
Q_STEP = Q_GROUPS * Q_BLOCK
KEY_CHUNK = 256
CHUNKS_PER_LOOP_STEP = 4
COUNT_ROWS = 32
INDEX_BITS = 12
COARSE_BITS = 16
FINE_BITS = 17
BF16_TILE_ROWS = 16
VMEM_LIMIT = 56 * 1024 * 1024
MASK_NEG = -(2.0 ** 100)
LOG2_E = 1.4426950408889634

BF16 = jnp.bfloat16
F32 = jnp.float32


def _params(*sem):
    return pltpu.CompilerParams(dimension_semantics=sem, vmem_limit_bytes=VMEM_LIMIT)


def _const_spec(shape):
    nd = len(shape)
    return pl.BlockSpec(shape, lambda *_: (0,) * nd, pipeline_mode=pl.Buffered(1))


def _rms(x, g):
    return x * lax.rsqrt(jnp.mean(x * x, axis=-1, keepdims=True) + EPS) * g


def _dot(a, b):
    return jnp.dot(a, b, preferred_element_type=F32)


def _dot_nt(a, b):
    return lax.dot_general(a, b, (((1,), (1,)), ((), ())), preferred_element_type=F32)


def _dot_tn(a, b):
    return lax.dot_general(a, b, (((0,), (0,)), ((), ())), preferred_element_type=F32)


T_Q = 0
T_C = T_Q + A_Q_COLS
T_QI = T_C + A_LAT
T_WI = T_QI + IDX_HEADS * IDX_DIM
T_ROWS = T_WI + IDX_HEADS


def _proj_a_kernel(x_ref, g_ref, wck_ref, wt_ref, qg_ref, kg_ref, iqg_ref, ikg_ref,
                   qt_out, k_out, ct_out, qit_out, ki_out, wit_out):
    def head_norm(rows, gain):
        return rows * lax.rsqrt(jnp.mean(rows * rows, axis=0, keepdims=True) + EPS) * gain

    qg = qg_ref[...] * (A_LAT ** -0.5 * LOG2_E)
    hb = _rms(x_ref[...], g_ref[...]).astype(BF16)
    ck = _dot(hb, wck_ref[...])
    k_out[...] = _rms(ck[:, :A_LAT], kg_ref[...]).astype(BF16)
    ki_out[...] = _rms(ck[:, A_LAT:A_LAT + IDX_DIM], ikg_ref[...]).astype(BF16)

    t = _dot_nt(wt_ref[...], hb)
    for h in range(A_HEADS):
        r0 = T_Q + h * A_LAT
        qt_out[h * A_LAT:(h + 1) * A_LAT, :] = head_norm(t[r0:r0 + A_LAT, :], qg).astype(BF16)
    ct_out[...] = t[T_C:T_C + A_LAT, :].astype(BF16)
    for h in range(IDX_HEADS):
        r0 = T_QI + h * IDX_DIM
        qit_out[h * IDX_DIM:(h + 1) * IDX_DIM, :] = head_norm(t[r0:r0 + IDX_DIM, :], iqg_ref[...]).astype(BF16)
    wit_out[...] = t[T_WI:, :] * (IDX_HEADS ** -0.5 * IDX_DIM ** -0.5)


def _proj_a(x2, g, wck, wt, qg, kg, iqg, ikg):
    n = TOKENS // ROW_TILE
    return pl.pallas_call(
        _proj_a_kernel,
        grid=(n,),
        in_specs=[
            pl.BlockSpec((ROW_TILE, D_MODEL), lambda i: (i, 0)),
            _const_spec((1, D_MODEL)),
            _const_spec((D_MODEL, 2 * A_LAT)),
            _const_spec((T_ROWS, D_MODEL)),
            _const_spec((A_LAT, 1)),
            _const_spec((1, A_LAT)),
            _const_spec((IDX_DIM, 1)),
            _const_spec((1, IDX_DIM)),
        ],
        out_specs=[
            pl.BlockSpec((A_Q_COLS, ROW_TILE), lambda i: (0, i)),
            pl.BlockSpec((ROW_TILE, A_LAT), lambda i: (i, 0)),
            pl.BlockSpec((A_LAT, ROW_TILE), lambda i: (0, i)),
            pl.BlockSpec((IDX_HEADS * IDX_DIM, ROW_TILE), lambda i: (0, i)),
            pl.BlockSpec((ROW_TILE, IDX_DIM), lambda i: (i, 0)),
            pl.BlockSpec((IDX_HEADS, ROW_TILE), lambda i: (0, i)),
        ],
        out_shape=[
            jax.ShapeDtypeStruct((A_Q_COLS, TOKENS), BF16),
            jax.ShapeDtypeStruct((TOKENS, A_LAT), BF16),
            jax.ShapeDtypeStruct((A_LAT, TOKENS), BF16),
            jax.ShapeDtypeStruct((IDX_HEADS * IDX_DIM, TOKENS), BF16),
            jax.ShapeDtypeStruct((TOKENS, IDX_DIM), BF16),
            jax.ShapeDtypeStruct((IDX_HEADS, TOKENS), F32),
        ],
        compiler_params=_params("parallel"),
        name="dsa_in_proj",
    )(x2, g, wck, wt, qg, kg, iqg, ikg)


def _ordered_bits_to_float(u):
    k = u ^ jnp.int32(-2 ** 31)
    bits = k ^ ((k >> 31) & jnp.int32(0x7FFFFFFF))
    return pltpu.bitcast(bits, F32)


def _dsa_kernel(n_weights, qt_ref, qit_ref, wit_ref, ki_ref, k_ref, ct_ref, *refs):
    w_refs = refs[:n_weights]
    ot_ref = refs[n_weights]
    wo_refs = refs[n_weights + 1:2 * n_weights + 1]
    score_scr, score16_scr, thr_scr, cut_scr, s_scr, m_scr, l_scr, acc_scr = refs[2 * n_weights + 1:]
    for w_ref, wo_ref in zip(w_refs, wo_refs):
        wo_ref[...] = w_ref[...].astype(BF16)
    step = pl.program_id(1)
    n_chunks = step + 1
    cols = A_HEADS * Q_BLOCK
    qpos = step * Q_STEP + lax.broadcasted_iota(jnp.int32, (1, Q_STEP), 1)
    limit = ((qpos >> CHUNK_SHIFT) + 1) << CHUNK_SHIFT

    def block_lanes(g):
        return slice(g * Q_BLOCK, (g + 1) * Q_BLOCK)

    def block_cols(g):
        return slice(g * cols, (g + 1) * cols)

    def key_pos(r0, rows):
        return r0 + lax.broadcasted_iota(jnp.int32, (rows, Q_STEP), 0)

    def for_chunks(fn):
        def run(first_chunk, count):
            for t in range(count):
                fn(pl.multiple_of((first_chunk + t) * KEY_CHUNK, KEY_CHUNK))

        def group(jj, carry):
            run(jj * CHUNKS_PER_LOOP_STEP, CHUNKS_PER_LOOP_STEP)
            return carry
        lax.fori_loop(0, n_chunks // CHUNKS_PER_LOOP_STEP, group, 0)
        done = (n_chunks // CHUNKS_PER_LOOP_STEP) * CHUNKS_PER_LOOP_STEP
        size = CHUNKS_PER_LOOP_STEP // 2
        while size >= 1:
            rest = n_chunks - done
            pl.when(rest >= size)(functools.partial(run, done, size))
            done = done + jnp.where(rest >= size, size, 0)
            size //= 2

    def count(n, pred_fn):
        acc = None
        for j in range(n):
            r0 = j * KEY_CHUNK
            hit = pred_fn(score_scr[r0:r0 + KEY_CHUNK, :], key_pos(r0, KEY_CHUNK))
            part = jnp.sum(hit.reshape(KEY_CHUNK // COUNT_ROWS, COUNT_ROWS, Q_STEP), axis=0)
            acc = part if acc is None else acc + part
        return jnp.sum(acc, axis=0, keepdims=True)

    thr_scr[...] = jnp.full(thr_scr.shape, -jnp.inf, F32)
    cut_scr[...] = jnp.full(cut_scr.shape, 2 * SEQ, jnp.int32)

    @pl.when(n_chunks * KEY_CHUNK <= TOPK)
    def _():
        score_scr[0:KEY_CHUNK, :] = jnp.zeros((KEY_CHUNK, Q_STEP), F32)

    @pl.when(n_chunks * KEY_CHUNK > TOPK)
    def _():
        def score_chunk(r0):
            kic = ki_ref[pl.ds(r0, KEY_CHUNK), :]
            acc = jnp.zeros((KEY_CHUNK, Q_STEP), F32)
            for h in range(IDX_HEADS):
                rel = _dot(kic, qit_ref[h * IDX_DIM:(h + 1) * IDX_DIM, :])
                acc = acc + jnp.maximum(rel, 0.0) * wit_ref[h:h + 1, :]
            masked = jnp.where(key_pos(r0, KEY_CHUNK) < limit, acc, -jnp.inf)
            score_scr[pl.ds(r0, KEY_CHUNK), :] = masked
            score16_scr[pl.ds(r0, KEY_CHUNK), :] = masked.astype(BF16)
        for_chunks(score_chunk)

    def count_coarse(n, cand):
        one = jnp.ones((), BF16)
        zero = jnp.zeros((), BF16)
        accs = [None, None]
        for j in range(n):
            r0 = j * KEY_CHUNK
            hit = jnp.where(score16_scr[r0:r0 + KEY_CHUNK, :] >= cand, one, zero)
            for i, r in enumerate(range(0, KEY_CHUNK, COUNT_ROWS)):
                part = hit[r:r + COUNT_ROWS, :]
                accs[i % 2] = part if accs[i % 2] is None else accs[i % 2] + part
        return jnp.sum(accs[0].astype(F32) + accs[1].astype(F32), axis=0, keepdims=True)

    def search(n):
        def coarse_body(b, u):
            cand = u | (jnp.int32(1) << (31 - b))
            n_ge = count_coarse(n, _ordered_bits_to_float(cand).astype(BF16))
            return jnp.where(n_ge >= float(TOPK), cand, u)
        u16 = lax.fori_loop(0, COARSE_BITS, coarse_body, jnp.zeros((1, Q_STEP), jnp.int32))

        base = u16 - jnp.int32(1 << (31 - COARSE_BITS))

        def fine_body(b, d):
            cand = d | (jnp.int32(1) << (FINE_BITS - 1 - b))
            cf = _ordered_bits_to_float(base + cand)
            n_ge = count(n, lambda s, _: jnp.where(s >= cf, 1.0, 0.0))
            return jnp.where(n_ge >= float(TOPK), cand, d)
        d = lax.fori_loop(0, FINE_BITS, fine_body, jnp.zeros((1, Q_STEP), jnp.int32))
        thr = _ordered_bits_to_float(base + d)
        thr_scr[...] = jnp.broadcast_to(thr, thr_scr.shape)

        excess = count(n, lambda s, _: jnp.where(s >= thr, 1.0, 0.0)) - float(TOPK)
        max_excess = jnp.max(excess).astype(jnp.int32)

        @pl.when(jnp.logical_and(max_excess > 0, max_excess <= INDEX_BITS))
        def _():
            def drop_body(i, cut):
                top = None
                for j in range(n):
                    r0 = j * KEY_CHUNK
                    kp = key_pos(r0, KEY_CHUNK)
                    tied = jnp.where(score_scr[r0:r0 + KEY_CHUNK, :] == thr, jnp.where(kp < cut, kp, -1), -1)
                    part = jnp.max(tied.reshape(KEY_CHUNK // COUNT_ROWS, COUNT_ROWS, Q_STEP), axis=0)
                    top = part if top is None else jnp.maximum(top, part)
                top = jnp.max(top, axis=0, keepdims=True)
                return jnp.where(excess > i.astype(F32), top, cut)
            cut = lax.fori_loop(0, max_excess, drop_body, jnp.full((1, Q_STEP), 2 * SEQ, jnp.int32))
            cut_scr[...] = jnp.broadcast_to(cut, cut_scr.shape)

        @pl.when(max_excess > INDEX_BITS)
        def _():
            n_gt = count(n, lambda s, _: jnp.where(s > thr, 1.0, 0.0))
            need = float(TOPK) - n_gt

            def cut_body(b, cut):
                cand = cut | (jnp.int32(1) << (INDEX_BITS - 1 - b))
                n_eq = count(n, lambda s, kp: jnp.where(s == thr, jnp.where(kp < cand, 1.0, 0.0), 0.0))
                return jnp.where(n_eq <= need, cand, cut)
            cut = lax.fori_loop(0, INDEX_BITS, cut_body, jnp.zeros((1, Q_STEP), jnp.int32))
            cut_scr[...] = jnp.broadcast_to(cut, cut_scr.shape)

    for n in range(TOPK // KEY_CHUNK + 1, SEQ // KEY_CHUNK + 1):
        pl.when(n_chunks == n)(functools.partial(search, n))

    thr = thr_scr[0:1, :]
    cut = cut_scr[0:1, :]
    eye = (lax.broadcasted_iota(jnp.int32, (Q_BLOCK, Q_BLOCK), 0)
           == lax.broadcasted_iota(jnp.int32, (Q_BLOCK, Q_BLOCK), 1))
    eye = jnp.concatenate([jnp.where(eye, 1.0, 0.0).astype(BF16)] * A_HEADS, axis=1)
    qt_aug = []
    for g in range(Q_GROUPS):
        qt = jnp.concatenate([qt_ref[h * A_LAT:(h + 1) * A_LAT, block_lanes(g)] for h in range(A_HEADS)], axis=1)
        qt_aug.append(jnp.concatenate([qt, eye], axis=0))

    def row_groups(a):
        return a.reshape(KEY_CHUNK // SUBLANES, SUBLANES, a.shape[-1])

    m_scr[...] = jnp.full(m_scr.shape, -jnp.inf, F32)

    def logits_chunk(r0):
        sc = score_scr[pl.ds(r0, KEY_CHUNK), :]
        kp = key_pos(r0, KEY_CHUNK)
        tie_bias = jnp.where(sc == thr, jnp.where(kp < cut, 0.0, MASK_NEG), MASK_NEG)
        bias = jnp.where(kp < limit, jnp.where(sc > thr, 0.0, tie_bias), MASK_NEG).astype(BF16)
        kc = k_ref[pl.ds(r0, KEY_CHUNK), :]
        for g in range(Q_GROUPS):
            st = _dot(jnp.concatenate([kc, bias[:, block_lanes(g)]], axis=1), qt_aug[g])
            s_scr[pl.ds(r0, KEY_CHUNK), block_cols(g)] = st
            m_scr[:, block_cols(g)] = jnp.maximum(m_scr[:, block_cols(g)], jnp.max(row_groups(st), axis=0))
    for_chunks(logits_chunk)
    m = jnp.max(m_scr[...], axis=0, keepdims=True)

    l_scr[...] = jnp.zeros(l_scr.shape, F32)
    acc_scr[...] = jnp.zeros(acc_scr.shape, F32)

    def pv_chunk(r0):
        p = jnp.exp2(s_scr[pl.ds(r0, KEY_CHUNK), :] - m)
        l_scr[...] += jnp.sum(row_groups(p), axis=0)
        acc_scr[...] += _dot(ct_ref[:, pl.ds(r0, KEY_CHUNK)], p.astype(BF16))
    for_chunks(pv_chunk)

    o = acc_scr[...] / jnp.sum(l_scr[...], axis=0, keepdims=True)
    for g in range(Q_GROUPS):
        for h in range(A_HEADS):
            c0 = g * cols + h * Q_BLOCK
            ot_ref[h * A_LAT:(h + 1) * A_LAT, block_lanes(g)] = o[:, c0:c0 + Q_BLOCK].astype(BF16)


def _dsa(qt, qit, wit, ki, k, ct, weights):
    assert Q_STEP == KEY_CHUNK
    nq = SEQ // Q_STEP
    cols = Q_GROUPS * A_HEADS * Q_BLOCK
    steps = BATCH * nq
    views = [w.reshape(-1, w.shape[-1]) for w in weights]

    def slab_spec(v):
        n_slabs = max(d for d in range(1, steps + 1)
                      if steps % d == 0 and v.shape[0] % (d * BF16_TILE_ROWS) == 0)
        per_slab = steps // n_slabs
        return pl.BlockSpec((v.shape[0] // n_slabs, v.shape[1]), lambda b, i: ((b * nq + i) // per_slab, 0))
    outs = pl.pallas_call(
        functools.partial(_dsa_kernel, len(views)),
        grid=(BATCH, nq),
        in_specs=[
            pl.BlockSpec((A_Q_COLS, Q_STEP), lambda b, i: (0, b * nq + i)),
            pl.BlockSpec((IDX_HEADS * IDX_DIM, Q_STEP), lambda b, i: (0, b * nq + i)),
            pl.BlockSpec((IDX_HEADS, Q_STEP), lambda b, i: (0, b * nq + i)),
            pl.BlockSpec((SEQ, IDX_DIM), lambda b, i: (b, 0)),
            pl.BlockSpec((SEQ, A_LAT), lambda b, i: (b, 0)),
            pl.BlockSpec((A_LAT, SEQ), lambda b, i: (0, b)),
        ] + [slab_spec(v) for v in views],
        out_specs=[pl.BlockSpec((A_Q_COLS, Q_STEP), lambda b, i: (0, b * nq + i))] + [slab_spec(v) for v in views],
        out_shape=[jax.ShapeDtypeStruct((A_Q_COLS, TOKENS), BF16)]
        + [jax.ShapeDtypeStruct(v.shape, BF16) for v in views],
        scratch_shapes=[
            pltpu.VMEM((SEQ, Q_STEP), F32),
            pltpu.VMEM((SEQ, Q_STEP), BF16),
            pltpu.VMEM((SUBLANES, Q_STEP), F32),
            pltpu.VMEM((SUBLANES, Q_STEP), jnp.int32),
            pltpu.VMEM((SEQ, cols), F32),
            pltpu.VMEM((SUBLANES, cols), F32),
            pltpu.VMEM((SUBLANES, cols), F32),
            pltpu.VMEM((A_LAT, cols), F32),
        ],
        compiler_params=_params("parallel", "arbitrary"),
        name="dsa_core",
    )(qt, qit, wit, ki, k, ct, *views)
    return outs[0], [o.reshape(w.shape) for o, w in zip(outs[1:], weights)]


SEG = ROW_TILE // SUBLANES
HALO = (CONV_W - 1) * SUBLANES


def _interleave_rows(a):
    return jnp.transpose(a.reshape(SUBLANES, SEG, a.shape[-1]), (1, 0, 2)).reshape(a.shape)


def _deinterleave_rows(a):
    return jnp.transpose(a.reshape(SEG, SUBLANES, a.shape[-1]), (1, 0, 2)).reshape(a.shape)


def _ffn_kernel(m_transposed, x_ref, m_ref, wout_ref, g_ref, wup_ref, cw_ref, cb_ref, wdown_ref, o_ref,
                hb_scr, u_even, u_odd, gate_scr, carry_scr):
    t = pl.program_id(0)
    first_of_sequence = (t % (SEQ // ROW_TILE)) == 0
    if m_ref is None:
        x1 = x_ref[...]
    else:
        x1 = x_ref[...] + (_dot_tn if m_transposed else _dot)(m_ref[...], wout_ref[...])
    o_ref[...] = x1
    hb_scr[...] = _interleave_rows(_rms(x1, g_ref[...])).astype(BF16)
    first_sublane = lax.broadcasted_iota(jnp.int32, (HALO, 2 * FF_CHUNK), 0) % SUBLANES == 0

    def chunk_cols(ref, c):
        if isinstance(c, int):
            a0, b0 = c * FF_CHUNK, D_FF + c * FF_CHUNK
        else:
            a0 = pl.multiple_of(c * FF_CHUNK, FF_CHUNK)
            b0 = pl.multiple_of(D_FF + c * FF_CHUNK, FF_CHUNK)
        return ref[:, pl.ds(a0, FF_CHUNK)], ref[:, pl.ds(b0, FF_CHUNK)]

    def up(c):
        hb = hb_scr[...]
        wa, wb = chunk_cols(wup_ref, c)
        return jnp.concatenate([_dot(hb, wa), _dot(hb, wb)], axis=1)

    def down_one_sublane(a):
        return jnp.concatenate([pltpu.roll(a[r:r + SUBLANES, :], 1, 0)
                                for r in range(0, a.shape[0], SUBLANES)], axis=0)

    def act(c, u):
        tail = u[ROW_TILE - HALO:, :]
        prev = jnp.where(first_of_sequence, 0.0, carry_scr[c])
        carry_scr[c] = tail
        wrap = jnp.where(first_sublane, down_one_sublane(prev), down_one_sublane(tail))
        cw = jnp.concatenate(chunk_cols(cw_ref, c), axis=1)
        cb = jnp.concatenate(chunk_cols(cb_ref, c), axis=1)
        v = u * cw[CONV_W - 1:CONV_W, :] + cb
        for d in range(1, CONV_W):
            shifted = jnp.concatenate([wrap[HALO - d * SUBLANES:, :], u[:ROW_TILE - d * SUBLANES, :]], axis=0)
            v = v + shifted * cw[CONV_W - 1 - d:CONV_W - d, :]
        a = v[:, :FF_CHUNK]
        b = v[:, FF_CHUNK:]
        return (a * jax.nn.sigmoid(a) * b).astype(BF16)

    def stage(c, u_cur, u_next, has_up=True):
        if has_up:
            u_next[...] = up(c + 1)
        g0 = c * FF_CHUNK if isinstance(c, int) else pl.multiple_of(c * FF_CHUNK, FF_CHUNK)
        gate_scr[:, pl.ds(g0, FF_CHUNK)] = act(c, u_cur[...])

    def even_stage(c, **kw):
        stage(c, u_even, u_odd, **kw)

    def odd_stage(c, **kw):
        stage(c, u_odd, u_even, **kw)

    u_even[...] = up(0)
    for c in range(N_FF_CHUNKS):
        (even_stage if c % 2 == 0 else odd_stage)(c, has_up=c + 1 < N_FF_CHUNKS)
    o_ref[...] += _deinterleave_rows(_dot(gate_scr[...], wdown_ref[...]))


def _layer_spec(shape, layer):
    nd = len(shape)
    return pl.BlockSpec((None,) + tuple(shape), lambda *_: (layer,) + (0,) * nd, pipeline_mode=pl.Buffered(1))


def _ffn(x2, m, m_transposed, wout, g, layer, wup, cw, cb, wdown):
    n = TOKENS // ROW_TILE
    if m is None:
        body = functools.partial(_ffn_kernel_no_mixer)
        mixer_specs, mixer_args = [], []
    else:
        km = wout.shape[0]
        body = functools.partial(_ffn_kernel, m_transposed)
        mixer_specs = [pl.BlockSpec((km, ROW_TILE), lambda i: (0, i)) if m_transposed
                       else pl.BlockSpec((ROW_TILE, km), lambda i: (i, 0)),
                       _const_spec((km, D_MODEL))]
        mixer_args = [m, wout]
    return pl.pallas_call(
        body,
        grid=(n,),
        in_specs=[pl.BlockSpec((ROW_TILE, D_MODEL), lambda i: (i, 0))] + mixer_specs + [
            _const_spec((1, D_MODEL)),
            _layer_spec((D_MODEL, 2 * D_FF), layer),
            _const_spec((CONV_W, 2 * D_FF)),
            _const_spec((1, 2 * D_FF)),
            _layer_spec((D_FF, D_MODEL), layer),
        ],
        out_specs=pl.BlockSpec((ROW_TILE, D_MODEL), lambda i: (i, 0)),
        out_shape=jax.ShapeDtypeStruct((TOKENS, D_MODEL), F32),
        scratch_shapes=[
            pltpu.VMEM((ROW_TILE, D_MODEL), BF16),
            pltpu.VMEM((ROW_TILE, 2 * FF_CHUNK), F32),
            pltpu.VMEM((ROW_TILE, 2 * FF_CHUNK), F32),
            pltpu.VMEM((ROW_TILE, D_FF), BF16),
            pltpu.VMEM((N_FF_CHUNKS, HALO, 2 * FF_CHUNK), F32),
        ],
        compiler_params=_params("arbitrary"),
        name="conv_ffn",
    )(x2, *mixer_args, g, wup, cw, cb, wdown)


def _ffn_kernel_no_mixer(x_ref, *refs):
    _ffn_kernel(False, x_ref, None, None, *refs)


RET_VALUE_COLS = 256


def _retention_mixer_kernel(x_ref, g_ref, w_ref, cos_ref, sin_ref, zeta_ref, decay_ref, xi_ref, gc_ref, og_ref,
                            wout_ref, o_ref, state_scr, y_ref):
    @pl.when(pl.program_id(1) == 0)
    def _():
        state_scr[...] = jnp.zeros(state_scr.shape, F32)

    row_halves = (slice(0, ROW_TILE // 2), slice(ROW_TILE // 2, ROW_TILE))
    hb_halves = [_rms(x_ref[rows, :], g_ref[...]).astype(BF16) for rows in row_halves]
    hb = jnp.concatenate(hb_halves, axis=0)
    cos = cos_ref[...]
    sin = sin_ref[...]
    half = B_DK // 2
    dq = B_HEADS * B_DK
    dv = B_HEADS * B_DV

    def rotary(z):
        z1, z2 = z[:, :half], z[:, half:]
        return jnp.concatenate([z1 * cos - z2 * sin, z1 * sin + z2 * cos], axis=1)

    for h in range(B_HEADS):
        kcols = slice(h * B_DK, (h + 1) * B_DK)
        gate = jnp.concatenate(
            [_dot(hb_half, w_ref[:, 2 * dq + dv + h * B_DV:2 * dq + dv + (h + 1) * B_DV]) for hb_half in hb_halves],
            axis=0)
        sg = (gate * jax.nn.sigmoid(gate)).astype(BF16)
        q = rotary(_dot(hb, w_ref[:, h * B_DK:(h + 1) * B_DK])).astype(BF16)
        kr = rotary(_dot(hb, w_ref[:, dq + h * B_DK:dq + (h + 1) * B_DK])) * (B_DK ** -0.5)
        k = kr.astype(BF16)
        kz = (kr * zeta_ref[:, kcols]).astype(BF16)
        v = _dot(hb, w_ref[:, 2 * dq + h * B_DV:2 * dq + (h + 1) * B_DV]).astype(BF16)
        for n in range(ROW_TILE // RET_CHUNK):
            rows = slice(n * RET_CHUNK, (n + 1) * RET_CHUNK)
            qc = q[rows, :]
            kzc = kz[rows, :]
            s = (_dot_nt(qc, k[rows, :]) * decay_ref[h]).astype(BF16)
            rets = []
            for c0 in range(0, B_DV, RET_VALUE_COLS):
                hcols = slice(c0, c0 + RET_VALUE_COLS)
                vc = v[rows, hcols]
                state = state_scr[h, :, hcols]
                rets.append(_dot(s, vc) + _dot(qc, state.astype(BF16)) * xi_ref[h, :, hcols])
                state_scr[h, :, hcols] = state * gc_ref[h, :, hcols] + _dot_tn(kzc, vc)
            sumsq = functools.reduce(jnp.add, [jnp.sum(r * r, axis=-1, keepdims=True) for r in rets])
            scale = lax.rsqrt(sumsq * (1.0 / B_DV) + EPS)
            for i, r in enumerate(rets):
                hcols = slice(i * RET_VALUE_COLS, (i + 1) * RET_VALUE_COLS)
                cols = slice(h * B_DV + i * RET_VALUE_COLS, h * B_DV + (i + 1) * RET_VALUE_COLS)
                y = r * scale * og_ref[:, cols] * sg[rows, hcols].astype(F32)
                y_ref[rows, cols] = y.astype(BF16)
    o_ref[...] = x_ref[...] + _dot(y_ref[...], wout_ref[...])


def _retention_mixer(x2, g, w, cos, sin, zeta_rows, decay, xi, gc, og, wout):
    assert ROW_TILE % RET_CHUNK == 0
    per_seq = SEQ // ROW_TILE
    dq = B_HEADS * B_DK
    dv = B_HEADS * B_DV
    return pl.pallas_call(
        _retention_mixer_kernel,
        grid=(BATCH, per_seq),
        in_specs=[
            pl.BlockSpec((ROW_TILE, D_MODEL), lambda b, t: (b * per_seq + t, 0)),
            _const_spec((1, D_MODEL)),
            _const_spec((D_MODEL, 2 * dq + 2 * dv)),
            pl.BlockSpec((ROW_TILE, B_DK // 2), lambda b, t: (t, 0)),
            pl.BlockSpec((ROW_TILE, B_DK // 2), lambda b, t: (t, 0)),
            _const_spec((ROW_TILE, dq)),
            _const_spec((B_HEADS, RET_CHUNK, RET_CHUNK)),
            _const_spec((B_HEADS, RET_CHUNK, B_DV)),
            _const_spec((B_HEADS, 1, B_DV)),
            _const_spec((1, dv)),
            _const_spec((dv, D_MODEL)),
        ],
        out_specs=pl.BlockSpec((ROW_TILE, D_MODEL), lambda b, t: (b * per_seq + t, 0)),
        out_shape=jax.ShapeDtypeStruct((TOKENS, D_MODEL), F32),
        scratch_shapes=[pltpu.VMEM((B_HEADS, B_DK, B_DV), F32), pltpu.VMEM((ROW_TILE, dv), BF16)],
        compiler_params=_params("parallel", "arbitrary"),
        name="retention_mixer",
    )(x2, g, w, cos, sin, zeta_rows, decay, xi, gc, og, wout)


def _retention_tables():
    log_gamma = np.log1p(-(2.0 ** (-5.0 - np.arange(B_HEADS, dtype=np.float32)))).astype(np.float32)
    pos = np.arange(RET_CHUNK, dtype=np.float32)
    diff = pos[:, None] - pos[None, :]
    decay = np.where(diff[None] >= 0,
                     np.exp(np.maximum(diff, 0.0)[None] * log_gamma[:, None, None]), 0.0)
    xi = np.exp((pos + 1.0)[None, :] * log_gamma[:, None])
    zeta = np.exp((RET_CHUNK - 1.0 - pos)[None, :] * log_gamma[:, None])
    gc = np.exp(RET_CHUNK * log_gamma)
    xi_b = np.broadcast_to(xi[:, :, None], (B_HEADS, RET_CHUNK, B_DV))
    gc_b = np.broadcast_to(gc[:, None, None], (B_HEADS, 1, B_DV))
    zeta_rows = np.tile(np.repeat(zeta.T, B_DK, axis=1), (ROW_TILE // RET_CHUNK, 1))
    return tuple(jnp.asarray(a, F32) for a in (decay, xi_b, gc_b, zeta_rows))


def _rotary_tables():
    inv = (1.0 / (ROPE_BASE ** (np.arange(0, B_DK, 2, dtype=np.float32) / B_DK))).astype(np.float32)
    ang = np.arange(SEQ, dtype=np.float32)[:, None] * inv[None, :]
    return jnp.asarray(np.cos(ang), F32), jnp.asarray(np.sin(ang), F32)


def kernel(x, norm_mix_g, norm_ffn_g, a_w_in, a_q_g, a_k_g, a_iq_g, a_ik_g, a_w_out,
           b_w_in, b_out_g, b_w_out, f_w_up, f_conv_w, f_conv_b, f_w_down):
    x2 = x.reshape(TOKENS, D_MODEL)

    w = a_w_in[0]
    o1 = A_Q_COLS
    o2 = o1 + A_LAT
    o3 = o2 + IDX_HEADS * IDX_DIM
    o4 = o3 + IDX_DIM
    wck = jnp.concatenate(
        [w[:, o1:o2], w[:, o3:o4], jnp.zeros((D_MODEL, A_LAT - IDX_DIM), F32)], axis=1).astype(BF16)
    wt = jnp.concatenate([w[:, :o3], w[:, o4:]], axis=1).T.astype(BF16)
    qt, k, ct, qit, ki, wit = _proj_a(
        x2, norm_mix_g[0][None, :], wck, wt,
        a_q_g[0][:, None], a_k_g[0][None, :], a_iq_g[0][:, None], a_ik_g[0][None, :])
    ot, (w_up, w_down, wa_out, wb_in, wb_out) = _dsa(
        qt, qit, wit, ki, k, ct, [f_w_up, f_w_down, a_w_out[0], b_w_in[0], b_w_out[0]])
    x2 = _ffn(x2, ot, True, wa_out, norm_ffn_g[0][None, :],
              0, w_up, f_conv_w[0], f_conv_b[0][None, :], w_down)

    cos, sin = _rotary_tables()
    decay, xi, gc, zeta_rows = _retention_tables()
    x2 = _retention_mixer(x2, norm_mix_g[1][None, :], wb_in, cos, sin, zeta_rows, decay, xi, gc,
                          b_out_g[0][None, :], wb_out)
    x2 = _ffn(x2, None, False, None, norm_ffn_g[1][None, :],
              1, w_up, f_conv_w[1], f_conv_b[1][None, :], w_down)
    return x2.reshape(BATCH, SEQ, D_MODEL)
```
